```python
import math
import jax, jax.numpy as jnp
from jax import lax
import numpy as np

D_MODEL = 2048
BATCH = 16
SEQ = 2048
DEPTH = 1

EPS = 1e-5
D_CONV_MIX = D_MODEL // 2
CONV_A_WIDTH = 3
D_SSD = D_MODEL // 2
SSD_HEAD_DIM = 64
SSD_HEADS = D_SSD // SSD_HEAD_DIM
SSD_GROUPS = 4
HEADS_PER_GROUP = SSD_HEADS // SSD_GROUPS
SSD_STATE = 128
SSD_CONV_WIDTH = 5
SSD_CHUNK = 128
D_XBC = D_SSD + 2 * SSD_GROUPS * SSD_STATE
D_MIX = D_CONV_MIX + D_SSD
COL_SPLITS = (
    D_CONV_MIX,
    2 * D_CONV_MIX,
    3 * D_CONV_MIX,
    3 * D_CONV_MIX + D_SSD,
    3 * D_CONV_MIX + D_SSD + D_XBC,
)
IN_COLS = 3 * D_CONV_MIX + D_SSD + D_XBC + 2 * SSD_HEADS
N_EXPERTS = 32
TOP_K = 4
D_FF = D_MODEL
SWIGLU_LIMIT = 7.0
SWIGLU_ALPHA = 1.702
MOE_BLOCK = 128

kernel_name = "hybrid_conv_ssd_moe_encoder"


def rms_norm(x, w):
    xf = x.astype(jnp.float32)
    y = xf * lax.rsqrt(jnp.mean(xf * xf, axis=-1, keepdims=True) + EPS)
    return (y * w.astype(jnp.float32)).astype(x.dtype)


def depthwise_conv(u, w):
    return lax.conv_general_dilated(
        u, w.astype(u.dtype)[:, None, :], window_strides=(1,), padding="SAME",
        dimension_numbers=("NWC", "WIO", "NWC"), feature_group_count=u.shape[-1])


def ssd_scan(xs, dt, a, bm, cm):
    bsz, seq = xs.shape[:2]
    nc = seq // SSD_CHUNK
    xc = xs.reshape(bsz, nc, SSD_CHUNK, SSD_GROUPS, HEADS_PER_GROUP, SSD_HEAD_DIM)
    dtc = dt.reshape(bsz, nc, SSD_CHUNK, SSD_GROUPS, HEADS_PER_GROUP)
    bc = bm.reshape(bsz, nc, SSD_CHUNK, SSD_GROUPS, SSD_STATE)
    cc = cm.reshape(bsz, nc, SSD_CHUNK, SSD_GROUPS, SSD_STATE)
    a_cum = jnp.cumsum(dtc * a.reshape(SSD_GROUPS, HEADS_PER_GROUP), axis=2)
    x_dt = xc * dtc[..., None]
    seg = a_cum[:, :, :, None] - a_cum[:, :, None, :]
    causal_in_chunk = (jnp.arange(SSD_CHUNK)[:, None] >= jnp.arange(SSD_CHUNK)[None, :])
    decay = jnp.exp(jnp.where(causal_in_chunk[None, None, :, :, None, None], seg, -jnp.inf))
    cb = jnp.einsum("bclgn,bcsgn->bclsg", cc, bc)
    y_diag = jnp.einsum("bclsg,bclsgr,bcsgrp->bclgrp", cb, decay, x_dt)
    decay_to_end = jnp.exp(a_cum[:, :, -1:] - a_cum)
    chunk_states = jnp.einsum("bcsgn,bcsgr,bcsgrp->bcgrpn", bc, decay_to_end, x_dt)
    chunk_decay = jnp.exp(a_cum[:, :, -1])

    def step(state, inp):
        st_c, dec_c = inp
        return state * dec_c[..., None, None] + st_c, state

    init = jnp.zeros_like(chunk_states[:, 0])
    _, prev = lax.scan(step, init, (jnp.moveaxis(chunk_states, 1, 0), jnp.moveaxis(chunk_decay, 1, 0)))
    prev = jnp.moveaxis(prev, 0, 1)
    y_off = jnp.einsum("bclgn,bcgrpn,bclgr->bclgrp", cc, prev, jnp.exp(a_cum))
    return (y_diag + y_off).reshape(bsz, seq, SSD_GROUPS, HEADS_PER_GROUP, SSD_HEAD_DIM)


def hybrid_mixer(h, w_in, conv_a_w, ssd_conv_w, ssd_conv_b, dt_bias_fw, dt_bias_bw,
                 a_log_fw, a_log_bw, d_skip, ssd_norm_w, w_out):
    bsz, seq, _ = h.shape
    proj = jnp.einsum("bsd,de->bse", h, w_in)
    gate_b, gate_c, u, z, xbc, dt_raw = jnp.split(proj, COL_SPLITS, axis=-1)
    y_a = gate_b * depthwise_conv(gate_c * u, conv_a_w)
    xbc = jax.nn.silu(depthwise_conv(xbc, ssd_conv_w) + ssd_conv_b.astype(xbc.dtype))
    xs, bm, cm = jnp.split(xbc, (D_SSD, D_SSD + SSD_GROUPS * SSD_STATE), axis=-1)
    xs = xs.reshape(bsz, seq, SSD_GROUPS, HEADS_PER_GROUP, SSD_HEAD_DIM)
    bm = bm.reshape(bsz, seq, SSD_GROUPS, SSD_STATE)
    cm = cm.reshape(bsz, seq, SSD_GROUPS, SSD_STATE)
    dt_raw = dt_raw.astype(jnp.float32)
    dt_fw = jax.nn.softplus(dt_raw[..., :SSD_HEADS] + dt_bias_fw.astype(jnp.float32))
    dt_bw = jax.nn.softplus(dt_raw[..., SSD_HEADS:] + dt_bias_bw.astype(jnp.float32))
    a_fw = -jnp.exp(a_log_fw.astype(jnp.float32))
    a_bw = -jnp.exp(a_log_bw.astype(jnp.float32))
    y_fw = ssd_scan(xs, dt_fw, a_fw, bm, cm)
    y_bw = jnp.flip(ssd_scan(jnp.flip(xs, 1), jnp.flip(dt_bw, 1), a_bw,
                             jnp.flip(bm, 1), jnp.flip(cm, 1)), 1)
    y_b = y_fw + y_bw + xs * d_skip.reshape(SSD_GROUPS, HEADS_PER_GROUP)[..., None]
    g = (y_b.reshape(bsz, seq, D_SSD) * jax.nn.silu(z)).astype(jnp.float32)
    g = g.reshape(bsz, seq, SSD_GROUPS, D_SSD // SSD_GROUPS)
    g = g * lax.rsqrt(jnp.mean(g * g, axis=-1, keepdims=True) + EPS)
    y_b = (g.reshape(bsz, seq, D_SSD) * ssd_norm_w.astype(jnp.float32)).astype(h.dtype)
    y = jnp.concatenate([y_a.astype(h.dtype), y_b], axis=-1)
    return jnp.einsum("bse,ed->bsd", y, w_out).astype(h.dtype)


def moe_ffn(h, w_router, b_router, w_gate_up, b_gate_up, w_down, b_down):
    bsz, seq, d = h.shape
    ht = h.reshape(-1, d)
    n_tok = ht.shape[0]
    logits = (ht @ w_router + b_router).astype(jnp.float32)
    top_logits, top_idx = lax.top_k(logits, TOP_K)
    gates = jax.nn.softmax(top_logits, axis=-1)
    n_slots = n_tok * TOP_K
    e_flat = top_idx.reshape(-1)
    order = jnp.argsort(e_flat)
    e_sorted = e_flat[order]
    counts = jnp.bincount(e_flat, length=N_EXPERTS)
    starts = jnp.cumsum(counts) - counts
    padded = (counts + MOE_BLOCK - 1) // MOE_BLOCK * MOE_BLOCK
    pends = jnp.cumsum(padded)
    pstarts = pends - padded
    dest = pstarts[e_sorted] + jnp.arange(n_slots, dtype=jnp.int32) - starts[e_sorted]
    n_blocks = -(-n_slots // MOE_BLOCK) + N_EXPERTS
    n_rows = n_blocks * MOE_BLOCK
    row_tok = jnp.zeros((n_rows,), jnp.int32).at[dest].set((order // TOP_K).astype(jnp.int32))
    row_gate = jnp.zeros((n_rows,), jnp.float32).at[dest].set(gates.reshape(-1)[order])
    block_e = jnp.minimum(
        jnp.searchsorted(pends, jnp.arange(n_blocks) * MOE_BLOCK, side="right"), N_EXPERTS - 1)

    def expert_block(args):
        tok, gate, e = args
        xb = ht[tok]
        gu = xb @ w_gate_up[e] + b_gate_up[e]
        x_glu = jnp.minimum(gu[:, :D_FF], SWIGLU_LIMIT)
        x_lin = jnp.clip(gu[:, D_FF:], -SWIGLU_LIMIT, SWIGLU_LIMIT)
        act = x_glu * jax.nn.sigmoid(SWIGLU_ALPHA * x_glu) * (x_lin + 1.0)
        yb = act @ w_down[e] + b_down[e]
        return yb.astype(jnp.float32) * gate[:, None]

    ys = lax.map(expert_block, (row_tok.reshape(n_blocks, MOE_BLOCK),
                                row_gate.reshape(n_blocks, MOE_BLOCK), block_e))
    out = jax.ops.segment_sum(ys.reshape(n_rows, d), row_tok, num_segments=n_tok)
    return out.reshape(bsz, seq, d).astype(h.dtype)


def setup_inputs(seed: int = 0) -> dict:
    key = jax.random.key(seed)
    ks = jax.random.split(key, 24)
    f32 = jnp.float32
    nrm = lambda k, shape, scale: jax.random.normal(k, shape, f32) * scale

    def dt_bias(k):
        dt = jnp.exp(jax.random.uniform(k, (DEPTH, SSD_HEADS), f32)
                     * (math.log(0.1) - math.log(0.001)) + math.log(0.001))
        return dt + jnp.log(-jnp.expm1(-dt))

    def a_log(k):
        return jnp.log(jax.random.uniform(k, (DEPTH, SSD_HEADS), f32, 1.0, 16.0))

    return {
        "x": nrm(ks[0], (BATCH, SEQ, D_MODEL), 1.0),
        "norm_mix_w": 1.0 + nrm(ks[1], (DEPTH, D_MODEL), 0.02),
        "w_in": nrm(ks[2], (DEPTH, D_MODEL, IN_COLS), D_MODEL ** -0.5),
        "conv_a_w": nrm(ks[3], (DEPTH, CONV_A_WIDTH, D_CONV_MIX), CONV_A_WIDTH ** -0.5),
        "ssd_conv_w": nrm(ks[4], (DEPTH, SSD_CONV_WIDTH, D_XBC), SSD_CONV_WIDTH ** -0.5),
        "ssd_conv_b": nrm(ks[5], (DEPTH, D_XBC), 0.01),
        "dt_bias_fw": dt_bias(ks[6]),
        "dt_bias_bw": dt_bias(ks[7]),
        "a_log_fw": a_log(ks[8]),
        "a_log_bw": a_log(ks[9]),
        "d_skip": 1.0 + nrm(ks[10], (DEPTH, SSD_HEADS), 0.1),
        "ssd_norm_w": 1.0 + nrm(ks[11], (DEPTH, D_SSD), 0.02),
        "w_out": nrm(ks[12], (DEPTH, D_MIX, D_MODEL), D_MIX ** -0.5),
        "norm_ffn_w": 1.0 + nrm(ks[13], (DEPTH, D_MODEL), 0.02),
        "w_router": nrm(ks[14], (DEPTH, D_MODEL, N_EXPERTS), D_MODEL ** -0.5),
        "b_router": nrm(ks[15], (DEPTH, N_EXPERTS), 0.01),
        "w_gate_up": nrm(ks[16], (DEPTH, N_EXPERTS, D_MODEL, 2 * D_FF), D_MODEL ** -0.5),
        "b_gate_up": nrm(ks[17], (DEPTH, N_EXPERTS, 2 * D_FF), 0.01),
        "w_down": nrm(ks[18], (DEPTH, N_EXPERTS, D_FF, D_MODEL), D_FF ** -0.5),
        "b_down": nrm(ks[19], (DEPTH, N_EXPERTS, D_MODEL), 0.01),
        "norm_final_w": 1.0 + nrm(ks[20], (D_MODEL,), 0.02),
    }


def reference(x, norm_mix_w, w_in, conv_a_w, ssd_conv_w, ssd_conv_b, dt_bias_fw, dt_bias_bw,
              a_log_fw, a_log_bw, d_skip, ssd_norm_w, w_out, norm_ffn_w, w_router, b_router,
              w_gate_up, b_gate_up, w_down, b_down, norm_final_w):
    for layer in range(DEPTH):
        h = rms_norm(x, norm_mix_w[layer])
        x = x + hybrid_mixer(h, w_in[layer], conv_a_w[layer], ssd_conv_w[layer], ssd_conv_b[layer],
                             dt_bias_fw[layer], dt_bias_bw[layer], a_log_fw[layer], a_log_bw[layer],
                             d_skip[layer], ssd_norm_w[layer], w_out[layer])
        h = rms_norm(x, norm_ffn_w[layer])
        x = x + moe_ffn(h, w_router[layer], b_router[layer], w_gate_up[layer], b_gate_up[layer],
                        w_down[layer], b_down[layer])
    return rms_norm(x, norm_final_w)
```

```python
import functools

import jax
import jax.numpy as jnp
from jax import lax
from jax.experimental import pallas as pl
from jax.experimental.pallas import tpu as pltpu

F32 = jnp.float32
BF16 = jnp.bfloat16

EPS = 1e-5
SSD_HEAD_DIM = 64
SSD_GROUPS = 4
SSD_STATE = 128
SSD_CHUNK = 128
TOP_K = 4
SWIGLU_LIMIT = 7.0
SWIGLU_ALPHA = 1.702

LANES = 128
VMEM_LIMIT_BYTES = 56 * 1024 * 1024


def _largest_divisor(n, candidates):
    for c in candidates:
        if n % c == 0:
            return c
    raise ValueError(f"no tile in {candidates} divides {n}")


def _cparams(*sem):
    return pltpu.CompilerParams(dimension_semantics=tuple(sem), vmem_limit_bytes=VMEM_LIMIT_BYTES)


def _inproj_kernel(x_ref, nw_ref, w_ref, wdt_ref, o_ref, dt_ref, h_scr):
    @pl.when(pl.program_id(1) == 0)
    def _():
        x = x_ref[...]
        ms = jnp.mean(x * x, axis=-1, keepdims=True)
        h = (x * lax.rsqrt(ms + EPS) * nw_ref[...]).astype(BF16)
        h_scr[...] = h
        dt_ref[...] = jnp.dot(h, wdt_ref[...], preferred_element_type=F32)

    o_ref[...] = jnp.dot(h_scr[...], w_ref[...], preferred_element_type=F32).astype(o_ref.dtype)


def _inproj(x2, norm_w, w_main, w_dt):
    t, d = x2.shape
    n = w_main.shape[1]
    tm = _largest_divisor(t, (512, 256, 128))
    tn = _largest_divisor(n, (1536, 1024, 512, 256, 128))
    return pl.pallas_call(
        _inproj_kernel,
        grid=(t // tm, n // tn),
        in_specs=[
            pl.BlockSpec((tm, d), lambda i, j: (i, 0)),
            pl.BlockSpec((1, d), lambda i, j: (0, 0)),
            pl.BlockSpec((d, tn), lambda i, j: (0, j)),
            pl.BlockSpec((d, LANES), lambda i, j: (0, 0)),
        ],
        out_specs=[
            pl.BlockSpec((tm, tn), lambda i, j: (i, j)),
            pl.BlockSpec((tm, LANES), lambda i, j: (i, 0)),
        ],
        out_shape=[
            jax.ShapeDtypeStruct((t, n), BF16),
            jax.ShapeDtypeStruct((t, LANES), F32),
        ],
        scratch_shapes=[pltpu.VMEM((tm, d), BF16)],
        compiler_params=_cparams("parallel", "arbitrary"),
        name="inproj",
    )(x2, norm_w, w_main, w_dt)


def _centred_conv(v, w_ref):
    s = v.shape[0]
    width = w_ref.shape[0]
    half = width // 2
    row = lax.broadcasted_iota(jnp.int32, v.shape, 0)
    acc = v * w_ref[half:half + 1, :]
    for k in range(width):
        off = k - half
        if off == 0:
            continue
        shifted = pltpu.roll(v, (-off) % s, 0)
        valid = (row + off >= 0) & (row + off < s)
        acc = acc + jnp.where(valid, shifted, 0.0) * w_ref[k:k + 1, :]
    return acc


def _conv_a_kernel(gb_ref, gc_ref, u_ref, w_ref, o_ref):
    v = gc_ref[0].astype(F32) * u_ref[0].astype(F32)
    o_ref[0] = (gb_ref[0].astype(F32) * _centred_conv(v, w_ref)).astype(o_ref.dtype)


def _conv_a(proj3, conv_w, dc):
    b, s, _ = proj3.shape
    tc = _largest_divisor(dc, (512, 256, 128))
    nb = dc // tc
    blk = lambda off: pl.BlockSpec((1, s, tc), lambda i, j: (i, 0, off + j))
    return pl.pallas_call(
        _conv_a_kernel,
        grid=(b, nb),
        in_specs=[blk(0), blk(nb), blk(2 * nb),
                  pl.BlockSpec((conv_w.shape[0], tc), lambda i, j: (0, j))],
        out_specs=pl.BlockSpec((1, s, tc), lambda i, j: (i, 0, j)),
        out_shape=jax.ShapeDtypeStruct((b, s, dc), BF16),
        compiler_params=_cparams("parallel", "parallel"),
        name="conv_a",
    )(proj3, proj3, proj3, conv_w)


def _conv_ssd_kernel(x_ref, w_ref, b_ref, o_ref):
    y = _centred_conv(x_ref[0].astype(F32), w_ref) + b_ref[...]
    o_ref[0] = (y * jax.nn.sigmoid(y)).astype(o_ref.dtype)


def _conv_ssd(proj3, conv_w, conv_b, col0, dxbc):
    b, s, _ = proj3.shape
    tc = _largest_divisor(dxbc, (512, 256, 128))
    assert col0 % tc == 0
    off = col0 // tc
    return pl.pallas_call(
        _conv_ssd_kernel,
        grid=(b, dxbc // tc),
        in_specs=[pl.BlockSpec((1, s, tc), lambda i, j: (i, 0, off + j)),
                  pl.BlockSpec((conv_w.shape[0], tc), lambda i, j: (0, j)),
                  pl.BlockSpec((1, tc), lambda i, j: (0, j))],
        out_specs=pl.BlockSpec((1, s, tc), lambda i, j: (i, 0, j)),
        out_shape=jax.ShapeDtypeStruct((b, s, dxbc), BF16),
        compiler_params=_cparams("parallel", "parallel"),
        name="conv_ssd",
    )(proj3, conv_w, conv_b)


def _expand_heads(arr, cols):
    l = arr.shape[0]
    lane = lax.broadcasted_iota(jnp.int32, (l, LANES), 1)
    outs = []
    for p in range(0, len(cols), 2):
        b0 = jnp.broadcast_to(arr[:, cols[p]:cols[p] + 1], (l, LANES))
        b1 = jnp.broadcast_to(arr[:, cols[p + 1]:cols[p + 1] + 1], (l, LANES))
        outs.append(jnp.where(lane < SSD_HEAD_DIM, b0, b1))
    return outs[0] if len(outs) == 1 else jnp.concatenate(outs, axis=1)


def _ssd_kernel(xs_ref, b_ref, c_ref, z_ref, dt_ref, bias_ref, alog_ref, dskip_ref, nw_ref, o_ref,
                dt_scr, a_scr, q_scr, e_scr, tot_scr, at_scr, dtt_scr, st_scr, *, hpg):
    s = xs_ref.shape[1]
    gw = xs_ref.shape[2]
    nc = s // SSD_CHUNK
    L = SSD_CHUNK
    fw_cols = list(range(hpg))
    bw_cols = list(range(hpg, 2 * hpg))

    raw = dt_ref[0, 0] + bias_ref[0]
    dt = jnp.maximum(raw, 0.0) + jnp.log1p(jnp.exp(-jnp.abs(raw)))
    dta = dt * (-jnp.exp(alog_ref[0]))
    row = lax.broadcasted_iota(jnp.int32, (s, LANES), 0)
    pos = row & (L - 1)
    lane = lax.broadcasted_iota(jnp.int32, (s, LANES), 1)
    pre = dta
    suf = dta
    sh = 1
    while sh < L:
        pre = pre + jnp.where(pos >= sh, pltpu.roll(pre, sh, 0), 0.0)
        suf = suf + jnp.where(pos < L - sh, pltpu.roll(suf, s - sh, 0), 0.0)
        sh *= 2
    is_fw = lane < hpg
    acum = jnp.where(is_fw, pre, suf)
    tot = pre + suf - dta
    dt_scr[...] = dt
    a_scr[...] = acum
    q_scr[...] = dt * jnp.exp(tot - acum)
    e_scr[...] = jnp.exp(acum)
    tot_scr[...] = jnp.exp(tot)

    def chunk_rows(c):
        return pl.ds(pl.multiple_of(c * L, L), L)

    def phase1(c, carry):
        rows = chunk_rows(c)
        x = xs_ref[0, rows, :].astype(F32)
        q = q_scr[rows, :]
        xw = jnp.concatenate([x * _expand_heads(q, fw_cols), x * _expand_heads(q, bw_cols)], axis=1)
        st_scr[c] = lax.dot_general(b_ref[0, rows, :], xw.astype(BF16), (((0,), (0,)), ((), ())),
                                    preferred_element_type=F32)
        at_scr[c] = a_scr[rows, :].T
        dtt_scr[c] = dt_scr[rows, :].T
        return carry

    lax.fori_loop(0, nc, phase1, 0)

    def scan_fw(c, run):
        rows = chunk_rows(c)
        contrib = st_scr[c, :, 0:gw]
        st_scr[c, :, 0:gw] = run
        return run * _expand_heads(tot_scr[rows, :], fw_cols) + contrib

    lax.fori_loop(0, nc, scan_fw, jnp.zeros((SSD_STATE, gw), F32))

    def scan_bw(i, run):
        c = nc - 1 - i
        rows = chunk_rows(c)
        contrib = st_scr[c, :, gw:2 * gw]
        st_scr[c, :, gw:2 * gw] = run
        return run * _expand_heads(tot_scr[rows, :], bw_cols) + contrib

    lax.fori_loop(0, nc, scan_bw, jnp.zeros((SSD_STATE, gw), F32))

    li = lax.broadcasted_iota(jnp.int32, (L, L), 0)
    si = lax.broadcasted_iota(jnp.int32, (L, L), 1)
    causal = li >= si
    anti = si >= li

    def phase3(c, carry):
        rows = chunk_rows(c)
        xb = xs_ref[0, rows, :]
        cm = c_ref[0, rows, :]
        cb = lax.dot_general(cm, b_ref[0, rows, :], (((1,), (1,)), ((), ())), preferred_element_type=F32)
        a = a_scr[rows, :]
        at = at_scr[c]
        dtt = dtt_scr[c]
        lane_l = lax.broadcasted_iota(jnp.int32, (L, LANES), 1)
        ys = []
        for p in range(hpg // 2):
            xpair = xb[:, p * LANES:(p + 1) * LANES]
            y_pair = None
            for q in range(2):
                r = 2 * p + q
                kf, kb = fw_cols[r], bw_cols[r]
                col_f = jnp.broadcast_to(a[:, kf:kf + 1], (L, L))
                col_b = jnp.broadcast_to(a[:, kb:kb + 1], (L, L))
                m_f = jnp.where(causal, jnp.exp(col_f - at[kf:kf + 1, :]), 0.0) * dtt[kf:kf + 1, :]
                m_b = jnp.where(anti, jnp.exp(col_b - at[kb:kb + 1, :]), 0.0) * dtt[kb:kb + 1, :]
                m = (cb * (m_f + m_b)).astype(BF16)
                in_head = (lane_l < SSD_HEAD_DIM) if q == 0 else (lane_l >= SSD_HEAD_DIM)
                part = jnp.dot(m, jnp.where(in_head, xpair, jnp.zeros_like(xpair)),
                               preferred_element_type=F32)
                y_pair = part if y_pair is None else y_pair + part
            ys.append(y_pair)
        y = ys[0] if len(ys) == 1 else jnp.concatenate(ys, axis=1)
        cs = jnp.dot(cm, st_scr[c].astype(BF16), preferred_element_type=F32)
        e = e_scr[rows, :]
        y = y + cs[:, 0:gw] * _expand_heads(e, fw_cols) + cs[:, gw:2 * gw] * _expand_heads(e, bw_cols)
        y = y + xb.astype(F32) * dskip_ref[...]
        z = z_ref[0, rows, :].astype(F32)
        g = y * (z * jax.nn.sigmoid(z))
        g = g * lax.rsqrt(jnp.mean(g * g, axis=-1, keepdims=True) + EPS)
        o_ref[0, rows, :] = (g * nw_ref[...]).astype(o_ref.dtype)
        return carry

    lax.fori_loop(0, nc, phase3, 0)


def _ssd(xbc_act, proj3, dt_g, bias_g, alog_g, dskip, norm_w, ds, z_col0):
    b, s, _ = xbc_act.shape
    g = SSD_GROUPS
    gw = ds // g
    hpg = gw // SSD_HEAD_DIM
    nc = s // SSD_CHUNK
    n = SSD_STATE
    assert z_col0 % gw == 0 and ds % n == 0
    kernel = functools.partial(_ssd_kernel, hpg=hpg)
    return pl.pallas_call(
        kernel,
        grid=(b, g),
        in_specs=[
            pl.BlockSpec((1, s, gw), lambda i, j: (i, 0, j)),
            pl.BlockSpec((1, s, n), lambda i, j: (i, 0, ds // n + j)),
            pl.BlockSpec((1, s, n), lambda i, j: (i, 0, ds // n + g + j)),
            pl.BlockSpec((1, s, gw), lambda i, j: (i, 0, z_col0 // gw + j)),
            pl.BlockSpec((1, 1, s, LANES), lambda i, j: (i, j, 0, 0)),
            pl.BlockSpec((1, 1, LANES), lambda i, j: (j, 0, 0)),
            pl.BlockSpec((1, 1, LANES), lambda i, j: (j, 0, 0)),
            pl.BlockSpec((1, gw), lambda i, j: (0, j)),
            pl.BlockSpec((1, gw), lambda i, j: (0, j)),
        ],
        out_specs=pl.BlockSpec((1, s, gw), lambda i, j: (i, 0, j)),
        out_shape=jax.ShapeDtypeStruct((b, s, ds), BF16),
        scratch_shapes=[
            pltpu.VMEM((s, LANES), F32),
            pltpu.VMEM((s, LANES), F32),
            pltpu.VMEM((s, LANES), F32),
            pltpu.VMEM((s, LANES), F32),
            pltpu.VMEM((s, LANES), F32),
            pltpu.VMEM((nc, LANES, SSD_CHUNK), F32),
            pltpu.VMEM((nc, LANES, SSD_CHUNK), F32),
            pltpu.VMEM((nc, n, 2 * gw), F32),
        ],
        compiler_params=_cparams("parallel", "parallel"),
        name="ssd",
    )(xbc_act, xbc_act, xbc_act, proj3, dt_g, bias_g, alog_g, dskip, norm_w)


def _split_bf16(a):
    hi = a.astype(BF16)
    lo = (a - hi.astype(F32)).astype(BF16)
    return hi, lo


def _outproj_kernel(ya_ref, yb_ref, x_ref, wa_ref, wb_ref, nw_ref, wr_ref, br_ref,
                    x1_ref, h_ref, route_ref, cnt_ref, carry_scr, *, n_experts):
    i = pl.program_id(0)
    tm = x_ref.shape[0]

    @pl.when(i == 0)
    def _():
        carry_scr[...] = jnp.zeros_like(carry_scr)

    x1 = x_ref[...] + jnp.dot(ya_ref[...], wa_ref[...], preferred_element_type=F32) \
        + jnp.dot(yb_ref[...], wb_ref[...], preferred_element_type=F32)
    x1_ref[...] = x1
    h = x1 * lax.rsqrt(jnp.mean(x1 * x1, axis=-1, keepdims=True) + EPS) * nw_ref[...]
    h_ref[...] = h

    h_hi, h_lo = _split_bf16(h)
    w_hi, w_lo = _split_bf16(wr_ref[...])
    logits = (jnp.dot(h_hi, w_hi, preferred_element_type=F32)
              + jnp.dot(h_hi, w_lo, preferred_element_type=F32)
              + jnp.dot(h_lo, w_hi, preferred_element_type=F32)) + br_ref[...]

    lane = lax.broadcasted_iota(jnp.int32, (tm, LANES), 1)
    neg = jnp.finfo(F32).min
    work = jnp.where(lane < n_experts, logits, neg)
    tops, idxs, sels = [], [], []
    for _k in range(TOP_K):
        m = jnp.max(work, axis=-1, keepdims=True)
        idx = jnp.min(jnp.where(work == m, lane, LANES), axis=-1, keepdims=True)
        sel = lane == idx
        work = jnp.where(sel, neg, work)
        tops.append(m)
        idxs.append(idx)
        sels.append(sel)
    exps = [jnp.exp(t - tops[0]) for t in tops]
    denom = exps[0]
    for e in exps[1:]:
        denom = denom + e
    inv = 1.0 / denom

    onehot = jnp.zeros((tm, LANES), F32)
    for sel in sels:
        onehot = onehot + jnp.where(sel, 1.0, 0.0)
    ri = lax.broadcasted_iota(jnp.int32, (tm, tm), 0)
    ci = lax.broadcasted_iota(jnp.int32, (tm, tm), 1)
    tri = jnp.where(ri > ci, 1.0, 0.0).astype(BF16)
    carry = carry_scr[0:1, :]
    prefix = jnp.dot(tri, onehot.astype(BF16), preferred_element_type=F32) + carry
    new_carry = carry + jnp.sum(onehot, axis=0, keepdims=True)
    carry_scr[...] = jnp.broadcast_to(new_carry, carry_scr.shape)
    cnt_ref[...] = jnp.broadcast_to(new_carry, cnt_ref.shape)

    route = jnp.zeros((tm, LANES), F32)
    for k in range(TOP_K):
        rank = jnp.sum(jnp.where(sels[k], prefix, 0.0), axis=-1, keepdims=True)
        route = jnp.where(lane == k, idxs[k].astype(F32), route)
        route = jnp.where(lane == TOP_K + k, rank, route)
        route = jnp.where(lane == 2 * TOP_K + k, exps[k] * inv, route)
    route_ref[...] = route


def _outproj_route(ya, yb, x2, wa, wb, norm_w, w_router, b_router, n_experts):
    t, d = x2.shape
    dc = ya.shape[1]
    ds = yb.shape[1]
    tm = _largest_divisor(t, (512, 256, 128))
    kernel = functools.partial(_outproj_kernel, n_experts=n_experts)
    const = lambda shape: pl.BlockSpec(shape, lambda i: (0, 0))
    return pl.pallas_call(
        kernel,
        grid=(t // tm,),
        in_specs=[
            pl.BlockSpec((tm, dc), lambda i: (i, 0)),
            pl.BlockSpec((tm, ds), lambda i: (i, 0)),
            pl.BlockSpec((tm, d), lambda i: (i, 0)),
            const((dc, d)), const((ds, d)), const((1, d)), const((d, LANES)), const((1, LANES)),
        ],
        out_specs=[
            pl.BlockSpec((tm, d), lambda i: (i, 0)),
            pl.BlockSpec((tm, d), lambda i: (i, 0)),
            pl.BlockSpec((tm, LANES), lambda i: (i, 0)),
            pl.BlockSpec((8, LANES), lambda i: (0, 0)),
        ],
        out_shape=[
            jax.ShapeDtypeStruct((t, d), F32),
            jax.ShapeDtypeStruct((t, d), F32),
            jax.ShapeDtypeStruct((t, LANES), F32),
            jax.ShapeDtypeStruct((8, LANES), F32),
        ],
        scratch_shapes=[pltpu.VMEM((8, LANES), F32)],
        compiler_params=_cparams("arbitrary"),
        name="outproj_route",
    )(ya, yb, x2, wa, wb, norm_w, w_router, b_router)


def _dispatch_kernel(dest_ref, h_ref, zeros_ref, o_ref, sem):
    del zeros_ref
    tt = h_ref.shape[0]

    def issue(t, carry):
        for k in range(TOP_K):
            pltpu.make_async_copy(h_ref.at[pl.ds(t, 1)], o_ref.at[pl.ds(dest_ref[t * TOP_K + k], 1)],
                                  sem).start()
        return carry

    lax.fori_loop(0, tt, issue, 0)
    for _k in range(TOP_K):
        pltpu.make_async_copy(h_ref, o_ref.at[pl.ds(0, tt)], sem).wait()


def _dispatch(dest_flat, h, n_rows):
    t, d = h.shape
    tt = 256
    assert t % tt == 0
    zeros = jnp.zeros((n_rows, d), h.dtype)
    return pl.pallas_call(
        _dispatch_kernel,
        grid=(t // tt,),
        in_specs=[
            pl.BlockSpec((tt * TOP_K,), lambda i: (i,), memory_space=pltpu.SMEM),
            pl.BlockSpec((tt, d), lambda i: (i, 0)),
            pl.BlockSpec(memory_space=pl.ANY),
        ],
        out_specs=pl.BlockSpec(memory_space=pl.ANY),
        out_shape=jax.ShapeDtypeStruct((n_rows, d), h.dtype),
        scratch_shapes=[pltpu.SemaphoreType.DMA(())],
        input_output_aliases={2: 0},
        compiler_params=_cparams("arbitrary"),
        name="dispatch",
    )(dest_flat, h, zeros)


def _expert_kernel(te_ref, nu_ref, x_ref, wg_ref, wl_ref, bg_ref, bl_ref, wd_ref, bd_ref, o_ref,
                   xb_scr, acc_scr):
    del te_ref
    i = pl.program_id(0)
    j = pl.program_id(1)
    nj = pl.num_programs(1)

    @pl.when(i < nu_ref[0])
    def _():
        @pl.when(j == 0)
        def _():
            xb_scr[...] = x_ref[...].astype(BF16)
            acc_scr[...] = jnp.zeros_like(acc_scr)

        xb = xb_scr[...]
        glu = jnp.dot(xb, wg_ref[...], preferred_element_type=F32) + bg_ref[...]
        lin = jnp.dot(xb, wl_ref[...], preferred_element_type=F32) + bl_ref[...]
        glu = jnp.minimum(glu, SWIGLU_LIMIT)
        lin = jnp.clip(lin, -SWIGLU_LIMIT, SWIGLU_LIMIT)
        act = glu * jax.nn.sigmoid(SWIGLU_ALPHA * glu) * (lin + 1.0)
        acc_scr[...] += jnp.dot(act.astype(BF16), wd_ref[...], preferred_element_type=F32)

        @pl.when(j == nj - 1)
        def _():
            o_ref[...] = (acc_scr[...] + bd_ref[...]).astype(o_ref.dtype)

    @pl.when((i >= nu_ref[0]) & (j == 0))
    def _():
        o_ref[...] = jnp.zeros_like(o_ref)


def _experts(tile_e, n_used, x_sorted, w_gu, b_gu, w_d, b_d, tm):
    n_rows, d = x_sorted.shape
    f = w_d.shape[1]
    tf = _largest_divisor(f, (512, 256, 128))
    nj = f // tf
    n_tiles = n_rows // tm

    def row_map(i, j, te, nu):
        return (jnp.minimum(i, nu[0] - 1), 0)

    def jj(i, j, nu):
        return jnp.where(i < nu[0], j, nj - 1)

    grid_spec = pltpu.PrefetchScalarGridSpec(
        num_scalar_prefetch=2,
        grid=(n_tiles, nj),
        in_specs=[
            pl.BlockSpec((tm, d), row_map),
            pl.BlockSpec((None, d, tf), lambda i, j, te, nu: (te[i], 0, jj(i, j, nu))),
            pl.BlockSpec((None, d, tf), lambda i, j, te, nu: (te[i], 0, nj + jj(i, j, nu))),
            pl.BlockSpec((None, 1, tf), lambda i, j, te, nu: (te[i], 0, jj(i, j, nu))),
            pl.BlockSpec((None, 1, tf), lambda i, j, te, nu: (te[i], 0, nj + jj(i, j, nu))),
            pl.BlockSpec((None, tf, d), lambda i, j, te, nu: (te[i], jj(i, j, nu), 0)),
            pl.BlockSpec((None, 1, d), lambda i, j, te, nu: (te[i], 0, 0)),
        ],
        out_specs=pl.BlockSpec((tm, d), lambda i, j, te, nu: (i, 0)),
        scratch_shapes=[pltpu.VMEM((tm, d), BF16), pltpu.VMEM((tm, d), F32)],
    )
    return pl.pallas_call(
        _expert_kernel,
        grid_spec=grid_spec,
        out_shape=jax.ShapeDtypeStruct((n_rows, d), F32),
        compiler_params=_cparams("arbitrary", "arbitrary"),
        name="experts",
    )(tile_e, n_used, x_sorted, w_gu, w_gu, b_gu, b_gu, w_d, b_d)


def _combine_kernel(dest_ref, y_ref, x1_ref, route_ref, nw_ref, o_ref, buf, sem, *, final_norm):
    tt = x1_ref.shape[0]

    def issue(t, carry):
        for k in range(TOP_K):
            pltpu.make_async_copy(y_ref.at[pl.ds(dest_ref[t * TOP_K + k], 1)],
                                  buf.at[pl.ds(k * tt + t, 1)], sem).start()
        return carry

    lax.fori_loop(0, tt, issue, 0)
    pltpu.make_async_copy(y_ref.at[pl.ds(0, TOP_K * tt)], buf, sem).wait()

    route = route_ref[...]
    x = x1_ref[...]
    for k in range(TOP_K):
        gate = route[:, 2 * TOP_K + k:2 * TOP_K + k + 1]
        x = x + gate * buf[k * tt:(k + 1) * tt, :]
    if final_norm:
        x = x * lax.rsqrt(jnp.mean(x * x, axis=-1, keepdims=True) + EPS) * nw_ref[...]
    o_ref[...] = x


def _combine(dest_flat, y_sorted, x1, route, norm_w, final_norm):
    t, d = x1.shape
    tt = 256
    assert t % tt == 0
    return pl.pallas_call(
        functools.partial(_combine_kernel, final_norm=final_norm),
        grid=(t // tt,),
        in_specs=[
            pl.BlockSpec((tt * TOP_K,), lambda i: (i,), memory_space=pltpu.SMEM),
            pl.BlockSpec(memory_space=pl.ANY),
            pl.BlockSpec((tt, d), lambda i: (i, 0)),
            pl.BlockSpec((tt, LANES), lambda i: (i, 0)),
            pl.BlockSpec((1, d), lambda i: (0, 0)),
        ],
        out_specs=pl.BlockSpec((tt, d), lambda i: (i, 0)),
        out_shape=jax.ShapeDtypeStruct((t, d), F32),
        scratch_shapes=[pltpu.VMEM((TOP_K * tt, d), F32), pltpu.SemaphoreType.DMA(())],
        compiler_params=_cparams("arbitrary"),
        name="combine",
    )(dest_flat, y_sorted, x1, route, norm_w)


MOE_ROW_TILE = 512


def _layer(x2, bsz, seq, norm_mix_w, w_in, conv_a_w, ssd_conv_w, ssd_conv_b, dt_bias_fw, dt_bias_bw,
           a_log_fw, a_log_bw, d_skip, ssd_norm_w, w_out, norm_ffn_w, w_router, b_router,
           w_gate_up, b_gate_up, w_down, b_down):
    t, d = x2.shape
    dc = conv_a_w.shape[1]
    ds = ssd_norm_w.shape[0]
    dxbc = ssd_conv_w.shape[1]
    heads = dt_bias_fw.shape[0]
    g = SSD_GROUPS
    hpg = heads // g
    n_main = 3 * dc + ds + dxbc
    n_experts = w_router.shape[1]

    w_main = w_in[:, :n_main].astype(BF16)
    w_dt = jnp.pad(w_in[:, n_main:], ((0, 0), (0, LANES - 2 * heads))).astype(BF16)
    proj, dt_raw = _inproj(x2, norm_mix_w.reshape(1, d), w_main, w_dt)
    proj3 = proj.reshape(bsz, seq, n_main)
    y_a = _conv_a(proj3, conv_a_w, dc)
    xbc_act = _conv_ssd(proj3, ssd_conv_w, ssd_conv_b.reshape(1, dxbc), 3 * dc + ds, dxbc)

    def per_group(fw, bw):
        lead = fw.shape[:-1]
        both = jnp.concatenate([fw.reshape(*lead, g, hpg), bw.reshape(*lead, g, hpg)], axis=-1)
        return jnp.pad(both, [(0, 0)] * (both.ndim - 1) + [(0, LANES - 2 * hpg)])

    dt3 = dt_raw.reshape(bsz, seq, LANES)
    dt_g = jnp.transpose(per_group(dt3[..., :heads], dt3[..., heads:2 * heads]), (0, 2, 1, 3))
    bias_g = per_group(dt_bias_fw, dt_bias_bw).reshape(g, 1, LANES)
    alog_g = per_group(a_log_fw, a_log_bw).reshape(g, 1, LANES)
    dskip = jnp.repeat(d_skip, SSD_HEAD_DIM).reshape(1, ds)
    y_b = _ssd(xbc_act, proj3, dt_g, bias_g, alog_g, dskip, ssd_norm_w.reshape(1, ds), ds, 3 * dc)

    w_out_b = w_out.astype(BF16)
    w_r = jnp.pad(w_router, ((0, 0), (0, LANES - n_experts)))
    b_r = jnp.pad(b_router, (0, LANES - n_experts)).reshape(1, LANES)
    x1, h2, route, counts = _outproj_route(
        y_a.reshape(t, dc), y_b.reshape(t, ds), x2, w_out_b[:dc], w_out_b[dc:],
        norm_ffn_w.reshape(1, d), w_r, b_r, n_experts)

    tm = MOE_ROW_TILE
    n_slots = t * TOP_K
    n_tiles = n_slots // tm + n_experts
    cnt = counts[0, :n_experts].astype(jnp.int32)
    padded = (cnt + tm - 1) // tm * tm
    pends = jnp.cumsum(padded)
    pstarts = pends - padded
    e_idx = route[:, :TOP_K].astype(jnp.int32)
    rank = route[:, TOP_K:2 * TOP_K].astype(jnp.int32)
    dest = (pstarts[e_idx] + rank).reshape(n_slots)
    n_used = (pends[-1] // tm).astype(jnp.int32)
    tile_ids = jnp.minimum(jnp.arange(n_tiles, dtype=jnp.int32), n_used - 1)
    tile_e = jnp.minimum(jnp.searchsorted(pends, tile_ids * tm, side="right"), n_experts - 1).astype(jnp.int32)

    x_sorted = _dispatch(dest, h2, n_tiles * tm)
    f = w_down.shape[1]
    y_sorted = _experts(tile_e, n_used.reshape(1), x_sorted, w_gate_up.astype(BF16),
                        b_gate_up.reshape(n_experts, 1, 2 * f), w_down.astype(BF16),
                        b_down.reshape(n_experts, 1, d), tm)
    return dest, y_sorted, x1, route


def kernel(x, norm_mix_w, w_in, conv_a_w, ssd_conv_w, ssd_conv_b, dt_bias_fw, dt_bias_bw, a_log_fw,
           a_log_bw, d_skip, ssd_norm_w, w_out, norm_ffn_w, w_router, b_router, w_gate_up, b_gate_up,
           w_down, b_down, norm_final_w):
    bsz, seq, d = x.shape
    depth = w_in.shape[0]
    x2 = x.reshape(bsz * seq, d)
    for layer in range(depth):
        dest, y_sorted, x1, route = _layer(
            x2, bsz, seq, norm_mix_w[layer], w_in[layer], conv_a_w[layer], ssd_conv_w[layer],
            ssd_conv_b[layer], dt_bias_fw[layer], dt_bias_bw[layer], a_log_fw[layer], a_log_bw[layer],
            d_skip[layer], ssd_norm_w[layer], w_out[layer], norm_ffn_w[layer], w_router[layer],
            b_router[layer], w_gate_up[layer], b_gate_up[layer], w_down[layer], b_down[layer])
        x2 = _combine(dest, y_sorted, x1, route, norm_final_w.reshape(1, d), layer == depth - 1)
    return x2.reshape(bsz, seq, d)
```

```python
import functools

import jax
import jax.numpy as jnp
from jax import lax
from jax.experimental import pallas as pl
from jax.experimental.pallas import tpu as pltpu

F32 = jnp.float32
BF16 = jnp.bfloat16

EPS = 1e-5
SSD_HEAD_DIM = 64
SSD_GROUPS = 4
SSD_STATE = 128
SSD_CHUNK = 128
TOP_K = 4
SWIGLU_LIMIT = 7.0
SWIGLU_ALPHA = 1.702

LANES = 128
SUBLANES = 8
VMEM_LIMIT_BYTES = 56 * 1024 * 1024


def _largest_divisor(n, candidates):
    for c in candidates:
        if n % c == 0:
            return c
    raise ValueError(f"no tile in {candidates} divides {n}")


def _cparams(*sem):
    return pltpu.CompilerParams(dimension_semantics=tuple(sem), vmem_limit_bytes=VMEM_LIMIT_BYTES)


def _inproj_kernel(x_ref, nw_ref, w_ref, wdt_ref, o_ref, dt_ref, h_scr):
    @pl.when(pl.program_id(1) == 0)
    def _():
        x = x_ref[...]
        ms = jnp.mean(x * x, axis=-1, keepdims=True)
        h = (x * lax.rsqrt(ms + EPS) * nw_ref[...]).astype(BF16)
        h_scr[...] = h
        dt_ref[...] = jnp.dot(h, wdt_ref[...], preferred_element_type=F32)

    o_ref[...] = jnp.dot(h_scr[...], w_ref[...], preferred_element_type=F32).astype(o_ref.dtype)


def _inproj(x2, norm_w, w_main, w_dt):
    t, d = x2.shape
    n = w_main.shape[1]
    tm = _largest_divisor(t, (1024, 512, 256, 128))
    tn = _largest_divisor(n, (1536, 1024, 512, 256, 128))
    return pl.pallas_call(
        _inproj_kernel,
        grid=(t // tm, n // tn),
        in_specs=[
            pl.BlockSpec((tm, d), lambda i, j: (i, 0)),
            pl.BlockSpec((1, d), lambda i, j: (0, 0)),
            pl.BlockSpec((d, tn), lambda i, j: (0, j)),
            pl.BlockSpec((d, LANES), lambda i, j: (0, 0)),
        ],
        out_specs=[
            pl.BlockSpec((tm, tn), lambda i, j: (i, j)),
            pl.BlockSpec((tm, LANES), lambda i, j: (i, 0)),
        ],
        out_shape=[
            jax.ShapeDtypeStruct((t, n), BF16),
            jax.ShapeDtypeStruct((t, LANES), F32),
        ],
        scratch_shapes=[pltpu.VMEM((tm, d), BF16)],
        compiler_params=_cparams("parallel", "arbitrary"),
        name="inproj",
    )(x2, norm_w, w_main, w_dt)


def _centred_conv(v, w_ref):
    s = v.shape[0]
    width = w_ref.shape[0]
    half = width // 2
    row = lax.broadcasted_iota(jnp.int32, v.shape, 0)
    acc = v * w_ref[half:half + 1, :]
    for k in range(width):
        off = k - half
        if off == 0:
            continue
        shifted = pltpu.roll(v, (-off) % s, 0)
        valid = (row + off >= 0) & (row + off < s)
        acc = acc + jnp.where(valid, shifted, 0.0) * w_ref[k:k + 1, :]
    return acc


def _conv_a_kernel(gb_ref, gc_ref, u_ref, w_ref, o_ref):
    v = gc_ref[0].astype(F32) * u_ref[0].astype(F32)
    o_ref[0] = (gb_ref[0].astype(F32) * _centred_conv(v, w_ref)).astype(o_ref.dtype)


def _conv_a(proj3, conv_w, dc):
    b, s, _ = proj3.shape
    tc = _largest_divisor(dc, (512, 256, 128))
    nb = dc // tc
    blk = lambda off: pl.BlockSpec((1, s, tc), lambda i, j: (i, 0, off + j))
    return pl.pallas_call(
        _conv_a_kernel,
        grid=(b, nb),
        in_specs=[blk(0), blk(nb), blk(2 * nb),
                  pl.BlockSpec((conv_w.shape[0], tc), lambda i, j: (0, j))],
        out_specs=pl.BlockSpec((1, s, tc), lambda i, j: (i, 0, j)),
        out_shape=jax.ShapeDtypeStruct((b, s, dc), BF16),
        compiler_params=_cparams("parallel", "parallel"),
        name="conv_a",
    )(proj3, proj3, proj3, conv_w)


def _conv_ssd_kernel(x_ref, w_ref, b_ref, o_ref):
    y = _centred_conv(x_ref[0].astype(F32), w_ref) + b_ref[...]
    o_ref[0] = (y * jax.nn.sigmoid(y)).astype(o_ref.dtype)


def _conv_ssd(proj3, conv_w, conv_b, col0, dxbc):
    b, s, _ = proj3.shape
    tc = _largest_divisor(dxbc, (512, 256, 128))
    assert col0 % tc == 0
    off = col0 // tc
    return pl.pallas_call(
        _conv_ssd_kernel,
        grid=(b, dxbc // tc),
        in_specs=[pl.BlockSpec((1, s, tc), lambda i, j: (i, 0, off + j)),
                  pl.BlockSpec((conv_w.shape[0], tc), lambda i, j: (0, j)),
                  pl.BlockSpec((1, tc), lambda i, j: (0, j))],
        out_specs=pl.BlockSpec((1, s, tc), lambda i, j: (i, 0, j)),
        out_shape=jax.ShapeDtypeStruct((b, s, dxbc), BF16),
        compiler_params=_cparams("parallel", "parallel"),
        name="conv_ssd",
    )(proj3, conv_w, conv_b)


def _split_rows(v, passes):
    parts = []
    rem = v
    for _ in range(passes):
        term = rem.astype(BF16).astype(F32)
        parts.append(term)
        rem = rem - term
    while len(parts) % 2:
        parts.append(jnp.zeros_like(v))
    return jnp.concatenate(parts, axis=0).astype(BF16)


def _expander(n_rows, lanes_per_row, n_cols):
    r = lax.broadcasted_iota(jnp.int32, (n_rows, n_cols), 0) & (SUBLANES - 1)
    c = lax.broadcasted_iota(jnp.int32, (n_rows, n_cols), 1)
    lo = r * lanes_per_row
    return jnp.where((c >= lo) & (c < lo + lanes_per_row), 1.0, 0.0).astype(BF16)


def _expand(rows_bf16, expander):
    return lax.dot_general(rows_bf16, expander, (((0,), (0,)), ((), ())), preferred_element_type=F32)


def _ssd_kernel(xs_ref, b_ref, c_ref, z_ref, dt_ref, bias_ref, alog_ref, dskip_ref, nw_ref, o_ref,
                a_scr, dt_scr, qx_scr, ex_scr, acol_scr, st_scr, *, hpg):
    s = xs_ref.shape[1]
    gw = xs_ref.shape[2]
    nc = s // SSD_CHUNK
    L = SSD_CHUNK
    hd = SSD_HEAD_DIM

    raw = dt_ref[0, 0] + bias_ref[0]
    dt = jnp.maximum(raw, 0.0) + jnp.log1p(jnp.exp(-jnp.abs(raw)))
    dta = dt * (-jnp.exp(alog_ref[0]))
    pos = lax.broadcasted_iota(jnp.int32, (SUBLANES, s), 1) & (L - 1)
    head_row = lax.broadcasted_iota(jnp.int32, (SUBLANES, s), 0)
    pre = dta
    suf = dta
    sh = 1
    while sh < L:
        pre = pre + jnp.where(pos >= sh, pltpu.roll(pre, sh, 1), 0.0)
        suf = suf + jnp.where(pos < L - sh, pltpu.roll(suf, s - sh, 1), 0.0)
        sh *= 2
    acum = jnp.where(head_row < hpg, pre, suf)
    tot = pre + suf - dta
    for c in range(nc):
        lanes = slice(c * L, (c + 1) * L)
        a_scr[c] = acum[:, lanes]
        dt_scr[c] = dt[:, lanes]

    exp_head = _expander(2 * SUBLANES, hd, 2 * gw)
    exp_col = _expander(4 * SUBLANES, L, 2 * hpg * L)
    qx_scr[...] = _expand(_split_rows(dt * jnp.exp(tot - acum), 2), exp_head)
    ex_scr[...] = _expand(_split_rows(jnp.exp(acum), 2), exp_head)
    acol_scr[...] = _expand(_split_rows(acum, 3), exp_col)

    def phase1(i, run_b):
        c = nc - 1 - i
        row0 = pl.multiple_of(c * L, L)
        rows = pl.ds(row0, L)
        x = xs_ref[0, rows, :].astype(F32)
        qx = qx_scr[rows, :]
        xw = jnp.concatenate([x * qx[:, 0:gw], x * qx[:, gw:2 * gw]], axis=1).astype(BF16)
        contrib = lax.dot_general(b_ref[0, rows, :], xw, (((0,), (0,)), ((), ())),
                                  preferred_element_type=F32)
        st_scr[c, :, 0:gw] = contrib[:, 0:gw]
        st_scr[c, :, gw:2 * gw] = run_b
        dec_b = ex_scr[pl.ds(row0, 1), gw:2 * gw]
        return run_b * dec_b + contrib[:, gw:2 * gw]

    lax.fori_loop(0, nc, phase1, jnp.zeros((SSD_STATE, gw), F32), unroll=4)

    li = lax.broadcasted_iota(jnp.int32, (L, L), 0)
    si = lax.broadcasted_iota(jnp.int32, (L, L), 1)
    causal = li >= si
    below = li > si
    above = si > li
    lane_l = lax.broadcasted_iota(jnp.int32, (L, LANES), 1)

    def phase2(c, run_f):
        rows = pl.ds(pl.multiple_of(c * L, L), L)
        xb = xs_ref[0, rows, :]
        cm = c_ref[0, rows, :]
        cb = lax.dot_general(cm, b_ref[0, rows, :], (((1,), (1,)), ((), ())), preferred_element_type=F32)
        at = a_scr[c]
        dtt = dt_scr[c]
        acol = acol_scr[rows, :]
        ys = []
        for p in range(hpg // 2):
            xpair = xb[:, p * LANES:(p + 1) * LANES]
            y_pair = None
            for q in range(2):
                kf = 2 * p + q
                kb = hpg + kf
                seg = jnp.where(causal, acol[:, kf * L:(kf + 1) * L] - at[kf:kf + 1, :],
                                acol[:, kb * L:(kb + 1) * L] - at[kb:kb + 1, :])
                dt_f = dtt[kf:kf + 1, :]
                dt_b = dtt[kb:kb + 1, :]
                w = jnp.where(below, dt_f, jnp.where(above, dt_b, dt_f + dt_b))
                m = (cb * jnp.exp(seg) * w).astype(BF16)
                in_head = (lane_l < hd) if q == 0 else (lane_l >= hd)
                part = jnp.dot(m, jnp.where(in_head, xpair, jnp.zeros_like(xpair)),
                               preferred_element_type=F32)
                y_pair = part if y_pair is None else y_pair + part
            ys.append(y_pair)
        y = ys[0] if len(ys) == 1 else jnp.concatenate(ys, axis=1)
        ex = ex_scr[rows, :]
        cs_f = jnp.dot(cm, run_f.astype(BF16), preferred_element_type=F32)
        cs_b = jnp.dot(cm, st_scr[c, :, gw:2 * gw].astype(BF16), preferred_element_type=F32)
        y = y + cs_f * ex[:, 0:gw] + cs_b * ex[:, gw:2 * gw]
        y = y + xb.astype(F32) * dskip_ref[...]
        z = z_ref[0, rows, :].astype(F32)
        g = y * (z * jax.nn.sigmoid(z))
        g = g * lax.rsqrt(jnp.mean(g * g, axis=-1, keepdims=True) + EPS)
        o_ref[0, rows, :] = (g * nw_ref[...]).astype(o_ref.dtype)
        dec_f = ex[L - 1:L, 0:gw]
        return run_f * dec_f + st_scr[c, :, 0:gw]

    lax.fori_loop(0, nc, phase2, jnp.zeros((SSD_STATE, gw), F32), unroll=2)


def _ssd(xbc_act, proj3, dt_rows, bias_col, alog_col, dskip, norm_w, ds, z_col0):
    b, s, _ = xbc_act.shape
    g = SSD_GROUPS
    gw = ds // g
    hpg = gw // SSD_HEAD_DIM
    nc = s // SSD_CHUNK
    n = SSD_STATE
    L = SSD_CHUNK
    assert z_col0 % gw == 0 and ds % n == 0 and hpg % 2 == 0 and 2 * hpg <= SUBLANES
    kernel = functools.partial(_ssd_kernel, hpg=hpg)
    return pl.pallas_call(
        kernel,
        grid=(b, g),
        in_specs=[
            pl.BlockSpec((1, s, gw), lambda i, j: (i, 0, j)),
            pl.BlockSpec((1, s, n), lambda i, j: (i, 0, ds // n + j)),
            pl.BlockSpec((1, s, n), lambda i, j: (i, 0, ds // n + g + j)),
            pl.BlockSpec((1, s, gw), lambda i, j: (i, 0, z_col0 // gw + j)),
            pl.BlockSpec((1, 1, SUBLANES, s), lambda i, j: (i, j, 0, 0)),
            pl.BlockSpec((1, SUBLANES, 1), lambda i, j: (j, 0, 0)),
            pl.BlockSpec((1, SUBLANES, 1), lambda i, j: (j, 0, 0)),
            pl.BlockSpec((1, gw), lambda i, j: (0, j)),
            pl.BlockSpec((1, gw), lambda i, j: (0, j)),
        ],
        out_specs=pl.BlockSpec((1, s, gw), lambda i, j: (i, 0, j)),
        out_shape=jax.ShapeDtypeStruct((b, s, ds), BF16),
        scratch_shapes=[
            pltpu.VMEM((nc, SUBLANES, L), F32),
            pltpu.VMEM((nc, SUBLANES, L), F32),
            pltpu.VMEM((s, 2 * gw), F32),
            pltpu.VMEM((s, 2 * gw), F32),
            pltpu.VMEM((s, 2 * hpg * L), F32),
            pltpu.VMEM((nc, n, 2 * gw), F32),
        ],
        compiler_params=_cparams("parallel", "parallel"),
        name="ssd",
    )(xbc_act, xbc_act, xbc_act, proj3, dt_rows, bias_col, alog_col, dskip, norm_w)


def _split_bf16(a):
    hi = a.astype(BF16)
    lo = (a - hi.astype(F32)).astype(BF16)
    return hi, lo


def _outproj_kernel(ya_ref, yb_ref, x_ref, wa_ref, wb_ref, nw_ref, wr_ref, br_ref,
                    x1_ref, h_ref, route_ref, cnt_ref, carry_scr, *, n_experts):
    i = pl.program_id(0)
    tm = x_ref.shape[0]

    @pl.when(i == 0)
    def _():
        carry_scr[...] = jnp.zeros_like(carry_scr)

    x1 = x_ref[...] + jnp.dot(ya_ref[...], wa_ref[...], preferred_element_type=F32) \
        + jnp.dot(yb_ref[...], wb_ref[...], preferred_element_type=F32)
    x1_ref[...] = x1
    h = x1 * lax.rsqrt(jnp.mean(x1 * x1, axis=-1, keepdims=True) + EPS) * nw_ref[...]
    h_ref[...] = h

    h_hi, h_lo = _split_bf16(h)
    w_hi, w_lo = _split_bf16(wr_ref[...])
    logits = (jnp.dot(h_hi, w_hi, preferred_element_type=F32)
              + jnp.dot(h_hi, w_lo, preferred_element_type=F32)
              + jnp.dot(h_lo, w_hi, preferred_element_type=F32)) + br_ref[...]

    lane = lax.broadcasted_iota(jnp.int32, (tm, LANES), 1)
    neg = jnp.finfo(F32).min
    work = jnp.where(lane < n_experts, logits, neg)
    tops, idxs, sels = [], [], []
    for _k in range(TOP_K):
        m = jnp.max(work, axis=-1, keepdims=True)
        idx = jnp.min(jnp.where(work == m, lane, LANES), axis=-1, keepdims=True)
        sel = lane == idx
        work = jnp.where(sel, neg, work)
        tops.append(m)
        idxs.append(idx)
        sels.append(sel)
    exps = [jnp.exp(t - tops[0]) for t in tops]
    denom = exps[0]
    for e in exps[1:]:
        denom = denom + e
    inv = 1.0 / denom

    onehot = jnp.zeros((tm, LANES), F32)
    for sel in sels:
        onehot = onehot + jnp.where(sel, 1.0, 0.0)
    ri = lax.broadcasted_iota(jnp.int32, (tm, tm), 0)
    ci = lax.broadcasted_iota(jnp.int32, (tm, tm), 1)
    tri = jnp.where(ri > ci, 1.0, 0.0).astype(BF16)
    carry = carry_scr[0:1, :]
    prefix = jnp.dot(tri, onehot.astype(BF16), preferred_element_type=F32) + carry
    new_carry = carry + jnp.sum(onehot, axis=0, keepdims=True)
    carry_scr[...] = jnp.broadcast_to(new_carry, carry_scr.shape)
    cnt_ref[...] = jnp.broadcast_to(new_carry, cnt_ref.shape)

    route = jnp.zeros((tm, LANES), F32)
    for k in range(TOP_K):
        rank = jnp.sum(jnp.where(sels[k], prefix, 0.0), axis=-1, keepdims=True)
        route = jnp.where(lane == k, idxs[k].astype(F32), route)
        route = jnp.where(lane == TOP_K + k, rank, route)
        route = jnp.where(lane == 2 * TOP_K + k, exps[k] * inv, route)
    route_ref[...] = route


def _outproj_route(ya, yb, x2, wa, wb, norm_w, w_router, b_router, n_experts):
    t, d = x2.shape
    dc = ya.shape[1]
    ds = yb.shape[1]
    tm = _largest_divisor(t, (512, 256, 128))
    kernel = functools.partial(_outproj_kernel, n_experts=n_experts)
    const = lambda shape: pl.BlockSpec(shape, lambda i: (0, 0))
    return pl.pallas_call(
        kernel,
        grid=(t // tm,),
        in_specs=[
            pl.BlockSpec((tm, dc), lambda i: (i, 0)),
            pl.BlockSpec((tm, ds), lambda i: (i, 0)),
            pl.BlockSpec((tm, d), lambda i: (i, 0)),
            const((dc, d)), const((ds, d)), const((1, d)), const((d, LANES)), const((1, LANES)),
        ],
        out_specs=[
            pl.BlockSpec((tm, d), lambda i: (i, 0)),
            pl.BlockSpec((tm, d), lambda i: (i, 0)),
            pl.BlockSpec((tm, LANES), lambda i: (i, 0)),
            pl.BlockSpec((8, LANES), lambda i: (0, 0)),
        ],
        out_shape=[
            jax.ShapeDtypeStruct((t, d), F32),
            jax.ShapeDtypeStruct((t, d), F32),
            jax.ShapeDtypeStruct((t, LANES), F32),
            jax.ShapeDtypeStruct((8, LANES), F32),
        ],
        scratch_shapes=[pltpu.VMEM((8, LANES), F32)],
        compiler_params=_cparams("arbitrary"),
        name="outproj_route",
    )(ya, yb, x2, wa, wb, norm_w, w_router, b_router)


ZERO_ROWS = 256


def _dispatch_kernel(zflag_ref, dest_ref, h_ref, o_ref, zbuf, sem, zsem, *, tm):
    tt = h_ref.shape[0]
    n_tiles = o_ref.shape[0] // tm
    per_tile = tm // ZERO_ROWS

    def zero_copy(i, p):
        return pltpu.make_async_copy(zbuf, o_ref.at[pl.ds(i * tm + p * ZERO_ROWS, ZERO_ROWS)], zsem)

    @pl.when(pl.program_id(0) == 0)
    def _():
        zbuf[...] = jnp.zeros_like(zbuf)

        def start(i, carry):
            @pl.when(zflag_ref[i] != 0)
            def _():
                for p in range(per_tile):
                    zero_copy(i, p).start()
            return carry

        def wait(i, carry):
            @pl.when(zflag_ref[i] != 0)
            def _():
                for p in range(per_tile):
                    zero_copy(i, p).wait()
            return carry

        lax.fori_loop(0, n_tiles, start, 0)
        lax.fori_loop(0, n_tiles, wait, 0)

    def issue(t, carry):
        for k in range(TOP_K):
            pltpu.make_async_copy(h_ref.at[pl.ds(t, 1)], o_ref.at[pl.ds(dest_ref[t * TOP_K + k], 1)],
                                  sem).start(priority=k % 2)
        return carry

    lax.fori_loop(0, tt, issue, 0)
    for _k in range(TOP_K):
        pltpu.make_async_copy(h_ref, o_ref.at[pl.ds(0, tt)], sem).wait()


def _dispatch(zflag, dest_flat, h, n_rows, tm):
    t, d = h.shape
    tt = _largest_divisor(t, (512, 256))
    assert tm % ZERO_ROWS == 0
    grid_spec = pltpu.PrefetchScalarGridSpec(
        num_scalar_prefetch=1,
        grid=(t // tt,),
        in_specs=[
            pl.BlockSpec((tt * TOP_K,), lambda i, zf: (i,), memory_space=pltpu.SMEM),
            pl.BlockSpec((tt, d), lambda i, zf: (i, 0)),
        ],
        out_specs=pl.BlockSpec(memory_space=pl.ANY),
        scratch_shapes=[pltpu.VMEM((ZERO_ROWS, d), h.dtype), pltpu.SemaphoreType.DMA(()),
                        pltpu.SemaphoreType.DMA(())],
    )
    return pl.pallas_call(
        functools.partial(_dispatch_kernel, tm=tm),
        grid_spec=grid_spec,
        out_shape=jax.ShapeDtypeStruct((n_rows, d), h.dtype),
        compiler_params=_cparams("arbitrary"),
        name="dispatch",
    )(zflag, dest_flat, h)


def _expert_kernel(te_ref, nu_ref, x_ref, wg_ref, wl_ref, bg_ref, bl_ref, wd_ref, bd_ref, o_ref,
                   xb_scr, acc_scr):
    del te_ref
    i = pl.program_id(0)
    j = pl.program_id(1)
    nj = pl.num_programs(1)

    @pl.when(i < nu_ref[0])
    def _():
        @pl.when(j == 0)
        def _():
            xb_scr[...] = x_ref[...].astype(BF16)
            acc_scr[...] = jnp.zeros_like(acc_scr)

        xb = xb_scr[...]
        glu = jnp.dot(xb, wg_ref[...], preferred_element_type=F32) + bg_ref[...]
        lin = jnp.dot(xb, wl_ref[...], preferred_element_type=F32) + bl_ref[...]
        glu = jnp.minimum(glu, SWIGLU_LIMIT)
        lin = jnp.clip(lin, -SWIGLU_LIMIT, SWIGLU_LIMIT)
        act = glu * jax.nn.sigmoid(SWIGLU_ALPHA * glu) * (lin + 1.0)
        acc_scr[...] += jnp.dot(act.astype(BF16), wd_ref[...], preferred_element_type=F32)

        @pl.when(j == nj - 1)
        def _():
            o_ref[...] = (acc_scr[...] + bd_ref[...]).astype(o_ref.dtype)

    @pl.when((i >= nu_ref[0]) & (j == 0))
    def _():
        o_ref[...] = jnp.zeros_like(o_ref)


def _experts(tile_e, n_used, x_sorted, w_gu, b_gu, w_d, b_d, tm):
    n_rows, d = x_sorted.shape
    f = w_d.shape[1]
    tf = _largest_divisor(f, (512, 256, 128))
    nj = f // tf
    n_tiles = n_rows // tm

    def row_map(i, j, te, nu):
        return (jnp.minimum(i, nu[0] - 1), 0)

    def jj(i, j, nu):
        return jnp.where(i < nu[0], j, nj - 1)

    grid_spec = pltpu.PrefetchScalarGridSpec(
        num_scalar_prefetch=2,
        grid=(n_tiles, nj),
        in_specs=[
            pl.BlockSpec((tm, d), row_map),
            pl.BlockSpec((None, d, tf), lambda i, j, te, nu: (te[i], 0, jj(i, j, nu))),
            pl.BlockSpec((None, d, tf), lambda i, j, te, nu: (te[i], 0, nj + jj(i, j, nu))),
            pl.BlockSpec((None, 1, tf), lambda i, j, te, nu: (te[i], 0, jj(i, j, nu))),
            pl.BlockSpec((None, 1, tf), lambda i, j, te, nu: (te[i], 0, nj + jj(i, j, nu))),
            pl.BlockSpec((None, tf, d), lambda i, j, te, nu: (te[i], jj(i, j, nu), 0)),
            pl.BlockSpec((None, 1, d), lambda i, j, te, nu: (te[i], 0, 0)),
        ],
        out_specs=pl.BlockSpec((tm, d), lambda i, j, te, nu: (i, 0)),
        scratch_shapes=[pltpu.VMEM((tm, d), BF16), pltpu.VMEM((tm, d), F32)],
    )
    return pl.pallas_call(
        _expert_kernel,
        grid_spec=grid_spec,
        out_shape=jax.ShapeDtypeStruct((n_rows, d), F32),
        compiler_params=_cparams("arbitrary", "arbitrary"),
        name="experts",
    )(tile_e, n_used, x_sorted, w_gu, w_gu, b_gu, b_gu, w_d, b_d)


def _combine_kernel(dest_ref, dest_next_ref, y_ref, x1_ref, route_ref, nw_ref, o_ref, buf, sem,
                    *, final_norm):
    i = pl.program_id(0)
    n = pl.num_programs(0)
    tt = x1_ref.shape[0]
    slot = i % 2

    def gather(idx_ref, dst_slot):
        def issue(t, carry):
            for k in range(TOP_K):
                pltpu.make_async_copy(y_ref.at[pl.ds(idx_ref[t * TOP_K + k], 1)],
                                      buf.at[dst_slot, pl.ds(k * tt + t, 1)],
                                      sem.at[dst_slot]).start(priority=k % 2)
            return carry

        lax.fori_loop(0, tt, issue, 0)

    @pl.when(i == 0)
    def _():
        gather(dest_ref, slot)

    @pl.when(i + 1 < n)
    def _():
        gather(dest_next_ref, 1 - slot)

    pltpu.make_async_copy(y_ref.at[pl.ds(0, TOP_K * tt)], buf.at[slot], sem.at[slot]).wait()

    route = route_ref[...]
    x = x1_ref[...]
    for k in range(TOP_K):
        gate = route[:, 2 * TOP_K + k:2 * TOP_K + k + 1]
        x = x + gate * buf[slot, k * tt:(k + 1) * tt, :]
    if final_norm:
        x = x * lax.rsqrt(jnp.mean(x * x, axis=-1, keepdims=True) + EPS) * nw_ref[...]
    o_ref[...] = x


def _combine(dest_flat, y_sorted, x1, route, norm_w, final_norm):
    t, d = x1.shape
    tt = 256
    assert t % tt == 0
    n = t // tt
    return pl.pallas_call(
        functools.partial(_combine_kernel, final_norm=final_norm),
        grid=(n,),
        in_specs=[
            pl.BlockSpec((tt * TOP_K,), lambda i: (i,), memory_space=pltpu.SMEM),
            pl.BlockSpec((tt * TOP_K,), lambda i: (jnp.minimum(i + 1, n - 1),), memory_space=pltpu.SMEM),
            pl.BlockSpec(memory_space=pl.ANY),
            pl.BlockSpec((tt, d), lambda i: (i, 0)),
            pl.BlockSpec((tt, LANES), lambda i: (i, 0)),
            pl.BlockSpec((1, d), lambda i: (0, 0)),
        ],
        out_specs=pl.BlockSpec((tt, d), lambda i: (i, 0)),
        out_shape=jax.ShapeDtypeStruct((t, d), F32),
        scratch_shapes=[pltpu.VMEM((2, TOP_K * tt, d), F32), pltpu.SemaphoreType.DMA((2,))],
        compiler_params=_cparams("arbitrary"),
        name="combine",
    )(dest_flat, dest_flat, y_sorted, x1, route, norm_w)


MOE_ROW_TILE = 512


def _layer(x2, bsz, seq, norm_mix_w, w_in, conv_a_w, ssd_conv_w, ssd_conv_b, dt_bias_fw, dt_bias_bw,
           a_log_fw, a_log_bw, d_skip, ssd_norm_w, w_out, norm_ffn_w, w_router, b_router,
           w_gate_up, b_gate_up, w_down, b_down):
    t, d = x2.shape
    dc = conv_a_w.shape[1]
    ds = ssd_norm_w.shape[0]
    dxbc = ssd_conv_w.shape[1]
    heads = dt_bias_fw.shape[0]
    g = SSD_GROUPS
    hpg = heads // g
    n_main = 3 * dc + ds + dxbc
    n_experts = w_router.shape[1]

    w_main = w_in[:, :n_main].astype(BF16)
    w_dt = jnp.pad(w_in[:, n_main:], ((0, 0), (0, LANES - 2 * heads))).astype(BF16)
    proj, dt_raw = _inproj(x2, norm_mix_w.reshape(1, d), w_main, w_dt)
    proj3 = proj.reshape(bsz, seq, n_main)
    y_a = _conv_a(proj3, conv_a_w, dc)
    xbc_act = _conv_ssd(proj3, ssd_conv_w, ssd_conv_b.reshape(1, dxbc), 3 * dc + ds, dxbc)

    def head_rows(fw, bw):
        both = jnp.concatenate([fw.reshape(g, hpg), bw.reshape(g, hpg)], axis=1)
        return jnp.pad(both, ((0, 0), (0, SUBLANES - 2 * hpg))).reshape(g, SUBLANES, 1)

    dt_rows = dt_raw[:, :2 * heads].reshape(bsz, seq, 2, g, hpg)
    dt_rows = jnp.transpose(dt_rows, (0, 3, 2, 4, 1)).reshape(bsz, g, 2 * hpg, seq)
    dt_rows = jnp.pad(dt_rows, ((0, 0), (0, 0), (0, SUBLANES - 2 * hpg), (0, 0)))
    dskip = jnp.repeat(d_skip, SSD_HEAD_DIM).reshape(1, ds)
    y_b = _ssd(xbc_act, proj3, dt_rows, head_rows(dt_bias_fw, dt_bias_bw), head_rows(a_log_fw, a_log_bw),
               dskip, ssd_norm_w.reshape(1, ds), ds, 3 * dc)

    w_out_b = w_out.astype(BF16)
    w_r = jnp.pad(w_router, ((0, 0), (0, LANES - n_experts)))
    b_r = jnp.pad(b_router, (0, LANES - n_experts)).reshape(1, LANES)
    x1, h2, route, counts = _outproj_route(
        y_a.reshape(t, dc), y_b.reshape(t, ds), x2, w_out_b[:dc], w_out_b[dc:],
        norm_ffn_w.reshape(1, d), w_r, b_r, n_experts)

    tm = MOE_ROW_TILE
    n_slots = t * TOP_K
    n_tiles = n_slots // tm + n_experts
    cnt = counts[0, :n_experts].astype(jnp.int32)
    padded = (cnt + tm - 1) // tm * tm
    pends = jnp.cumsum(padded)
    pstarts = pends - padded
    e_idx = route[:, :TOP_K].astype(jnp.int32)
    rank = route[:, TOP_K:2 * TOP_K].astype(jnp.int32)
    dest = (pstarts[e_idx] + rank).reshape(n_slots)
    n_used = (pends[-1] // tm).astype(jnp.int32)
    tile_ids = jnp.arange(n_tiles, dtype=jnp.int32)
    tile_row0 = jnp.minimum(tile_ids, n_used - 1) * tm
    tile_e = jnp.minimum(jnp.sum((pends[None, :] <= tile_row0[:, None]).astype(jnp.int32), axis=1),
                         n_experts - 1)
    partial_last = (pends - tm) * ((cnt % tm) != 0) - (cnt % tm == 0)
    zflag = (jnp.any(partial_last[None, :] == (tile_ids * tm)[:, None], axis=1)
             | (tile_ids >= n_used)).astype(jnp.int32)

    x_sorted = _dispatch(zflag, dest, h2, n_tiles * tm, tm)
    f = w_down.shape[1]
    y_sorted = _experts(tile_e, n_used.reshape(1), x_sorted, w_gate_up.astype(BF16),
                        b_gate_up.reshape(n_experts, 1, 2 * f), w_down.astype(BF16),
                        b_down.reshape(n_experts, 1, d), tm)
    return dest, y_sorted, x1, route


def kernel(x, norm_mix_w, w_in, conv_a_w, ssd_conv_w, ssd_conv_b, dt_bias_fw, dt_bias_bw, a_log_fw,
           a_log_bw, d_skip, ssd_norm_w, w_out, norm_ffn_w, w_router, b_router, w_gate_up, b_gate_up,
           w_down, b_down, norm_final_w):
    bsz, seq, d = x.shape
    depth = w_in.shape[0]
    x2 = x.reshape(bsz * seq, d)
    for layer in range(depth):
        dest, y_sorted, x1, route = _layer(
            x2, bsz, seq, norm_mix_w[layer], w_in[layer], conv_a_w[layer], ssd_conv_w[layer],
            ssd_conv_b[layer], dt_bias_fw[layer], dt_bias_bw[layer], a_log_fw[layer], a_log_bw[layer],
            d_skip[layer], ssd_norm_w[layer], w_out[layer], norm_ffn_w[layer], w_router[layer],
            b_router[layer], w_gate_up[layer], b_gate_up[layer], w_down[layer], b_down[layer])
        x2 = _combine(dest, y_sorted, x1, route, norm_final_w.reshape(1, d), layer == depth - 1)
    return x2.reshape(bsz, seq, d)
```

```python
import functools

import jax
import jax.numpy as jnp
from jax import lax
from jax.experimental import pallas as pl
from jax.experimental.pallas import tpu as pltpu

F32 = jnp.float32
BF16 = jnp.bfloat16

EPS = 1e-5
SSD_HEAD_DIM = 64
SSD_GROUPS = 4
SSD_STATE = 128
SSD_CHUNK = 128
TOP_K = 4
SWIGLU_LIMIT = 7.0
SWIGLU_ALPHA = 1.702

LANES = 128
SUBLANES = 8
VMEM_LIMIT_BYTES = 56 * 1024 * 1024


def _largest_divisor(n, candidates):
    for c in candidates:
        if n % c == 0:
            return c
    raise ValueError(f"no tile in {candidates} divides {n}")


def _cparams(*sem):
    return pltpu.CompilerParams(dimension_semantics=tuple(sem), vmem_limit_bytes=VMEM_LIMIT_BYTES)


def _inproj_kernel(x_ref, nw_ref, w_ref, wdt_ref, o_ref, dt_ref, h_scr):
    @pl.when(pl.program_id(1) == 0)
    def _():
        x = x_ref[...]
        ms = jnp.mean(x * x, axis=-1, keepdims=True)
        h = (x * lax.rsqrt(ms + EPS) * nw_ref[...]).astype(BF16)
        h_scr[...] = h
        dt_ref[...] = jnp.dot(h, wdt_ref[...], preferred_element_type=F32)

    o_ref[...] = jnp.dot(h_scr[...], w_ref[...], preferred_element_type=F32).astype(o_ref.dtype)


def _inproj(x2, norm_w, w_main, w_dt):
    t, d = x2.shape
    n = w_main.shape[1]
    tm = _largest_divisor(t, (1024, 512, 256, 128))
    tn = _largest_divisor(n, (1536, 1024, 512, 256, 128))
    return pl.pallas_call(
        _inproj_kernel,
        grid=(t // tm, n // tn),
        in_specs=[
            pl.BlockSpec((tm, d), lambda i, j: (i, 0)),
            pl.BlockSpec((1, d), lambda i, j: (0, 0)),
            pl.BlockSpec((d, tn), lambda i, j: (0, j)),
            pl.BlockSpec((d, LANES), lambda i, j: (0, 0)),
        ],
        out_specs=[
            pl.BlockSpec((tm, tn), lambda i, j: (i, j)),
            pl.BlockSpec((tm, LANES), lambda i, j: (i, 0)),
        ],
        out_shape=[
            jax.ShapeDtypeStruct((t, n), BF16),
            jax.ShapeDtypeStruct((t, LANES), F32),
        ],
        scratch_shapes=[pltpu.VMEM((tm, d), BF16)],
        compiler_params=_cparams("parallel", "arbitrary"),
        name="inproj",
    )(x2, norm_w, w_main, w_dt)


def _centred_conv(v, w_ref):
    s = v.shape[0]
    width = w_ref.shape[0]
    half = width // 2
    row = lax.broadcasted_iota(jnp.int32, v.shape, 0)
    acc = v * w_ref[half:half + 1, :]
    for k in range(width):
        off = k - half
        if off == 0:
            continue
        shifted = pltpu.roll(v, (-off) % s, 0)
        valid = (row + off >= 0) & (row + off < s)
        acc = acc + jnp.where(valid, shifted, 0.0) * w_ref[k:k + 1, :]
    return acc


def _conv_a_kernel(gb_ref, gc_ref, u_ref, w_ref, o_ref):
    v = gc_ref[0].astype(F32) * u_ref[0].astype(F32)
    o_ref[0] = (gb_ref[0].astype(F32) * _centred_conv(v, w_ref)).astype(o_ref.dtype)


def _conv_a(proj3, conv_w, dc):
    b, s, _ = proj3.shape
    tc = _largest_divisor(dc, (512, 256, 128))
    nb = dc // tc
    blk = lambda off: pl.BlockSpec((1, s, tc), lambda i, j: (i, 0, off + j))
    return pl.pallas_call(
        _conv_a_kernel,
        grid=(b, nb),
        in_specs=[blk(0), blk(nb), blk(2 * nb),
                  pl.BlockSpec((conv_w.shape[0], tc), lambda i, j: (0, j))],
        out_specs=pl.BlockSpec((1, s, tc), lambda i, j: (i, 0, j)),
        out_shape=jax.ShapeDtypeStruct((b, s, dc), BF16),
        compiler_params=_cparams("parallel", "parallel"),
        name="conv_a",
    )(proj3, proj3, proj3, conv_w)


def _conv_ssd_kernel(x_ref, w_ref, b_ref, o_ref):
    y = _centred_conv(x_ref[0].astype(F32), w_ref) + b_ref[...]
    o_ref[0] = (y * jax.nn.sigmoid(y)).astype(o_ref.dtype)


def _conv_ssd(proj3, conv_w, conv_b, col0, dxbc):
    b, s, _ = proj3.shape
    tc = _largest_divisor(dxbc, (512, 256, 128))
    assert col0 % tc == 0
    off = col0 // tc
    return pl.pallas_call(
        _conv_ssd_kernel,
        grid=(b, dxbc // tc),
        in_specs=[pl.BlockSpec((1, s, tc), lambda i, j: (i, 0, off + j)),
                  pl.BlockSpec((conv_w.shape[0], tc), lambda i, j: (0, j)),
                  pl.BlockSpec((1, tc), lambda i, j: (0, j))],
        out_specs=pl.BlockSpec((1, s, tc), lambda i, j: (i, 0, j)),
        out_shape=jax.ShapeDtypeStruct((b, s, dxbc), BF16),
        compiler_params=_cparams("parallel", "parallel"),
        name="conv_ssd",
    )(proj3, conv_w, conv_b)


def _split_rows(v, passes):
    parts = []
    rem = v
    for _ in range(passes):
        term = rem.astype(BF16).astype(F32)
        parts.append(term)
        rem = rem - term
    while len(parts) % 2:
        parts.append(jnp.zeros_like(v))
    return jnp.concatenate(parts, axis=0).astype(BF16)


def _expander(n_rows, lanes_per_row, n_cols):
    r = lax.broadcasted_iota(jnp.int32, (n_rows, n_cols), 0) & (SUBLANES - 1)
    c = lax.broadcasted_iota(jnp.int32, (n_rows, n_cols), 1)
    lo = r * lanes_per_row
    return jnp.where((c >= lo) & (c < lo + lanes_per_row), 1.0, 0.0).astype(BF16)


def _expand(rows_bf16, expander):
    return lax.dot_general(rows_bf16, expander, (((0,), (0,)), ((), ())), preferred_element_type=F32)


def _ssd_kernel(xs_ref, b_ref, c_ref, z_ref, dt_ref, bias_ref, alog_ref, dskip_ref, nw_ref, o_ref,
                a_scr, dt_scr, qx_scr, ex_scr, acol_scr, st_scr, *, hpg):
    s = xs_ref.shape[1]
    gw = xs_ref.shape[2]
    nc = s // SSD_CHUNK
    L = SSD_CHUNK
    hd = SSD_HEAD_DIM

    raw = dt_ref[0, 0] + bias_ref[0]
    dt = jnp.maximum(raw, 0.0) + jnp.log1p(jnp.exp(-jnp.abs(raw)))
    dta = dt * (-jnp.exp(alog_ref[0]))
    pos = lax.broadcasted_iota(jnp.int32, (SUBLANES, s), 1) & (L - 1)
    head_row = lax.broadcasted_iota(jnp.int32, (SUBLANES, s), 0)
    pre = dta
    suf = dta
    sh = 1
    while sh < L:
        pre = pre + jnp.where(pos >= sh, pltpu.roll(pre, sh, 1), 0.0)
        suf = suf + jnp.where(pos < L - sh, pltpu.roll(suf, s - sh, 1), 0.0)
        sh *= 2
    acum = jnp.where(head_row < hpg, pre, suf)
    tot = pre + suf - dta
    for c in range(nc):
        lanes = slice(c * L, (c + 1) * L)
        a_scr[c] = acum[:, lanes]
        dt_scr[c] = dt[:, lanes]

    exp_head = _expander(2 * SUBLANES, hd, 2 * gw)
    exp_col = _expander(4 * SUBLANES, L, 2 * hpg * L)
    qx_scr[...] = _expand(_split_rows(dt * jnp.exp(tot - acum), 2), exp_head)
    ex_scr[...] = _expand(_split_rows(jnp.exp(acum), 2), exp_head)
    acol_scr[...] = _expand(_split_rows(acum, 3), exp_col)

    def phase1(i, run_b):
        c = nc - 1 - i
        row0 = pl.multiple_of(c * L, L)
        rows = pl.ds(row0, L)
        x = xs_ref[0, rows, :].astype(F32)
        qx = qx_scr[rows, :]
        xw = jnp.concatenate([x * qx[:, 0:gw], x * qx[:, gw:2 * gw]], axis=1).astype(BF16)
        contrib = lax.dot_general(b_ref[0, rows, :], xw, (((0,), (0,)), ((), ())),
                                  preferred_element_type=F32)
        st_scr[c, :, 0:gw] = contrib[:, 0:gw]
        st_scr[c, :, gw:2 * gw] = run_b
        dec_b = ex_scr[pl.ds(row0, 1), gw:2 * gw]
        return run_b * dec_b + contrib[:, gw:2 * gw]

    lax.fori_loop(0, nc, phase1, jnp.zeros((SSD_STATE, gw), F32), unroll=4)

    li = lax.broadcasted_iota(jnp.int32, (L, L), 0)
    si = lax.broadcasted_iota(jnp.int32, (L, L), 1)
    causal = li >= si
    below = li > si
    above = si > li
    lane_l = lax.broadcasted_iota(jnp.int32, (L, LANES), 1)

    def phase2(c, run_f):
        rows = pl.ds(pl.multiple_of(c * L, L), L)
        xb = xs_ref[0, rows, :]
        cm = c_ref[0, rows, :]
        cb = lax.dot_general(cm, b_ref[0, rows, :], (((1,), (1,)), ((), ())), preferred_element_type=F32)
        at = a_scr[c]
        dtt = dt_scr[c]
        acol = acol_scr[rows, :]
        ys = []
        for p in range(hpg // 2):
            xpair = xb[:, p * LANES:(p + 1) * LANES]
            y_pair = None
            for q in range(2):
                kf = 2 * p + q
                kb = hpg + kf
                seg = jnp.where(causal, acol[:, kf * L:(kf + 1) * L] - at[kf:kf + 1, :],
                                acol[:, kb * L:(kb + 1) * L] - at[kb:kb + 1, :])
                dt_f = dtt[kf:kf + 1, :]
                dt_b = dtt[kb:kb + 1, :]
                w = jnp.where(below, dt_f, jnp.where(above, dt_b, dt_f + dt_b))
                m = (cb * jnp.exp(seg) * w).astype(BF16)
                in_head = (lane_l < hd) if q == 0 else (lane_l >= hd)
                part = jnp.dot(m, jnp.where(in_head, xpair, jnp.zeros_like(xpair)),
                               preferred_element_type=F32)
                y_pair = part if y_pair is None else y_pair + part
            ys.append(y_pair)
        y = ys[0] if len(ys) == 1 else jnp.concatenate(ys, axis=1)
        ex = ex_scr[rows, :]
        cs_f = jnp.dot(cm, run_f.astype(BF16), preferred_element_type=F32)
        cs_b = jnp.dot(cm, st_scr[c, :, gw:2 * gw].astype(BF16), preferred_element_type=F32)
        y = y + cs_f * ex[:, 0:gw] + cs_b * ex[:, gw:2 * gw]
        y = y + xb.astype(F32) * dskip_ref[...]
        z = z_ref[0, rows, :].astype(F32)
        g = y * (z * jax.nn.sigmoid(z))
        g = g * lax.rsqrt(jnp.mean(g * g, axis=-1, keepdims=True) + EPS)
        o_ref[0, rows, :] = (g * nw_ref[...]).astype(o_ref.dtype)
        dec_f = ex[L - 1:L, 0:gw]
        return run_f * dec_f + st_scr[c, :, 0:gw]

    lax.fori_loop(0, nc, phase2, jnp.zeros((SSD_STATE, gw), F32), unroll=2)


def _ssd(xbc_act, proj3, dt_rows, bias_col, alog_col, dskip, norm_w, ds, z_col0):
    b, s, _ = xbc_act.shape
    g = SSD_GROUPS
    gw = ds // g
    hpg = gw // SSD_HEAD_DIM
    nc = s // SSD_CHUNK
    n = SSD_STATE
    L = SSD_CHUNK
    assert z_col0 % gw == 0 and ds % n == 0 and hpg % 2 == 0 and 2 * hpg <= SUBLANES
    kernel = functools.partial(_ssd_kernel, hpg=hpg)
    return pl.pallas_call(
        kernel,
        grid=(b, g),
        in_specs=[
            pl.BlockSpec((1, s, gw), lambda i, j: (i, 0, j)),
            pl.BlockSpec((1, s, n), lambda i, j: (i, 0, ds // n + j)),
            pl.BlockSpec((1, s, n), lambda i, j: (i, 0, ds // n + g + j)),
            pl.BlockSpec((1, s, gw), lambda i, j: (i, 0, z_col0 // gw + j)),
            pl.BlockSpec((1, 1, SUBLANES, s), lambda i, j: (i, j, 0, 0)),
            pl.BlockSpec((1, SUBLANES, 1), lambda i, j: (j, 0, 0)),
            pl.BlockSpec((1, SUBLANES, 1), lambda i, j: (j, 0, 0)),
            pl.BlockSpec((1, gw), lambda i, j: (0, j)),
            pl.BlockSpec((1, gw), lambda i, j: (0, j)),
        ],
        out_specs=pl.BlockSpec((1, s, gw), lambda i, j: (i, 0, j)),
        out_shape=jax.ShapeDtypeStruct((b, s, ds), BF16),
        scratch_shapes=[
            pltpu.VMEM((nc, SUBLANES, L), F32),
            pltpu.VMEM((nc, SUBLANES, L), F32),
            pltpu.VMEM((s, 2 * gw), F32),
            pltpu.VMEM((s, 2 * gw), F32),
            pltpu.VMEM((s, 2 * hpg * L), F32),
            pltpu.VMEM((nc, n, 2 * gw), F32),
        ],
        compiler_params=_cparams("parallel", "parallel"),
        name="ssd",
    )(xbc_act, xbc_act, xbc_act, proj3, dt_rows, bias_col, alog_col, dskip, norm_w)


def _split_bf16(a):
    hi = a.astype(BF16)
    lo = (a - hi.astype(F32)).astype(BF16)
    return hi, lo


def _outproj_kernel(ya_ref, yb_ref, x_ref, wa_ref, wb_ref, nw_ref, wr_ref, br_ref,
                    x1_ref, h_ref, route_ref, cnt_ref, carry_scr, *, n_experts):
    i = pl.program_id(0)
    tm = x_ref.shape[0]

    @pl.when(i == 0)
    def _():
        carry_scr[...] = jnp.zeros_like(carry_scr)

    x1 = x_ref[...] + jnp.dot(ya_ref[...], wa_ref[...], preferred_element_type=F32) \
        + jnp.dot(yb_ref[...], wb_ref[...], preferred_element_type=F32)
    x1_ref[...] = x1
    h = x1 * lax.rsqrt(jnp.mean(x1 * x1, axis=-1, keepdims=True) + EPS) * nw_ref[...]
    h_ref[...] = h

    h_hi, h_lo = _split_bf16(h)
    w_hi, w_lo = _split_bf16(wr_ref[...])
    logits = (jnp.dot(h_hi, w_hi, preferred_element_type=F32)
              + jnp.dot(h_hi, w_lo, preferred_element_type=F32)
              + jnp.dot(h_lo, w_hi, preferred_element_type=F32)) + br_ref[...]

    lane = lax.broadcasted_iota(jnp.int32, (tm, LANES), 1)
    neg = jnp.finfo(F32).min
    work = jnp.where(lane < n_experts, logits, neg)
    tops, idxs, sels = [], [], []
    for _k in range(TOP_K):
        m = jnp.max(work, axis=-1, keepdims=True)
        idx = jnp.min(jnp.where(work == m, lane, LANES), axis=-1, keepdims=True)
        sel = lane == idx
        work = jnp.where(sel, neg, work)
        tops.append(m)
        idxs.append(idx)
        sels.append(sel)
    exps = [jnp.exp(t - tops[0]) for t in tops]
    denom = exps[0]
    for e in exps[1:]:
        denom = denom + e
    inv = 1.0 / denom

    onehot = jnp.zeros((tm, LANES), F32)
    for sel in sels:
        onehot = onehot + jnp.where(sel, 1.0, 0.0)
    ri = lax.broadcasted_iota(jnp.int32, (tm, tm), 0)
    ci = lax.broadcasted_iota(jnp.int32, (tm, tm), 1)
    tri = jnp.where(ri > ci, 1.0, 0.0).astype(BF16)
    carry = carry_scr[0:1, :]
    prefix = jnp.dot(tri, onehot.astype(BF16), preferred_element_type=F32) + carry
    new_carry = carry + jnp.sum(onehot, axis=0, keepdims=True)
    carry_scr[...] = jnp.broadcast_to(new_carry, carry_scr.shape)
    cnt_ref[...] = jnp.broadcast_to(new_carry, cnt_ref.shape)

    route = jnp.zeros((tm, LANES), F32)
    for k in range(TOP_K):
        rank = jnp.sum(jnp.where(sels[k], prefix, 0.0), axis=-1, keepdims=True)
        route = jnp.where(lane == k, idxs[k].astype(F32), route)
        route = jnp.where(lane == TOP_K + k, rank, route)
        route = jnp.where(lane == 2 * TOP_K + k, exps[k] * inv, route)
    route_ref[...] = route


def _outproj_route(ya, yb, x2, wa, wb, norm_w, w_router, b_router, n_experts):
    t, d = x2.shape
    dc = ya.shape[1]
    ds = yb.shape[1]
    tm = _largest_divisor(t, (512, 256, 128))
    kernel = functools.partial(_outproj_kernel, n_experts=n_experts)
    const = lambda shape: pl.BlockSpec(shape, lambda i: (0, 0))
    return pl.pallas_call(
        kernel,
        grid=(t // tm,),
        in_specs=[
            pl.BlockSpec((tm, dc), lambda i: (i, 0)),
            pl.BlockSpec((tm, ds), lambda i: (i, 0)),
            pl.BlockSpec((tm, d), lambda i: (i, 0)),
            const((dc, d)), const((ds, d)), const((1, d)), const((d, LANES)), const((1, LANES)),
        ],
        out_specs=[
            pl.BlockSpec((tm, d), lambda i: (i, 0)),
            pl.BlockSpec((tm, d), lambda i: (i, 0)),
            pl.BlockSpec((tm, LANES), lambda i: (i, 0)),
            pl.BlockSpec((8, LANES), lambda i: (0, 0)),
        ],
        out_shape=[
            jax.ShapeDtypeStruct((t, d), F32),
            jax.ShapeDtypeStruct((t, d), F32),
            jax.ShapeDtypeStruct((t, LANES), F32),
            jax.ShapeDtypeStruct((8, LANES), F32),
        ],
        scratch_shapes=[pltpu.VMEM((8, LANES), F32)],
        compiler_params=_cparams("arbitrary"),
        name="outproj_route",
    )(ya, yb, x2, wa, wb, norm_w, w_router, b_router)


ZERO_ROWS = 256


def _dispatch_kernel(zflag_ref, dest_ref, h_ref, o_ref, zbuf, sem, zsem, *, tm):
    tt = h_ref.shape[0]
    n_tiles = o_ref.shape[0] // tm
    per_tile = tm // ZERO_ROWS

    def zero_copy(i, p):
        return pltpu.make_async_copy(zbuf, o_ref.at[pl.ds(i * tm + p * ZERO_ROWS, ZERO_ROWS)], zsem)

    @pl.when(pl.program_id(0) == 0)
    def _():
        zbuf[...] = jnp.zeros_like(zbuf)

        def start(i, carry):
            @pl.when(zflag_ref[i] != 0)
            def _():
                for p in range(per_tile):
                    zero_copy(i, p).start()
            return carry

        def wait(i, carry):
            @pl.when(zflag_ref[i] != 0)
            def _():
                for p in range(per_tile):
                    zero_copy(i, p).wait()
            return carry

        lax.fori_loop(0, n_tiles, start, 0)
        lax.fori_loop(0, n_tiles, wait, 0)

    def issue(t, carry):
        for k in range(TOP_K):
            pltpu.make_async_copy(h_ref.at[pl.ds(t, 1)], o_ref.at[pl.ds(dest_ref[t * TOP_K + k], 1)],
                                  sem).start(priority=k % 2)
        return carry

    lax.fori_loop(0, tt, issue, 0)
    for _k in range(TOP_K):
        pltpu.make_async_copy(h_ref, o_ref.at[pl.ds(0, tt)], sem).wait()


def _dispatch(zflag, dest_flat, h, n_rows, tm):
    t, d = h.shape
    tt = _largest_divisor(t, (512, 256))
    assert tm % ZERO_ROWS == 0
    grid_spec = pltpu.PrefetchScalarGridSpec(
        num_scalar_prefetch=1,
        grid=(t // tt,),
        in_specs=[
            pl.BlockSpec((tt * TOP_K,), lambda i, zf: (i,), memory_space=pltpu.SMEM),
            pl.BlockSpec((tt, d), lambda i, zf: (i, 0)),
        ],
        out_specs=pl.BlockSpec(memory_space=pl.ANY),
        scratch_shapes=[pltpu.VMEM((ZERO_ROWS, d), h.dtype), pltpu.SemaphoreType.DMA(()),
                        pltpu.SemaphoreType.DMA(())],
    )
    return pl.pallas_call(
        functools.partial(_dispatch_kernel, tm=tm),
        grid_spec=grid_spec,
        out_shape=jax.ShapeDtypeStruct((n_rows, d), h.dtype),
        compiler_params=_cparams("arbitrary"),
        name="dispatch",
    )(zflag, dest_flat, h)


def _expert_kernel(te_ref, nu_ref, x_ref, wgu_hbm, wd_hbm, bgu_ref, bd_ref, o_ref,
                   wgu_buf, wd_buf, xb_scr, act_scr, sem, *, tf):
    i = pl.program_id(0)
    f = wd_buf.shape[0]
    e = te_ref[i]
    active = i < nu_ref[0]

    @pl.when(active & ((i == 0) | (e != te_ref[jnp.maximum(i - 1, 0)])))
    def _():
        gu = pltpu.make_async_copy(wgu_hbm.at[e], wgu_buf, sem.at[0])
        dn = pltpu.make_async_copy(wd_hbm.at[e], wd_buf, sem.at[1])
        gu.start()
        dn.start()
        gu.wait()
        dn.wait()

    @pl.when(active)
    def _():
        xb_scr[...] = x_ref[...].astype(BF16)
        xb = xb_scr[...]
        for j in range(f // tf):
            gcols = slice(j * tf, (j + 1) * tf)
            lcols = slice(f + j * tf, f + (j + 1) * tf)
            glu = jnp.dot(xb, wgu_buf[:, gcols], preferred_element_type=F32) + bgu_ref[:, gcols]
            lin = jnp.dot(xb, wgu_buf[:, lcols], preferred_element_type=F32) + bgu_ref[:, lcols]
            glu = jnp.minimum(glu, SWIGLU_LIMIT)
            lin = jnp.clip(lin, -SWIGLU_LIMIT, SWIGLU_LIMIT)
            act_scr[:, gcols] = (glu * jax.nn.sigmoid(SWIGLU_ALPHA * glu) * (lin + 1.0)).astype(BF16)
        o_ref[...] = (jnp.dot(act_scr[...], wd_buf[...], preferred_element_type=F32)
                      + bd_ref[...]).astype(o_ref.dtype)

    @pl.when(jnp.logical_not(active))
    def _():
        o_ref[...] = jnp.zeros_like(o_ref)


def _experts(tile_e, n_used, x_sorted, w_gu, b_gu, w_d, b_d, tm):
    n_rows, d = x_sorted.shape
    f = w_d.shape[1]
    tf = _largest_divisor(f, (512, 256, 128))
    n_tiles = n_rows // tm
    grid_spec = pltpu.PrefetchScalarGridSpec(
        num_scalar_prefetch=2,
        grid=(n_tiles,),
        in_specs=[
            pl.BlockSpec((tm, d), lambda i, te, nu: (jnp.minimum(i, nu[0] - 1), 0)),
            pl.BlockSpec(memory_space=pl.ANY),
            pl.BlockSpec(memory_space=pl.ANY),
            pl.BlockSpec((None, 1, 2 * f), lambda i, te, nu: (te[i], 0, 0)),
            pl.BlockSpec((None, 1, d), lambda i, te, nu: (te[i], 0, 0)),
        ],
        out_specs=pl.BlockSpec((tm, d), lambda i, te, nu: (i, 0)),
        scratch_shapes=[
            pltpu.VMEM((d, 2 * f), BF16),
            pltpu.VMEM((f, d), BF16),
            pltpu.VMEM((tm, d), BF16),
            pltpu.VMEM((tm, f), BF16),
            pltpu.SemaphoreType.DMA((2,)),
        ],
    )
    return pl.pallas_call(
        functools.partial(_expert_kernel, tf=tf),
        grid_spec=grid_spec,
        out_shape=jax.ShapeDtypeStruct((n_rows, d), F32),
        compiler_params=_cparams("arbitrary"),
        name="experts",
    )(tile_e, n_used, x_sorted, w_gu, w_d, b_gu, b_d)


def _combine_kernel(dest_ref, dest_next_ref, y_ref, x1_ref, route_ref, nw_ref, o_ref, buf, sem,
                    *, final_norm):
    i = pl.program_id(0)
    n = pl.num_programs(0)
    tt = x1_ref.shape[0]
    slot = i % 2

    def gather(idx_ref, dst_slot):
        def issue(t, carry):
            for k in range(TOP_K):
                pltpu.make_async_copy(y_ref.at[pl.ds(idx_ref[t * TOP_K + k], 1)],
                                      buf.at[dst_slot, pl.ds(k * tt + t, 1)],
                                      sem.at[dst_slot]).start(priority=k % 2)
            return carry

        lax.fori_loop(0, tt, issue, 0)

    @pl.when(i == 0)
    def _():
        gather(dest_ref, slot)

    @pl.when(i + 1 < n)
    def _():
        gather(dest_next_ref, 1 - slot)

    pltpu.make_async_copy(y_ref.at[pl.ds(0, TOP_K * tt)], buf.at[slot], sem.at[slot]).wait()

    route = route_ref[...]
    x = x1_ref[...]
    for k in range(TOP_K):
        gate = route[:, 2 * TOP_K + k:2 * TOP_K + k + 1]
        x = x + gate * buf[slot, k * tt:(k + 1) * tt, :]
    if final_norm:
        x = x * lax.rsqrt(jnp.mean(x * x, axis=-1, keepdims=True) + EPS) * nw_ref[...]
    o_ref[...] = x


def _combine(dest_flat, y_sorted, x1, route, norm_w, final_norm):
    t, d = x1.shape
    tt = 256
    assert t % tt == 0
    n = t // tt
    return pl.pallas_call(
        functools.partial(_combine_kernel, final_norm=final_norm),
        grid=(n,),
        in_specs=[
            pl.BlockSpec((tt * TOP_K,), lambda i: (i,), memory_space=pltpu.SMEM),
            pl.BlockSpec((tt * TOP_K,), lambda i: (jnp.minimum(i + 1, n - 1),), memory_space=pltpu.SMEM),
            pl.BlockSpec(memory_space=pl.ANY),
            pl.BlockSpec((tt, d), lambda i: (i, 0)),
            pl.BlockSpec((tt, LANES), lambda i: (i, 0)),
            pl.BlockSpec((1, d), lambda i: (0, 0)),
        ],
        out_specs=pl.BlockSpec((tt, d), lambda i: (i, 0)),
        out_shape=jax.ShapeDtypeStruct((t, d), F32),
        scratch_shapes=[pltpu.VMEM((2, TOP_K * tt, d), F32), pltpu.SemaphoreType.DMA((2,))],
        compiler_params=_cparams("arbitrary"),
        name="combine",
    )(dest_flat, dest_flat, y_sorted, x1, route, norm_w)


MOE_ROW_TILE = 512


def _layer(x2, bsz, seq, norm_mix_w, w_in, conv_a_w, ssd_conv_w, ssd_conv_b, dt_bias_fw, dt_bias_bw,
           a_log_fw, a_log_bw, d_skip, ssd_norm_w, w_out, norm_ffn_w, w_router, b_router,
           w_gate_up, b_gate_up, w_down, b_down):
    t, d = x2.shape
    dc = conv_a_w.shape[1]
    ds = ssd_norm_w.shape[0]
    dxbc = ssd_conv_w.shape[1]
    heads = dt_bias_fw.shape[0]
    g = SSD_GROUPS
    hpg = heads // g
    n_main = 3 * dc + ds + dxbc
    n_experts = w_router.shape[1]

    w_main = w_in[:, :n_main].astype(BF16)
    w_dt = jnp.pad(w_in[:, n_main:], ((0, 0), (0, LANES - 2 * heads))).astype(BF16)
    proj, dt_raw = _inproj(x2, norm_mix_w.reshape(1, d), w_main, w_dt)
    proj3 = proj.reshape(bsz, seq, n_main)
    y_a = _conv_a(proj3, conv_a_w, dc)
    xbc_act = _conv_ssd(proj3, ssd_conv_w, ssd_conv_b.reshape(1, dxbc), 3 * dc + ds, dxbc)

    def head_rows(fw, bw):
        both = jnp.concatenate([fw.reshape(g, hpg), bw.reshape(g, hpg)], axis=1)
        return jnp.pad(both, ((0, 0), (0, SUBLANES - 2 * hpg))).reshape(g, SUBLANES, 1)

    dt_rows = dt_raw[:, :2 * heads].reshape(bsz, seq, 2, g, hpg)
    dt_rows = jnp.transpose(dt_rows, (0, 3, 2, 4, 1)).reshape(bsz, g, 2 * hpg, seq)
    dt_rows = jnp.pad(dt_rows, ((0, 0), (0, 0), (0, SUBLANES - 2 * hpg), (0, 0)))
    dskip = jnp.repeat(d_skip, SSD_HEAD_DIM).reshape(1, ds)
    y_b = _ssd(xbc_act, proj3, dt_rows, head_rows(dt_bias_fw, dt_bias_bw), head_rows(a_log_fw, a_log_bw),
               dskip, ssd_norm_w.reshape(1, ds), ds, 3 * dc)

    w_out_b = w_out.astype(BF16)
    w_r = jnp.pad(w_router, ((0, 0), (0, LANES - n_experts)))
    b_r = jnp.pad(b_router, (0, LANES - n_experts)).reshape(1, LANES)
    x1, h2, route, counts = _outproj_route(
        y_a.reshape(t, dc), y_b.reshape(t, ds), x2, w_out_b[:dc], w_out_b[dc:],
        norm_ffn_w.reshape(1, d), w_r, b_r, n_experts)

    tm = MOE_ROW_TILE
    n_slots = t * TOP_K
    n_tiles = n_slots // tm + n_experts
    cnt = counts[0, :n_experts].astype(jnp.int32)
    padded = (cnt + tm - 1) // tm * tm
    pends = jnp.cumsum(padded)
    pstarts = pends - padded
    e_idx = route[:, :TOP_K].astype(jnp.int32)
    rank = route[:, TOP_K:2 * TOP_K].astype(jnp.int32)
    dest = (pstarts[e_idx] + rank).reshape(n_slots)
    n_used = (pends[-1] // tm).astype(jnp.int32)
    tile_ids = jnp.arange(n_tiles, dtype=jnp.int32)
    tile_row0 = jnp.minimum(tile_ids, n_used - 1) * tm
    tile_e = jnp.minimum(jnp.sum((pends[None, :] <= tile_row0[:, None]).astype(jnp.int32), axis=1),
                         n_experts - 1)
    partial_last = (pends - tm) * ((cnt % tm) != 0) - (cnt % tm == 0)
    zflag = (jnp.any(partial_last[None, :] == (tile_ids * tm)[:, None], axis=1)
             | (tile_ids >= n_used)).astype(jnp.int32)

    x_sorted = _dispatch(zflag, dest, h2, n_tiles * tm, tm)
    f = w_down.shape[1]
    y_sorted = _experts(tile_e, n_used.reshape(1), x_sorted, w_gate_up.astype(BF16),
                        b_gate_up.reshape(n_experts, 1, 2 * f), w_down.astype(BF16),
                        b_down.reshape(n_experts, 1, d), tm)
    return dest, y_sorted, x1, route


def kernel(x, norm_mix_w, w_in, conv_a_w, ssd_conv_w, ssd_conv_b, dt_bias_fw, dt_bias_bw, a_log_fw,
           a_log_bw, d_skip, ssd_norm_w, w_out, norm_ffn_w, w_router, b_router, w_gate_up, b_gate_up,
           w_down, b_down, norm_final_w):
    bsz, seq, d = x.shape
    depth = w_in.shape[0]
    x2 = x.reshape(bsz * seq, d)
    for layer in range(depth):
        dest, y_sorted, x1, route = _layer(
            x2, bsz, seq, norm_mix_w[layer], w_in[layer], conv_a_w[layer], ssd_conv_w[layer],
            ssd_conv_b[layer], dt_bias_fw[layer], dt_bias_bw[layer], a_log_fw[layer], a_log_bw[layer],
            d_skip[layer], ssd_norm_w[layer], w_out[layer], norm_ffn_w[layer], w_router[layer],
            b_router[layer], w_gate_up[layer], b_gate_up[layer], w_down[layer], b_down[layer])
        x2 = _combine(dest, y_sorted, x1, route, norm_final_w.reshape(1, d), layer == depth - 1)
    return x2.reshape(bsz, seq, d)
```

```python
import functools

import jax
import jax.numpy as jnp
from jax import lax
from jax.experimental import pallas as pl
from jax.experimental.pallas import tpu as pltpu

F32 = jnp.float32
BF16 = jnp.bfloat16

EPS = 1e-5
SSD_HEAD_DIM = 64
SSD_GROUPS = 4
SSD_STATE = 128
SSD_CHUNK = 128
TOP_K = 4
SWIGLU_LIMIT = 7.0
SWIGLU_ALPHA = 1.702

LANES = 128
SUBLANES = 8
VMEM_LIMIT_BYTES = 56 * 1024 * 1024


def _largest_divisor(n, candidates):
    for c in candidates:
        if n % c == 0:
            return c
    raise ValueError(f"no tile in {candidates} divides {n}")


def _cparams(*sem):
    return pltpu.CompilerParams(dimension_semantics=tuple(sem), vmem_limit_bytes=VMEM_LIMIT_BYTES)


def _inproj_kernel(x_ref, nw_ref, w_ref, wdt_ref, o_ref, dt_ref, h_scr):
    @pl.when(pl.program_id(1) == 0)
    def _():
        x = x_ref[...]
        ms = jnp.mean(x * x, axis=-1, keepdims=True)
        h = (x * lax.rsqrt(ms + EPS) * nw_ref[...]).astype(BF16)
        h_scr[...] = h
        dt_ref[...] = jnp.dot(h, wdt_ref[...], preferred_element_type=F32)

    o_ref[...] = jnp.dot(h_scr[...], w_ref[...], preferred_element_type=F32).astype(o_ref.dtype)


def _inproj(x2, norm_w, w_main, w_dt):
    t, d = x2.shape
    n = w_main.shape[1]
    tm = _largest_divisor(t, (1024, 512, 256, 128))
    tn = _largest_divisor(n, (1536, 1024, 512, 256, 128))
    return pl.pallas_call(
        _inproj_kernel,
        grid=(t // tm, n // tn),
        in_specs=[
            pl.BlockSpec((tm, d), lambda i, j: (i, 0)),
            pl.BlockSpec((1, d), lambda i, j: (0, 0)),
            pl.BlockSpec((d, tn), lambda i, j: (0, j)),
            pl.BlockSpec((d, LANES), lambda i, j: (0, 0)),
        ],
        out_specs=[
            pl.BlockSpec((tm, tn), lambda i, j: (i, j)),
            pl.BlockSpec((tm, LANES), lambda i, j: (i, 0)),
        ],
        out_shape=[
            jax.ShapeDtypeStruct((t, n), BF16),
            jax.ShapeDtypeStruct((t, LANES), F32),
        ],
        scratch_shapes=[pltpu.VMEM((tm, d), BF16)],
        compiler_params=_cparams("parallel", "arbitrary"),
        name="inproj",
    )(x2, norm_w, w_main, w_dt)


def _centred_conv(v, w_ref):
    s = v.shape[0]
    width = w_ref.shape[0]
    half = width // 2
    row = lax.broadcasted_iota(jnp.int32, v.shape, 0)
    acc = v * w_ref[half:half + 1, :]
    for k in range(width):
        off = k - half
        if off == 0:
            continue
        shifted = pltpu.roll(v, (-off) % s, 0)
        valid = (row + off >= 0) & (row + off < s)
        acc = acc + jnp.where(valid, shifted, 0.0) * w_ref[k:k + 1, :]
    return acc


def _conv_a_kernel(gb_ref, gc_ref, u_ref, w_ref, o_ref):
    v = gc_ref[0].astype(F32) * u_ref[0].astype(F32)
    o_ref[0] = (gb_ref[0].astype(F32) * _centred_conv(v, w_ref)).astype(o_ref.dtype)


def _conv_a(proj3, conv_w, dc):
    b, s, _ = proj3.shape
    tc = _largest_divisor(dc, (512, 256, 128))
    nb = dc // tc
    blk = lambda off: pl.BlockSpec((1, s, tc), lambda i, j: (i, 0, off + j))
    return pl.pallas_call(
        _conv_a_kernel,
        grid=(b, nb),
        in_specs=[blk(0), blk(nb), blk(2 * nb),
                  pl.BlockSpec((conv_w.shape[0], tc), lambda i, j: (0, j))],
        out_specs=pl.BlockSpec((1, s, tc), lambda i, j: (i, 0, j)),
        out_shape=jax.ShapeDtypeStruct((b, s, dc), BF16),
        compiler_params=_cparams("parallel", "parallel"),
        name="conv_a",
    )(proj3, proj3, proj3, conv_w)


def _conv_ssd_kernel(x_ref, w_ref, b_ref, o_ref):
    y = _centred_conv(x_ref[0].astype(F32), w_ref) + b_ref[...]
    o_ref[0] = (y * jax.nn.sigmoid(y)).astype(o_ref.dtype)


def _conv_ssd(proj3, conv_w, conv_b, col0, dxbc):
    b, s, _ = proj3.shape
    tc = _largest_divisor(dxbc, (512, 256, 128))
    assert col0 % tc == 0
    off = col0 // tc
    return pl.pallas_call(
        _conv_ssd_kernel,
        grid=(b, dxbc // tc),
        in_specs=[pl.BlockSpec((1, s, tc), lambda i, j: (i, 0, off + j)),
                  pl.BlockSpec((conv_w.shape[0], tc), lambda i, j: (0, j)),
                  pl.BlockSpec((1, tc), lambda i, j: (0, j))],
        out_specs=pl.BlockSpec((1, s, tc), lambda i, j: (i, 0, j)),
        out_shape=jax.ShapeDtypeStruct((b, s, dxbc), BF16),
        compiler_params=_cparams("parallel", "parallel"),
        name="conv_ssd",
    )(proj3, conv_w, conv_b)


def _split_rows(v, passes):
    parts = []
    rem = v
    for _ in range(passes):
        term = rem.astype(BF16).astype(F32)
        parts.append(term)
        rem = rem - term
    while len(parts) % 2:
        parts.append(jnp.zeros_like(v))
    return jnp.concatenate(parts, axis=0).astype(BF16)


def _expander(n_rows, lanes_per_row, n_cols):
    r = lax.broadcasted_iota(jnp.int32, (n_rows, n_cols), 0) & (SUBLANES - 1)
    c = lax.broadcasted_iota(jnp.int32, (n_rows, n_cols), 1)
    lo = r * lanes_per_row
    return jnp.where((c >= lo) & (c < lo + lanes_per_row), 1.0, 0.0).astype(BF16)


def _expand(rows_bf16, expander):
    return lax.dot_general(rows_bf16, expander, (((0,), (0,)), ((), ())), preferred_element_type=F32)


def _ssd_kernel(xs_ref, b_ref, c_ref, z_ref, dt_ref, bias_ref, alog_ref, dskip_ref, nw_ref, o_ref,
                a_scr, dt_scr, qx_scr, ex_scr, acol_scr, st_scr, *, hpg):
    s = xs_ref.shape[1]
    gw = xs_ref.shape[2]
    nc = s // SSD_CHUNK
    L = SSD_CHUNK
    hd = SSD_HEAD_DIM

    raw = dt_ref[0, 0] + bias_ref[0]
    dt = jnp.maximum(raw, 0.0) + jnp.log1p(jnp.exp(-jnp.abs(raw)))
    dta = dt * (-jnp.exp(alog_ref[0]))
    pos = lax.broadcasted_iota(jnp.int32, (SUBLANES, s), 1) & (L - 1)
    head_row = lax.broadcasted_iota(jnp.int32, (SUBLANES, s), 0)
    pre = dta
    suf = dta
    sh = 1
    while sh < L:
        pre = pre + jnp.where(pos >= sh, pltpu.roll(pre, sh, 1), 0.0)
        suf = suf + jnp.where(pos < L - sh, pltpu.roll(suf, s - sh, 1), 0.0)
        sh *= 2
    acum = jnp.where(head_row < hpg, pre, suf)
    tot = pre + suf - dta
    for c in range(nc):
        lanes = slice(c * L, (c + 1) * L)
        a_scr[c] = acum[:, lanes]
        dt_scr[c] = dt[:, lanes]

    exp_head = _expander(2 * SUBLANES, hd, 2 * gw)
    exp_col = _expander(4 * SUBLANES, L, 2 * hpg * L)
    qx_scr[...] = _expand(_split_rows(dt * jnp.exp(tot - acum), 2), exp_head)
    ex_scr[...] = _expand(_split_rows(jnp.exp(acum), 2), exp_head)
    acol_scr[...] = _expand(_split_rows(acum, 3), exp_col)

    def phase1(i, run_b):
        c = nc - 1 - i
        row0 = pl.multiple_of(c * L, L)
        rows = pl.ds(row0, L)
        x = xs_ref[0, rows, :].astype(F32)
        qx = qx_scr[rows, :]
        xw = jnp.concatenate([x * qx[:, 0:gw], x * qx[:, gw:2 * gw]], axis=1).astype(BF16)
        contrib = lax.dot_general(b_ref[0, rows, :], xw, (((0,), (0,)), ((), ())),
                                  preferred_element_type=F32)
        st_scr[c, :, 0:gw] = contrib[:, 0:gw]
        st_scr[c, :, gw:2 * gw] = run_b
        dec_b = ex_scr[pl.ds(row0, 1), gw:2 * gw]
        return run_b * dec_b + contrib[:, gw:2 * gw]

    lax.fori_loop(0, nc, phase1, jnp.zeros((SSD_STATE, gw), F32), unroll=4)

    li = lax.broadcasted_iota(jnp.int32, (L, L), 0)
    si = lax.broadcasted_iota(jnp.int32, (L, L), 1)
    causal = li >= si
    below = li > si
    above = si > li
    lane_l = lax.broadcasted_iota(jnp.int32, (L, LANES), 1)

    def phase2(c, run_f):
        rows = pl.ds(pl.multiple_of(c * L, L), L)
        xb = xs_ref[0, rows, :]
        cm = c_ref[0, rows, :]
        cb = lax.dot_general(cm, b_ref[0, rows, :], (((1,), (1,)), ((), ())), preferred_element_type=F32)
        at = a_scr[c]
        dtt = dt_scr[c]
        acol = acol_scr[rows, :]
        ys = []
        for p in range(hpg // 2):
            xpair = xb[:, p * LANES:(p + 1) * LANES]
            y_pair = None
            for q in range(2):
                kf = 2 * p + q
                kb = hpg + kf
                seg = jnp.where(causal, acol[:, kf * L:(kf + 1) * L] - at[kf:kf + 1, :],
                                acol[:, kb * L:(kb + 1) * L] - at[kb:kb + 1, :])
                dt_f = dtt[kf:kf + 1, :]
                dt_b = dtt[kb:kb + 1, :]
                w = jnp.where(below, dt_f, jnp.where(above, dt_b, dt_f + dt_b))
                m = (cb * jnp.exp(seg) * w).astype(BF16)
                in_head = (lane_l < hd) if q == 0 else (lane_l >= hd)
                part = jnp.dot(m, jnp.where(in_head, xpair, jnp.zeros_like(xpair)),
                               preferred_element_type=F32)
                y_pair = part if y_pair is None else y_pair + part
            ys.append(y_pair)
        y = ys[0] if len(ys) == 1 else jnp.concatenate(ys, axis=1)
        ex = ex_scr[rows, :]
        cs_f = jnp.dot(cm, run_f.astype(BF16), preferred_element_type=F32)
        cs_b = jnp.dot(cm, st_scr[c, :, gw:2 * gw].astype(BF16), preferred_element_type=F32)
        y = y + cs_f * ex[:, 0:gw] + cs_b * ex[:, gw:2 * gw]
        y = y + xb.astype(F32) * dskip_ref[...]
        z = z_ref[0, rows, :].astype(F32)
        g = y * (z * jax.nn.sigmoid(z))
        g = g * lax.rsqrt(jnp.mean(g * g, axis=-1, keepdims=True) + EPS)
        o_ref[0, rows, :] = (g * nw_ref[...]).astype(o_ref.dtype)
        dec_f = ex[L - 1:L, 0:gw]
        return run_f * dec_f + st_scr[c, :, 0:gw]

    lax.fori_loop(0, nc, phase2, jnp.zeros((SSD_STATE, gw), F32), unroll=2)


def _ssd(xbc_act, proj3, dt_rows, bias_col, alog_col, dskip, norm_w, ds, z_col0):
    b, s, _ = xbc_act.shape
    g = SSD_GROUPS
    gw = ds // g
    hpg = gw // SSD_HEAD_DIM
    nc = s // SSD_CHUNK
    n = SSD_STATE
    L = SSD_CHUNK
    assert z_col0 % gw == 0 and ds % n == 0 and hpg % 2 == 0 and 2 * hpg <= SUBLANES
    kernel = functools.partial(_ssd_kernel, hpg=hpg)
    return pl.pallas_call(
        kernel,
        grid=(b, g),
        in_specs=[
            pl.BlockSpec((1, s, gw), lambda i, j: (i, 0, j)),
            pl.BlockSpec((1, s, n), lambda i, j: (i, 0, ds // n + j)),
            pl.BlockSpec((1, s, n), lambda i, j: (i, 0, ds // n + g + j)),
            pl.BlockSpec((1, s, gw), lambda i, j: (i, 0, z_col0 // gw + j)),
            pl.BlockSpec((1, 1, SUBLANES, s), lambda i, j: (i, j, 0, 0)),
            pl.BlockSpec((1, SUBLANES, 1), lambda i, j: (j, 0, 0)),
            pl.BlockSpec((1, SUBLANES, 1), lambda i, j: (j, 0, 0)),
            pl.BlockSpec((1, gw), lambda i, j: (0, j)),
            pl.BlockSpec((1, gw), lambda i, j: (0, j)),
        ],
        out_specs=pl.BlockSpec((1, s, gw), lambda i, j: (i, 0, j)),
        out_shape=jax.ShapeDtypeStruct((b, s, ds), BF16),
        scratch_shapes=[
            pltpu.VMEM((nc, SUBLANES, L), F32),
            pltpu.VMEM((nc, SUBLANES, L), F32),
            pltpu.VMEM((s, 2 * gw), F32),
            pltpu.VMEM((s, 2 * gw), F32),
            pltpu.VMEM((s, 2 * hpg * L), F32),
            pltpu.VMEM((nc, n, 2 * gw), F32),
        ],
        compiler_params=_cparams("parallel", "parallel"),
        name="ssd",
    )(xbc_act, xbc_act, xbc_act, proj3, dt_rows, bias_col, alog_col, dskip, norm_w)


def _split_bf16(a):
    hi = a.astype(BF16)
    lo = (a - hi.astype(F32)).astype(BF16)
    return hi, lo


def _outproj_kernel(ya_ref, yb_ref, x_ref, wa_ref, wb_ref, nw_ref, wr_ref, br_ref,
                    x1_ref, h_ref, route_ref, cnt_ref, carry_scr, *, n_experts):
    i = pl.program_id(0)
    tm = x_ref.shape[0]

    @pl.when(i == 0)
    def _():
        carry_scr[...] = jnp.zeros_like(carry_scr)

    x1 = x_ref[...] + jnp.dot(ya_ref[...], wa_ref[...], preferred_element_type=F32) \
        + jnp.dot(yb_ref[...], wb_ref[...], preferred_element_type=F32)
    x1_ref[...] = x1
    h = x1 * lax.rsqrt(jnp.mean(x1 * x1, axis=-1, keepdims=True) + EPS) * nw_ref[...]
    h_ref[...] = h

    h_hi, h_lo = _split_bf16(h)
    w_hi, w_lo = _split_bf16(wr_ref[...])
    logits = (jnp.dot(h_hi, w_hi, preferred_element_type=F32)
              + jnp.dot(h_hi, w_lo, preferred_element_type=F32)
              + jnp.dot(h_lo, w_hi, preferred_element_type=F32)) + br_ref[...]

    lane = lax.broadcasted_iota(jnp.int32, (tm, LANES), 1)
    neg = jnp.finfo(F32).min
    work = jnp.where(lane < n_experts, logits, neg)
    tops, idxs, sels = [], [], []
    for _k in range(TOP_K):
        m = jnp.max(work, axis=-1, keepdims=True)
        idx = jnp.min(jnp.where(work == m, lane, LANES), axis=-1, keepdims=True)
        sel = lane == idx
        work = jnp.where(sel, neg, work)
        tops.append(m)
        idxs.append(idx)
        sels.append(sel)
    exps = [jnp.exp(t - tops[0]) for t in tops]
    denom = exps[0]
    for e in exps[1:]:
        denom = denom + e
    inv = 1.0 / denom

    onehot = jnp.zeros((tm, LANES), F32)
    for sel in sels:
        onehot = onehot + jnp.where(sel, 1.0, 0.0)
    ri = lax.broadcasted_iota(jnp.int32, (tm, tm), 0)
    ci = lax.broadcasted_iota(jnp.int32, (tm, tm), 1)
    tri = jnp.where(ri > ci, 1.0, 0.0).astype(BF16)
    carry = carry_scr[0:1, :]
    prefix = jnp.dot(tri, onehot.astype(BF16), preferred_element_type=F32) + carry
    new_carry = carry + jnp.sum(onehot, axis=0, keepdims=True)
    carry_scr[...] = jnp.broadcast_to(new_carry, carry_scr.shape)
    cnt_ref[...] = jnp.broadcast_to(new_carry, cnt_ref.shape)

    route = jnp.zeros((tm, LANES), F32)
    for k in range(TOP_K):
        rank = jnp.sum(jnp.where(sels[k], prefix, 0.0), axis=-1, keepdims=True)
        route = jnp.where(lane == k, idxs[k].astype(F32), route)
        route = jnp.where(lane == TOP_K + k, rank, route)
        route = jnp.where(lane == 2 * TOP_K + k, exps[k] * inv, route)
    route_ref[...] = route


def _outproj_route(ya, yb, x2, wa, wb, norm_w, w_router, b_router, n_experts):
    t, d = x2.shape
    dc = ya.shape[1]
    ds = yb.shape[1]
    tm = _largest_divisor(t, (512, 256, 128))
    kernel = functools.partial(_outproj_kernel, n_experts=n_experts)
    const = lambda shape: pl.BlockSpec(shape, lambda i: (0, 0))
    return pl.pallas_call(
        kernel,
        grid=(t // tm,),
        in_specs=[
            pl.BlockSpec((tm, dc), lambda i: (i, 0)),
            pl.BlockSpec((tm, ds), lambda i: (i, 0)),
            pl.BlockSpec((tm, d), lambda i: (i, 0)),
            const((dc, d)), const((ds, d)), const((1, d)), const((d, LANES)), const((1, LANES)),
        ],
        out_specs=[
            pl.BlockSpec((tm, d), lambda i: (i, 0)),
            pl.BlockSpec((tm, d), lambda i: (i, 0)),
            pl.BlockSpec((tm, LANES), lambda i: (i, 0)),
            pl.BlockSpec((8, LANES), lambda i: (0, 0)),
        ],
        out_shape=[
            jax.ShapeDtypeStruct((t, d), F32),
            jax.ShapeDtypeStruct((t, d), F32),
            jax.ShapeDtypeStruct((t, LANES), F32),
            jax.ShapeDtypeStruct((8, LANES), F32),
        ],
        scratch_shapes=[pltpu.VMEM((8, LANES), F32)],
        compiler_params=_cparams("arbitrary"),
        name="outproj_route",
    )(ya, yb, x2, wa, wb, norm_w, w_router, b_router)


ZERO_ROWS = 256


def _dispatch_kernel(zflag_ref, dest_ref, h_ref, o_ref, zbuf, sem, zsem, *, tm):
    tt = h_ref.shape[0]
    n_tiles = o_ref.shape[0] // tm
    per_tile = tm // ZERO_ROWS

    def zero_copy(i, p):
        return pltpu.make_async_copy(zbuf, o_ref.at[pl.ds(i * tm + p * ZERO_ROWS, ZERO_ROWS)], zsem)

    @pl.when(pl.program_id(0) == 0)
    def _():
        zbuf[...] = jnp.zeros_like(zbuf)

        def start(i, carry):
            @pl.when(zflag_ref[i] != 0)
            def _():
                for p in range(per_tile):
                    zero_copy(i, p).start()
            return carry

        def wait(i, carry):
            @pl.when(zflag_ref[i] != 0)
            def _():
                for p in range(per_tile):
                    zero_copy(i, p).wait()
            return carry

        lax.fori_loop(0, n_tiles, start, 0)
        lax.fori_loop(0, n_tiles, wait, 0)

    def issue(t, carry):
        for k in range(TOP_K):
            pltpu.make_async_copy(h_ref.at[pl.ds(t, 1)], o_ref.at[pl.ds(dest_ref[t * TOP_K + k], 1)],
                                  sem).start(priority=k % 2)
        return carry

    lax.fori_loop(0, tt, issue, 0)
    for _k in range(TOP_K):
        pltpu.make_async_copy(h_ref, o_ref.at[pl.ds(0, tt)], sem).wait()


def _dispatch(zflag, dest_flat, h, n_rows, tm):
    t, d = h.shape
    tt = _largest_divisor(t, (512, 256))
    assert tm % ZERO_ROWS == 0
    grid_spec = pltpu.PrefetchScalarGridSpec(
        num_scalar_prefetch=1,
        grid=(t // tt,),
        in_specs=[
            pl.BlockSpec((tt * TOP_K,), lambda i, zf: (i,), memory_space=pltpu.SMEM),
            pl.BlockSpec((tt, d), lambda i, zf: (i, 0)),
        ],
        out_specs=pl.BlockSpec(memory_space=pl.ANY),
        scratch_shapes=[pltpu.VMEM((ZERO_ROWS, d), h.dtype), pltpu.SemaphoreType.DMA(()),
                        pltpu.SemaphoreType.DMA(())],
    )
    return pl.pallas_call(
        functools.partial(_dispatch_kernel, tm=tm),
        grid_spec=grid_spec,
        out_shape=jax.ShapeDtypeStruct((n_rows, d), h.dtype),
        compiler_params=_cparams("arbitrary"),
        name="dispatch",
    )(zflag, dest_flat, h)


def _load_cast(src, dst, stage, sem):
    rows = stage.shape[1]
    n = dst.shape[0] // rows

    def copy(k, slot):
        return pltpu.make_async_copy(src.at[pl.ds(k * rows, rows)], stage.at[slot], sem.at[slot])

    copy(0, 0).start()

    def body(kk, carry):
        for slot in range(2):
            k = 2 * kk + slot
            copy(k, slot).wait()

            @pl.when(k + 1 < n)
            def _():
                copy(k + 1, 1 - slot).start()

            dst[pl.ds(pl.multiple_of(k * rows, rows), rows), :] = stage[slot].astype(BF16)
        return carry

    lax.fori_loop(0, n // 2, body, 0)


def _expert_kernel(te_ref, nu_ref, x_ref, wgu_hbm, wd_hbm, bgu_ref, bd_ref, o_ref,
                   wgu_buf, wd_buf, xb_scr, act_scr, gu_stage, d_stage, sem, *, tf):
    i = pl.program_id(0)
    f = wd_buf.shape[0]
    e = te_ref[i]
    active = i < nu_ref[0]

    @pl.when(active & ((i == 0) | (e != te_ref[jnp.maximum(i - 1, 0)])))
    def _():
        _load_cast(wgu_hbm.at[e], wgu_buf, gu_stage, sem.at[0])
        _load_cast(wd_hbm.at[e], wd_buf, d_stage, sem.at[1])

    @pl.when(active)
    def _():
        xb_scr[...] = x_ref[...].astype(BF16)
        xb = xb_scr[...]
        for j in range(f // tf):
            gcols = slice(j * tf, (j + 1) * tf)
            lcols = slice(f + j * tf, f + (j + 1) * tf)
            glu = jnp.dot(xb, wgu_buf[:, gcols], preferred_element_type=F32) + bgu_ref[:, gcols]
            lin = jnp.dot(xb, wgu_buf[:, lcols], preferred_element_type=F32) + bgu_ref[:, lcols]
            glu = jnp.minimum(glu, SWIGLU_LIMIT)
            lin = jnp.clip(lin, -SWIGLU_LIMIT, SWIGLU_LIMIT)
            act_scr[:, gcols] = (glu * jax.nn.sigmoid(SWIGLU_ALPHA * glu) * (lin + 1.0)).astype(BF16)
        o_ref[...] = (jnp.dot(act_scr[...], wd_buf[...], preferred_element_type=F32)
                      + bd_ref[...]).astype(o_ref.dtype)

    @pl.when(jnp.logical_not(active))
    def _():
        o_ref[...] = jnp.zeros_like(o_ref)


WEIGHT_STAGE_ROWS = 64


def _experts(tile_e, n_used, x_sorted, w_gu, b_gu, w_d, b_d, tm):
    n_rows, d = x_sorted.shape
    f = w_d.shape[1]
    assert d % (2 * WEIGHT_STAGE_ROWS) == 0 and f % (4 * WEIGHT_STAGE_ROWS) == 0
    tf = _largest_divisor(f, (512, 256, 128))
    n_tiles = n_rows // tm
    grid_spec = pltpu.PrefetchScalarGridSpec(
        num_scalar_prefetch=2,
        grid=(n_tiles,),
        in_specs=[
            pl.BlockSpec((tm, d), lambda i, te, nu: (jnp.minimum(i, nu[0] - 1), 0)),
            pl.BlockSpec(memory_space=pl.ANY),
            pl.BlockSpec(memory_space=pl.ANY),
            pl.BlockSpec((None, 1, 2 * f), lambda i, te, nu: (te[i], 0, 0)),
            pl.BlockSpec((None, 1, d), lambda i, te, nu: (te[i], 0, 0)),
        ],
        out_specs=pl.BlockSpec((tm, d), lambda i, te, nu: (i, 0)),
        scratch_shapes=[
            pltpu.VMEM((d, 2 * f), BF16),
            pltpu.VMEM((f, d), BF16),
            pltpu.VMEM((tm, d), BF16),
            pltpu.VMEM((tm, f), BF16),
            pltpu.VMEM((2, WEIGHT_STAGE_ROWS, 2 * f), F32),
            pltpu.VMEM((2, 2 * WEIGHT_STAGE_ROWS, d), F32),
            pltpu.SemaphoreType.DMA((2, 2)),
        ],
    )
    return pl.pallas_call(
        functools.partial(_expert_kernel, tf=tf),
        grid_spec=grid_spec,
        out_shape=jax.ShapeDtypeStruct((n_rows, d), F32),
        compiler_params=_cparams("arbitrary"),
        name="experts",
    )(tile_e, n_used, x_sorted, w_gu, w_d, b_gu, b_d)


def _combine_kernel(dest_ref, dest_next_ref, y_ref, x1_ref, route_ref, nw_ref, o_ref, buf, sem,
                    *, final_norm):
    i = pl.program_id(0)
    n = pl.num_programs(0)
    tt = x1_ref.shape[0]
    slot = i % 2

    def gather(idx_ref, dst_slot):
        def issue(t, carry):
            for k in range(TOP_K):
                pltpu.make_async_copy(y_ref.at[pl.ds(idx_ref[t * TOP_K + k], 1)],
                                      buf.at[dst_slot, pl.ds(k * tt + t, 1)],
                                      sem.at[dst_slot]).start(priority=k % 2)
            return carry

        lax.fori_loop(0, tt, issue, 0)

    @pl.when(i == 0)
    def _():
        gather(dest_ref, slot)

    @pl.when(i + 1 < n)
    def _():
        gather(dest_next_ref, 1 - slot)

    pltpu.make_async_copy(y_ref.at[pl.ds(0, TOP_K * tt)], buf.at[slot], sem.at[slot]).wait()

    route = route_ref[...]
    x = x1_ref[...]
    for k in range(TOP_K):
        gate = route[:, 2 * TOP_K + k:2 * TOP_K + k + 1]
        x = x + gate * buf[slot, k * tt:(k + 1) * tt, :]
    if final_norm:
        x = x * lax.rsqrt(jnp.mean(x * x, axis=-1, keepdims=True) + EPS) * nw_ref[...]
    o_ref[...] = x


def _combine(dest_flat, y_sorted, x1, route, norm_w, final_norm):
    t, d = x1.shape
    tt = 256
    assert t % tt == 0
    n = t // tt
    return pl.pallas_call(
        functools.partial(_combine_kernel, final_norm=final_norm),
        grid=(n,),
        in_specs=[
            pl.BlockSpec((tt * TOP_K,), lambda i: (i,), memory_space=pltpu.SMEM),
            pl.BlockSpec((tt * TOP_K,), lambda i: (jnp.minimum(i + 1, n - 1),), memory_space=pltpu.SMEM),
            pl.BlockSpec(memory_space=pl.ANY),
            pl.BlockSpec((tt, d), lambda i: (i, 0)),
            pl.BlockSpec((tt, LANES), lambda i: (i, 0)),
            pl.BlockSpec((1, d), lambda i: (0, 0)),
        ],
        out_specs=pl.BlockSpec((tt, d), lambda i: (i, 0)),
        out_shape=jax.ShapeDtypeStruct((t, d), F32),
        scratch_shapes=[pltpu.VMEM((2, TOP_K * tt, d), F32), pltpu.SemaphoreType.DMA((2,))],
        compiler_params=_cparams("arbitrary"),
        name="combine",
    )(dest_flat, dest_flat, y_sorted, x1, route, norm_w)


MOE_ROW_TILE = 512


def _layer(x2, bsz, seq, norm_mix_w, w_in, conv_a_w, ssd_conv_w, ssd_conv_b, dt_bias_fw, dt_bias_bw,
           a_log_fw, a_log_bw, d_skip, ssd_norm_w, w_out, norm_ffn_w, w_router, b_router,
           w_gate_up, b_gate_up, w_down, b_down):
    t, d = x2.shape
    dc = conv_a_w.shape[1]
    ds = ssd_norm_w.shape[0]
    dxbc = ssd_conv_w.shape[1]
    heads = dt_bias_fw.shape[0]
    g = SSD_GROUPS
    hpg = heads // g
    n_main = 3 * dc + ds + dxbc
    n_experts = w_router.shape[1]

    w_main = w_in[:, :n_main].astype(BF16)
    w_dt = jnp.pad(w_in[:, n_main:], ((0, 0), (0, LANES - 2 * heads))).astype(BF16)
    proj, dt_raw = _inproj(x2, norm_mix_w.reshape(1, d), w_main, w_dt)
    proj3 = proj.reshape(bsz, seq, n_main)
    y_a = _conv_a(proj3, conv_a_w, dc)
    xbc_act = _conv_ssd(proj3, ssd_conv_w, ssd_conv_b.reshape(1, dxbc), 3 * dc + ds, dxbc)

    def head_rows(fw, bw):
        both = jnp.concatenate([fw.reshape(g, hpg), bw.reshape(g, hpg)], axis=1)
        return jnp.pad(both, ((0, 0), (0, SUBLANES - 2 * hpg))).reshape(g, SUBLANES, 1)

    dt_rows = dt_raw[:, :2 * heads].reshape(bsz, seq, 2, g, hpg)
    dt_rows = jnp.transpose(dt_rows, (0, 3, 2, 4, 1)).reshape(bsz, g, 2 * hpg, seq)
    dt_rows = jnp.pad(dt_rows, ((0, 0), (0, 0), (0, SUBLANES - 2 * hpg), (0, 0)))
    dskip = jnp.repeat(d_skip, SSD_HEAD_DIM).reshape(1, ds)
    y_b = _ssd(xbc_act, proj3, dt_rows, head_rows(dt_bias_fw, dt_bias_bw), head_rows(a_log_fw, a_log_bw),
               dskip, ssd_norm_w.reshape(1, ds), ds, 3 * dc)

    w_out_b = w_out.astype(BF16)
    w_r = jnp.pad(w_router, ((0, 0), (0, LANES - n_experts)))
    b_r = jnp.pad(b_router, (0, LANES - n_experts)).reshape(1, LANES)
    x1, h2, route, counts = _outproj_route(
        y_a.reshape(t, dc), y_b.reshape(t, ds), x2, w_out_b[:dc], w_out_b[dc:],
        norm_ffn_w.reshape(1, d), w_r, b_r, n_experts)

    tm = MOE_ROW_TILE
    n_slots = t * TOP_K
    n_tiles = n_slots // tm + n_experts
    cnt = counts[0, :n_experts].astype(jnp.int32)
    padded = (cnt + tm - 1) // tm * tm
    pends = jnp.cumsum(padded)
    pstarts = pends - padded
    e_idx = route[:, :TOP_K].astype(jnp.int32)
    rank = route[:, TOP_K:2 * TOP_K].astype(jnp.int32)
    dest = (pstarts[e_idx] + rank).reshape(n_slots)
    n_used = (pends[-1] // tm).astype(jnp.int32)
    tile_ids = jnp.arange(n_tiles, dtype=jnp.int32)
    tile_row0 = jnp.minimum(tile_ids, n_used - 1) * tm
    tile_e = jnp.minimum(jnp.sum((pends[None, :] <= tile_row0[:, None]).astype(jnp.int32), axis=1),
                         n_experts - 1)
    partial_last = (pends - tm) * ((cnt % tm) != 0) - (cnt % tm == 0)
    zflag = (jnp.any(partial_last[None, :] == (tile_ids * tm)[:, None], axis=1)
             | (tile_ids >= n_used)).astype(jnp.int32)

    x_sorted = _dispatch(zflag, dest, h2, n_tiles * tm, tm)
    f = w_down.shape[1]
    y_sorted = _experts(tile_e, n_used.reshape(1), x_sorted, w_gate_up,
                        b_gate_up.reshape(n_experts, 1, 2 * f), w_down,
                        b_down.reshape(n_experts, 1, d), tm)
    return dest, y_sorted, x1, route


def kernel(x, norm_mix_w, w_in, conv_a_w, ssd_conv_w, ssd_conv_b, dt_bias_fw, dt_bias_bw, a_log_fw,
           a_log_bw, d_skip, ssd_norm_w, w_out, norm_ffn_w, w_router, b_router, w_gate_up, b_gate_up,
           w_down, b_down, norm_final_w):
    bsz, seq, d = x.shape
    depth = w_in.shape[0]
    x2 = x.reshape(bsz * seq, d)
    for layer in range(depth):
        dest, y_sorted, x1, route = _layer(
            x2, bsz, seq, norm_mix_w[layer], w_in[layer], conv_a_w[layer], ssd_conv_w[layer],
            ssd_conv_b[layer], dt_bias_fw[layer], dt_bias_bw[layer], a_log_fw[layer], a_log_bw[layer],
            d_skip[layer], ssd_norm_w[layer], w_out[layer], norm_ffn_w[layer], w_router[layer],
            b_router[layer], w_gate_up[layer], b_gate_up[layer], w_down[layer], b_down[layer])
        x2 = _combine(dest, y_sorted, x1, route, norm_final_w.reshape(1, d), layer == depth - 1)
    return x2.reshape(bsz, seq, d)
```

```python
import functools

import jax
import jax.numpy as jnp
from jax import lax
from jax.experimental import pallas as pl
from jax.experimental.pallas import tpu as pltpu

F32 = jnp.float32
BF16 = jnp.bfloat16

EPS = 1e-5
SSD_HEAD_DIM = 64
SSD_GROUPS = 4
SSD_STATE = 128
SSD_CHUNK = 128
TOP_K = 4
SWIGLU_LIMIT = 7.0
SWIGLU_ALPHA = 1.702

LANES = 128
SUBLANES = 8
VMEM_LIMIT_BYTES = 56 * 1024 * 1024


def _largest_divisor(n, candidates):
    for c in candidates:
        if n % c == 0:
            return c
    raise ValueError(f"no tile in {candidates} divides {n}")


def _cparams(*sem):
    return pltpu.CompilerParams(dimension_semantics=tuple(sem), vmem_limit_bytes=VMEM_LIMIT_BYTES)


def _inproj_kernel(x_ref, nw_ref, w_ref, wdt_ref, o_ref, dt_ref, h_scr):
    @pl.when(pl.program_id(1) == 0)
    def _():
        x = x_ref[...]
        ms = jnp.mean(x * x, axis=-1, keepdims=True)
        h = (x * lax.rsqrt(ms + EPS) * nw_ref[...]).astype(BF16)
        h_scr[...] = h
        dt_ref[...] = jnp.dot(h, wdt_ref[...], preferred_element_type=F32)

    o_ref[...] = jnp.dot(h_scr[...], w_ref[...], preferred_element_type=F32).astype(o_ref.dtype)


def _inproj(x2, norm_w, w_main, w_dt):
    t, d = x2.shape
    n = w_main.shape[1]
    tm = _largest_divisor(t, (1024, 512, 256, 128))
    tn = _largest_divisor(n, (1536, 1024, 512, 256, 128))
    return pl.pallas_call(
        _inproj_kernel,
        grid=(t // tm, n // tn),
        in_specs=[
            pl.BlockSpec((tm, d), lambda i, j: (i, 0)),
            pl.BlockSpec((1, d), lambda i, j: (0, 0)),
            pl.BlockSpec((d, tn), lambda i, j: (0, j)),
            pl.BlockSpec((d, LANES), lambda i, j: (0, 0)),
        ],
        out_specs=[
            pl.BlockSpec((tm, tn), lambda i, j: (i, j)),
            pl.BlockSpec((tm, LANES), lambda i, j: (i, 0)),
        ],
        out_shape=[
            jax.ShapeDtypeStruct((t, n), BF16),
            jax.ShapeDtypeStruct((t, LANES), F32),
        ],
        scratch_shapes=[pltpu.VMEM((tm, d), BF16)],
        compiler_params=_cparams("parallel", "arbitrary"),
        name="inproj",
    )(x2, norm_w, w_main, w_dt)


def _centred_conv(v, w_ref):
    s = v.shape[0]
    width = w_ref.shape[0]
    half = width // 2
    row = lax.broadcasted_iota(jnp.int32, v.shape, 0)
    acc = v * w_ref[half:half + 1, :]
    for k in range(width):
        off = k - half
        if off == 0:
            continue
        shifted = pltpu.roll(v, (-off) % s, 0)
        valid = (row + off >= 0) & (row + off < s)
        acc = acc + jnp.where(valid, shifted, 0.0) * w_ref[k:k + 1, :]
    return acc


def _conv_a_kernel(gb_ref, gc_ref, u_ref, w_ref, o_ref):
    v = gc_ref[0].astype(F32) * u_ref[0].astype(F32)
    o_ref[0] = (gb_ref[0].astype(F32) * _centred_conv(v, w_ref)).astype(o_ref.dtype)


def _conv_a(proj3, conv_w, dc):
    b, s, _ = proj3.shape
    tc = _largest_divisor(dc, (512, 256, 128))
    nb = dc // tc
    blk = lambda off: pl.BlockSpec((1, s, tc), lambda i, j: (i, 0, off + j))
    return pl.pallas_call(
        _conv_a_kernel,
        grid=(b, nb),
        in_specs=[blk(0), blk(nb), blk(2 * nb),
                  pl.BlockSpec((conv_w.shape[0], tc), lambda i, j: (0, j))],
        out_specs=pl.BlockSpec((1, s, tc), lambda i, j: (i, 0, j)),
        out_shape=jax.ShapeDtypeStruct((b, s, dc), BF16),
        compiler_params=_cparams("parallel", "parallel"),
        name="conv_a",
    )(proj3, proj3, proj3, conv_w)


def _conv_ssd_kernel(x_ref, w_ref, b_ref, o_ref):
    y = _centred_conv(x_ref[0].astype(F32), w_ref) + b_ref[...]
    o_ref[0] = (y * jax.nn.sigmoid(y)).astype(o_ref.dtype)


def _conv_ssd(proj3, conv_w, conv_b, col0, dxbc):
    b, s, _ = proj3.shape
    tc = _largest_divisor(dxbc, (512, 256, 128))
    assert col0 % tc == 0
    off = col0 // tc
    return pl.pallas_call(
        _conv_ssd_kernel,
        grid=(b, dxbc // tc),
        in_specs=[pl.BlockSpec((1, s, tc), lambda i, j: (i, 0, off + j)),
                  pl.BlockSpec((conv_w.shape[0], tc), lambda i, j: (0, j)),
                  pl.BlockSpec((1, tc), lambda i, j: (0, j))],
        out_specs=pl.BlockSpec((1, s, tc), lambda i, j: (i, 0, j)),
        out_shape=jax.ShapeDtypeStruct((b, s, dxbc), BF16),
        compiler_params=_cparams("parallel", "parallel"),
        name="conv_ssd",
    )(proj3, conv_w, conv_b)


def _split_rows(v, passes):
    parts = []
    rem = v
    for _ in range(passes):
        term = rem.astype(BF16).astype(F32)
        parts.append(term)
        rem = rem - term
    while len(parts) % 2:
        parts.append(jnp.zeros_like(v))
    return jnp.concatenate(parts, axis=0).astype(BF16)


def _expander(n_rows, lanes_per_row, n_cols):
    r = lax.broadcasted_iota(jnp.int32, (n_rows, n_cols), 0) & (SUBLANES - 1)
    c = lax.broadcasted_iota(jnp.int32, (n_rows, n_cols), 1)
    lo = r * lanes_per_row
    return jnp.where((c >= lo) & (c < lo + lanes_per_row), 1.0, 0.0).astype(BF16)


def _expand(rows_bf16, expander):
    return lax.dot_general(rows_bf16, expander, (((0,), (0,)), ((), ())), preferred_element_type=F32)


def _ssd_kernel(xs_ref, b_ref, c_ref, z_ref, dt_ref, bias_ref, alog_ref, dskip_ref, nw_ref, o_ref,
                a_scr, dt_scr, qx_scr, ex_scr, acol_scr, st_scr, *, hpg):
    s = xs_ref.shape[1]
    gw = xs_ref.shape[2]
    nc = s // SSD_CHUNK
    L = SSD_CHUNK
    hd = SSD_HEAD_DIM

    raw = dt_ref[0, 0] + bias_ref[0]
    dt = jnp.maximum(raw, 0.0) + jnp.log1p(jnp.exp(-jnp.abs(raw)))
    dta = dt * (-jnp.exp(alog_ref[0]))
    pos = lax.broadcasted_iota(jnp.int32, (SUBLANES, s), 1) & (L - 1)
    head_row = lax.broadcasted_iota(jnp.int32, (SUBLANES, s), 0)
    pre = dta
    suf = dta
    sh = 1
    while sh < L:
        pre = pre + jnp.where(pos >= sh, pltpu.roll(pre, sh, 1), 0.0)
        suf = suf + jnp.where(pos < L - sh, pltpu.roll(suf, s - sh, 1), 0.0)
        sh *= 2
    acum = jnp.where(head_row < hpg, pre, suf)
    tot = pre + suf - dta
    for c in range(nc):
        lanes = slice(c * L, (c + 1) * L)
        a_scr[c] = acum[:, lanes]
        dt_scr[c] = dt[:, lanes]

    exp_head = _expander(2 * SUBLANES, hd, 2 * gw)
    exp_col = _expander(4 * SUBLANES, L, 2 * hpg * L)
    qx_scr[...] = _expand(_split_rows(dt * jnp.exp(tot - acum), 2), exp_head)
    ex_scr[...] = _expand(_split_rows(jnp.exp(acum), 2), exp_head)
    acol_scr[...] = _expand(_split_rows(acum, 3), exp_col)

    def phase1(i, run_b):
        c = nc - 1 - i
        row0 = pl.multiple_of(c * L, L)
        rows = pl.ds(row0, L)
        x = xs_ref[0, rows, :].astype(F32)
        qx = qx_scr[rows, :]
        xw = jnp.concatenate([x * qx[:, 0:gw], x * qx[:, gw:2 * gw]], axis=1).astype(BF16)
        contrib = lax.dot_general(b_ref[0, rows, :], xw, (((0,), (0,)), ((), ())),
                                  preferred_element_type=F32)
        st_scr[c, :, 0:gw] = contrib[:, 0:gw]
        st_scr[c, :, gw:2 * gw] = run_b
        dec_b = ex_scr[pl.ds(row0, 1), gw:2 * gw]
        return run_b * dec_b + contrib[:, gw:2 * gw]

    lax.fori_loop(0, nc, phase1, jnp.zeros((SSD_STATE, gw), F32), unroll=4)

    li = lax.broadcasted_iota(jnp.int32, (L, L), 0)
    si = lax.broadcasted_iota(jnp.int32, (L, L), 1)
    causal = li >= si
    below = li > si
    above = si > li
    lane_l = lax.broadcasted_iota(jnp.int32, (L, LANES), 1)

    def phase2(c, run_f):
        rows = pl.ds(pl.multiple_of(c * L, L), L)
        xb = xs_ref[0, rows, :]
        cm = c_ref[0, rows, :]
        cb = lax.dot_general(cm, b_ref[0, rows, :], (((1,), (1,)), ((), ())), preferred_element_type=F32)
        at = a_scr[c]
        dtt = dt_scr[c]
        acol = acol_scr[rows, :]
        ys = []
        for p in range(hpg // 2):
            xpair = xb[:, p * LANES:(p + 1) * LANES]
            y_pair = None
            for q in range(2):
                kf = 2 * p + q
                kb = hpg + kf
                seg = jnp.where(causal, acol[:, kf * L:(kf + 1) * L] - at[kf:kf + 1, :],
                                acol[:, kb * L:(kb + 1) * L] - at[kb:kb + 1, :])
                dt_f = dtt[kf:kf + 1, :]
                dt_b = dtt[kb:kb + 1, :]
                w = jnp.where(below, dt_f, jnp.where(above, dt_b, dt_f + dt_b))
                m = (cb * jnp.exp(seg) * w).astype(BF16)
                in_head = (lane_l < hd) if q == 0 else (lane_l >= hd)
                part = jnp.dot(m, jnp.where(in_head, xpair, jnp.zeros_like(xpair)),
                               preferred_element_type=F32)
                y_pair = part if y_pair is None else y_pair + part
            ys.append(y_pair)
        y = ys[0] if len(ys) == 1 else jnp.concatenate(ys, axis=1)
        ex = ex_scr[rows, :]
        cs_f = jnp.dot(cm, run_f.astype(BF16), preferred_element_type=F32)
        cs_b = jnp.dot(cm, st_scr[c, :, gw:2 * gw].astype(BF16), preferred_element_type=F32)
        y = y + cs_f * ex[:, 0:gw] + cs_b * ex[:, gw:2 * gw]
        y = y + xb.astype(F32) * dskip_ref[...]
        z = z_ref[0, rows, :].astype(F32)
        g = y * (z * jax.nn.sigmoid(z))
        g = g * lax.rsqrt(jnp.mean(g * g, axis=-1, keepdims=True) + EPS)
        o_ref[0, rows, :] = (g * nw_ref[...]).astype(o_ref.dtype)
        dec_f = ex[L - 1:L, 0:gw]
        return run_f * dec_f + st_scr[c, :, 0:gw]

    lax.fori_loop(0, nc, phase2, jnp.zeros((SSD_STATE, gw), F32), unroll=2)


def _ssd(xbc_act, proj3, dt_rows, bias_col, alog_col, dskip, norm_w, ds, z_col0):
    b, s, _ = xbc_act.shape
    g = SSD_GROUPS
    gw = ds // g
    hpg = gw // SSD_HEAD_DIM
    nc = s // SSD_CHUNK
    n = SSD_STATE
    L = SSD_CHUNK
    assert z_col0 % gw == 0 and ds % n == 0 and hpg % 2 == 0 and 2 * hpg <= SUBLANES
    kernel = functools.partial(_ssd_kernel, hpg=hpg)
    return pl.pallas_call(
        kernel,
        grid=(b, g),
        in_specs=[
            pl.BlockSpec((1, s, gw), lambda i, j: (i, 0, j)),
            pl.BlockSpec((1, s, n), lambda i, j: (i, 0, ds // n + j)),
            pl.BlockSpec((1, s, n), lambda i, j: (i, 0, ds // n + g + j)),
            pl.BlockSpec((1, s, gw), lambda i, j: (i, 0, z_col0 // gw + j)),
            pl.BlockSpec((1, 1, SUBLANES, s), lambda i, j: (i, j, 0, 0)),
            pl.BlockSpec((1, SUBLANES, 1), lambda i, j: (j, 0, 0)),
            pl.BlockSpec((1, SUBLANES, 1), lambda i, j: (j, 0, 0)),
            pl.BlockSpec((1, gw), lambda i, j: (0, j)),
            pl.BlockSpec((1, gw), lambda i, j: (0, j)),
        ],
        out_specs=pl.BlockSpec((1, s, gw), lambda i, j: (i, 0, j)),
        out_shape=jax.ShapeDtypeStruct((b, s, ds), BF16),
        scratch_shapes=[
            pltpu.VMEM((nc, SUBLANES, L), F32),
            pltpu.VMEM((nc, SUBLANES, L), F32),
            pltpu.VMEM((s, 2 * gw), F32),
            pltpu.VMEM((s, 2 * gw), F32),
            pltpu.VMEM((s, 2 * hpg * L), F32),
            pltpu.VMEM((nc, n, 2 * gw), F32),
        ],
        compiler_params=_cparams("parallel", "parallel"),
        name="ssd",
    )(xbc_act, xbc_act, xbc_act, proj3, dt_rows, bias_col, alog_col, dskip, norm_w)


def _split_bf16(a):
    hi = a.astype(BF16)
    lo = (a - hi.astype(F32)).astype(BF16)
    return hi, lo


def _outproj_kernel(ya_ref, yb_ref, x_ref, wa_ref, wb_ref, nw_ref, wr_ref, br_ref,
                    x1_ref, h_ref, route_ref, cnt_ref, carry_scr, *, n_experts):
    i = pl.program_id(0)
    tm = x_ref.shape[0]

    @pl.when(i == 0)
    def _():
        carry_scr[...] = jnp.zeros_like(carry_scr)

    x1 = x_ref[...] + jnp.dot(ya_ref[...], wa_ref[...], preferred_element_type=F32) \
        + jnp.dot(yb_ref[...], wb_ref[...], preferred_element_type=F32)
    x1_ref[...] = x1
    h = x1 * lax.rsqrt(jnp.mean(x1 * x1, axis=-1, keepdims=True) + EPS) * nw_ref[...]
    h_ref[...] = h

    h_hi, h_lo = _split_bf16(h)
    w_hi, w_lo = _split_bf16(wr_ref[...])
    logits = (jnp.dot(h_hi, w_hi, preferred_element_type=F32)
              + jnp.dot(h_hi, w_lo, preferred_element_type=F32)
              + jnp.dot(h_lo, w_hi, preferred_element_type=F32)) + br_ref[...]

    lane = lax.broadcasted_iota(jnp.int32, (tm, LANES), 1)
    neg = jnp.finfo(F32).min
    work = jnp.where(lane < n_experts, logits, neg)
    tops, idxs, sels = [], [], []
    for _k in range(TOP_K):
        m = jnp.max(work, axis=-1, keepdims=True)
        idx = jnp.min(jnp.where(work == m, lane, LANES), axis=-1, keepdims=True)
        sel = lane == idx
        work = jnp.where(sel, neg, work)
        tops.append(m)
        idxs.append(idx)
        sels.append(sel)
    exps = [jnp.exp(t - tops[0]) for t in tops]
    denom = exps[0]
    for e in exps[1:]:
        denom = denom + e
    inv = 1.0 / denom

    onehot = jnp.zeros((tm, LANES), F32)
    for sel in sels:
        onehot = onehot + jnp.where(sel, 1.0, 0.0)
    ri = lax.broadcasted_iota(jnp.int32, (tm, tm), 0)
    ci = lax.broadcasted_iota(jnp.int32, (tm, tm), 1)
    tri = jnp.where(ri > ci, 1.0, 0.0).astype(BF16)
    carry = carry_scr[0:1, :]
    prefix = jnp.dot(tri, onehot.astype(BF16), preferred_element_type=F32) + carry
    new_carry = carry + jnp.sum(onehot, axis=0, keepdims=True)
    carry_scr[...] = jnp.broadcast_to(new_carry, carry_scr.shape)
    cnt_ref[...] = jnp.broadcast_to(new_carry, cnt_ref.shape)

    route = jnp.zeros((tm, LANES), F32)
    for k in range(TOP_K):
        rank = jnp.sum(jnp.where(sels[k], prefix, 0.0), axis=-1, keepdims=True)
        route = jnp.where(lane == k, idxs[k].astype(F32), route)
        route = jnp.where(lane == TOP_K + k, rank, route)
        route = jnp.where(lane == 2 * TOP_K + k, exps[k] * inv, route)
    route_ref[...] = route


def _outproj_route(ya, yb, x2, wa, wb, norm_w, w_router, b_router, n_experts):
    t, d = x2.shape
    dc = ya.shape[1]
    ds = yb.shape[1]
    tm = _largest_divisor(t, (512, 256, 128))
    kernel = functools.partial(_outproj_kernel, n_experts=n_experts)
    const = lambda shape: pl.BlockSpec(shape, lambda i: (0, 0))
    return pl.pallas_call(
        kernel,
        grid=(t // tm,),
        in_specs=[
            pl.BlockSpec((tm, dc), lambda i: (i, 0)),
            pl.BlockSpec((tm, ds), lambda i: (i, 0)),
            pl.BlockSpec((tm, d), lambda i: (i, 0)),
            const((dc, d)), const((ds, d)), const((1, d)), const((d, LANES)), const((1, LANES)),
        ],
        out_specs=[
            pl.BlockSpec((tm, d), lambda i: (i, 0)),
            pl.BlockSpec((tm, d), lambda i: (i, 0)),
            pl.BlockSpec((tm, LANES), lambda i: (i, 0)),
            pl.BlockSpec((8, LANES), lambda i: (0, 0)),
        ],
        out_shape=[
            jax.ShapeDtypeStruct((t, d), F32),
            jax.ShapeDtypeStruct((t, d), F32),
            jax.ShapeDtypeStruct((t, LANES), F32),
            jax.ShapeDtypeStruct((8, LANES), F32),
        ],
        scratch_shapes=[pltpu.VMEM((8, LANES), F32)],
        compiler_params=_cparams("arbitrary"),
        name="outproj_route",
    )(ya, yb, x2, wa, wb, norm_w, w_router, b_router)


ZERO_ROWS = 256


def _dispatch_kernel(zflag_ref, dest_ref, h_ref, o_ref, zbuf, sem, zsem, *, tm):
    tt = h_ref.shape[0]
    n_tiles = o_ref.shape[0] // tm
    per_tile = tm // ZERO_ROWS

    def zero_copy(i, p):
        return pltpu.make_async_copy(zbuf, o_ref.at[pl.ds(i * tm + p * ZERO_ROWS, ZERO_ROWS)], zsem)

    @pl.when(pl.program_id(0) == 0)
    def _():
        zbuf[...] = jnp.zeros_like(zbuf)

        def start(i, carry):
            @pl.when(zflag_ref[i] != 0)
            def _():
                for p in range(per_tile):
                    zero_copy(i, p).start()
            return carry

        def wait(i, carry):
            @pl.when(zflag_ref[i] != 0)
            def _():
                for p in range(per_tile):
                    zero_copy(i, p).wait()
            return carry

        lax.fori_loop(0, n_tiles, start, 0)
        lax.fori_loop(0, n_tiles, wait, 0)

    def issue(t, carry):
        for k in range(TOP_K):
            pltpu.make_async_copy(h_ref.at[pl.ds(t, 1)], o_ref.at[pl.ds(dest_ref[t * TOP_K + k], 1)],
                                  sem).start(priority=k % 2)
        return carry

    lax.fori_loop(0, tt, issue, 0)
    for _k in range(TOP_K):
        pltpu.make_async_copy(h_ref, o_ref.at[pl.ds(0, tt)], sem).wait()


def _dispatch(zflag, dest_flat, h, n_rows, tm):
    t, d = h.shape
    tt = _largest_divisor(t, (512, 256))
    assert tm % ZERO_ROWS == 0
    grid_spec = pltpu.PrefetchScalarGridSpec(
        num_scalar_prefetch=1,
        grid=(t // tt,),
        in_specs=[
            pl.BlockSpec((tt * TOP_K,), lambda i, zf: (i,), memory_space=pltpu.SMEM),
            pl.BlockSpec((tt, d), lambda i, zf: (i, 0)),
        ],
        out_specs=pl.BlockSpec(memory_space=pl.ANY),
        scratch_shapes=[pltpu.VMEM((ZERO_ROWS, d), h.dtype), pltpu.SemaphoreType.DMA(()),
                        pltpu.SemaphoreType.DMA(())],
    )
    return pl.pallas_call(
        functools.partial(_dispatch_kernel, tm=tm),
        grid_spec=grid_spec,
        out_shape=jax.ShapeDtypeStruct((n_rows, d), h.dtype),
        compiler_params=_cparams("arbitrary"),
        name="dispatch",
    )(zflag, dest_flat, h)


def _load_cast(src, dst, stage, sem):
    n_slots, rows, width = stage.shape
    per = width // src.shape[1]
    w = src.shape[1]
    n = dst.shape[0] // (rows * per)

    def copies(k, slot):
        return [pltpu.make_async_copy(src.at[pl.ds((k * per + p) * rows, rows)],
                                      stage.at[slot, :, p * w:(p + 1) * w], sem.at[slot])
                for p in range(per)]

    for slot in range(n_slots):
        for c in copies(slot, slot):
            c.start()

    def body(kk, carry):
        for slot in range(n_slots):
            k = n_slots * kk + slot
            for c in copies(k, slot):
                c.wait()
            for p in range(per):
                row0 = pl.multiple_of((k * per + p) * rows, rows)
                dst[pl.ds(row0, rows), :] = stage[slot, :, p * w:(p + 1) * w].astype(BF16)

            @pl.when(k + n_slots < n)
            def _():
                for c in copies(k + n_slots, slot):
                    c.start()
        return carry

    lax.fori_loop(0, n // n_slots, body, 0)


def _expert_kernel(te_ref, nu_ref, x_ref, wgu_hbm, wd_hbm, bgu_ref, bd_ref, o_ref,
                   wgu_buf, wd_buf, xb_scr, act_scr, stage, sem, *, tf):
    i = pl.program_id(0)
    f = wd_buf.shape[0]
    e = te_ref[i]
    active = i < nu_ref[0]

    @pl.when(active & ((i == 0) | (e != te_ref[jnp.maximum(i - 1, 0)])))
    def _():
        _load_cast(wgu_hbm.at[e], wgu_buf, stage, sem)
        _load_cast(wd_hbm.at[e], wd_buf, stage, sem)

    @pl.when(active)
    def _():
        xb_scr[...] = x_ref[...].astype(BF16)
        xb = xb_scr[...]
        for j in range(f // tf):
            gcols = slice(j * tf, (j + 1) * tf)
            lcols = slice(f + j * tf, f + (j + 1) * tf)
            glu = jnp.dot(xb, wgu_buf[:, gcols], preferred_element_type=F32) + bgu_ref[:, gcols]
            lin = jnp.dot(xb, wgu_buf[:, lcols], preferred_element_type=F32) + bgu_ref[:, lcols]
            glu = jnp.minimum(glu, SWIGLU_LIMIT)
            lin = jnp.clip(lin, -SWIGLU_LIMIT, SWIGLU_LIMIT)
            act_scr[:, gcols] = (glu * jax.nn.sigmoid(SWIGLU_ALPHA * glu) * (lin + 1.0)).astype(BF16)
        o_ref[...] = (jnp.dot(act_scr[...], wd_buf[...], preferred_element_type=F32)
                      + bd_ref[...]).astype(o_ref.dtype)

    @pl.when(jnp.logical_not(active))
    def _():
        o_ref[...] = jnp.zeros_like(o_ref)


WEIGHT_STAGE_ROWS = 64
WEIGHT_STAGE_SLOTS = 8


def _experts(tile_e, n_used, x_sorted, w_gu, b_gu, w_d, b_d, tm):
    n_rows, d = x_sorted.shape
    f = w_d.shape[1]
    chunk = WEIGHT_STAGE_ROWS * WEIGHT_STAGE_SLOTS
    assert (2 * f) % d == 0 and d % chunk == 0 and f % (chunk * (2 * f // d)) == 0
    tf = _largest_divisor(f, (512, 256, 128))
    n_tiles = n_rows // tm
    grid_spec = pltpu.PrefetchScalarGridSpec(
        num_scalar_prefetch=2,
        grid=(n_tiles,),
        in_specs=[
            pl.BlockSpec((tm, d), lambda i, te, nu: (jnp.minimum(i, nu[0] - 1), 0)),
            pl.BlockSpec(memory_space=pl.ANY),
            pl.BlockSpec(memory_space=pl.ANY),
            pl.BlockSpec((None, 1, 2 * f), lambda i, te, nu: (te[i], 0, 0)),
            pl.BlockSpec((None, 1, d), lambda i, te, nu: (te[i], 0, 0)),
        ],
        out_specs=pl.BlockSpec((tm, d), lambda i, te, nu: (i, 0)),
        scratch_shapes=[
            pltpu.VMEM((d, 2 * f), BF16),
            pltpu.VMEM((f, d), BF16),
            pltpu.VMEM((tm, d), BF16),
            pltpu.VMEM((tm, f), BF16),
            pltpu.VMEM((WEIGHT_STAGE_SLOTS, WEIGHT_STAGE_ROWS, 2 * f), F32),
            pltpu.SemaphoreType.DMA((WEIGHT_STAGE_SLOTS,)),
        ],
    )
    return pl.pallas_call(
        functools.partial(_expert_kernel, tf=tf),
        grid_spec=grid_spec,
        out_shape=jax.ShapeDtypeStruct((n_rows, d), F32),
        compiler_params=_cparams("arbitrary"),
        name="experts",
    )(tile_e, n_used, x_sorted, w_gu, w_d, b_gu, b_d)


def _combine_kernel(dest_ref, dest_next_ref, y_ref, x1_ref, route_ref, nw_ref, o_ref, buf, sem,
                    *, final_norm):
    i = pl.program_id(0)
    n = pl.num_programs(0)
    tt = x1_ref.shape[0]
    slot = i % 2

    def gather(idx_ref, dst_slot):
        def issue(t, carry):
            for k in range(TOP_K):
                pltpu.make_async_copy(y_ref.at[pl.ds(idx_ref[t * TOP_K + k], 1)],
                                      buf.at[dst_slot, pl.ds(k * tt + t, 1)],
                                      sem.at[dst_slot]).start(priority=k % 2)
            return carry

        lax.fori_loop(0, tt, issue, 0)

    @pl.when(i == 0)
    def _():
        gather(dest_ref, slot)

    @pl.when(i + 1 < n)
    def _():
        gather(dest_next_ref, 1 - slot)

    pltpu.make_async_copy(y_ref.at[pl.ds(0, TOP_K * tt)], buf.at[slot], sem.at[slot]).wait()

    route = route_ref[...]
    x = x1_ref[...]
    for k in range(TOP_K):
        gate = route[:, 2 * TOP_K + k:2 * TOP_K + k + 1]
        x = x + gate * buf[slot, k * tt:(k + 1) * tt, :]
    if final_norm:
        x = x * lax.rsqrt(jnp.mean(x * x, axis=-1, keepdims=True) + EPS) * nw_ref[...]
    o_ref[...] = x


def _combine(dest_flat, y_sorted, x1, route, norm_w, final_norm):
    t, d = x1.shape
    tt = 256
    assert t % tt == 0
    n = t // tt
    return pl.pallas_call(
        functools.partial(_combine_kernel, final_norm=final_norm),
        grid=(n,),
        in_specs=[
            pl.BlockSpec((tt * TOP_K,), lambda i: (i,), memory_space=pltpu.SMEM),
            pl.BlockSpec((tt * TOP_K,), lambda i: (jnp.minimum(i + 1, n - 1),), memory_space=pltpu.SMEM),
            pl.BlockSpec(memory_space=pl.ANY),
            pl.BlockSpec((tt, d), lambda i: (i, 0)),
            pl.BlockSpec((tt, LANES), lambda i: (i, 0)),
            pl.BlockSpec((1, d), lambda i: (0, 0)),
        ],
        out_specs=pl.BlockSpec((tt, d), lambda i: (i, 0)),
        out_shape=jax.ShapeDtypeStruct((t, d), F32),
        scratch_shapes=[pltpu.VMEM((2, TOP_K * tt, d), F32), pltpu.SemaphoreType.DMA((2,))],
        compiler_params=_cparams("arbitrary"),
        name="combine",
    )(dest_flat, dest_flat, y_sorted, x1, route, norm_w)


MOE_ROW_TILE = 512


def _layer(x2, bsz, seq, norm_mix_w, w_in, conv_a_w, ssd_conv_w, ssd_conv_b, dt_bias_fw, dt_bias_bw,
           a_log_fw, a_log_bw, d_skip, ssd_norm_w, w_out, norm_ffn_w, w_router, b_router,
           w_gate_up, b_gate_up, w_down, b_down):
    t, d = x2.shape
    dc = conv_a_w.shape[1]
    ds = ssd_norm_w.shape[0]
    dxbc = ssd_conv_w.shape[1]
    heads = dt_bias_fw.shape[0]
    g = SSD_GROUPS
    hpg = heads // g
    n_main = 3 * dc + ds + dxbc
    n_experts = w_router.shape[1]

    w_main = w_in[:, :n_main].astype(BF16)
    w_dt = jnp.pad(w_in[:, n_main:], ((0, 0), (0, LANES - 2 * heads))).astype(BF16)
    proj, dt_raw = _inproj(x2, norm_mix_w.reshape(1, d), w_main, w_dt)
    proj3 = proj.reshape(bsz, seq, n_main)
    y_a = _conv_a(proj3, conv_a_w, dc)
    xbc_act = _conv_ssd(proj3, ssd_conv_w, ssd_conv_b.reshape(1, dxbc), 3 * dc + ds, dxbc)

    def head_rows(fw, bw):
        both = jnp.concatenate([fw.reshape(g, hpg), bw.reshape(g, hpg)], axis=1)
        return jnp.pad(both, ((0, 0), (0, SUBLANES - 2 * hpg))).reshape(g, SUBLANES, 1)

    dt_rows = dt_raw[:, :2 * heads].reshape(bsz, seq, 2, g, hpg)
    dt_rows = jnp.transpose(dt_rows, (0, 3, 2, 4, 1)).reshape(bsz, g, 2 * hpg, seq)
    dt_rows = jnp.pad(dt_rows, ((0, 0), (0, 0), (0, SUBLANES - 2 * hpg), (0, 0)))
    dskip = jnp.repeat(d_skip, SSD_HEAD_DIM).reshape(1, ds)
    y_b = _ssd(xbc_act, proj3, dt_rows, head_rows(dt_bias_fw, dt_bias_bw), head_rows(a_log_fw, a_log_bw),
               dskip, ssd_norm_w.reshape(1, ds), ds, 3 * dc)

    w_out_b = w_out.astype(BF16)
    w_r = jnp.pad(w_router, ((0, 0), (0, LANES - n_experts)))
    b_r = jnp.pad(b_router, (0, LANES - n_experts)).reshape(1, LANES)
    x1, h2, route, counts = _outproj_route(
        y_a.reshape(t, dc), y_b.reshape(t, ds), x2, w_out_b[:dc], w_out_b[dc:],
        norm_ffn_w.reshape(1, d), w_r, b_r, n_experts)

    tm = MOE_ROW_TILE
    n_slots = t * TOP_K
    n_tiles = n_slots // tm + n_experts
    cnt = counts[0, :n_experts].astype(jnp.int32)
    padded = (cnt + tm - 1) // tm * tm
    pends = jnp.cumsum(padded)
    pstarts = pends - padded
    e_idx = route[:, :TOP_K].astype(jnp.int32)
    rank = route[:, TOP_K:2 * TOP_K].astype(jnp.int32)
    dest = (pstarts[e_idx] + rank).reshape(n_slots)
    n_used = (pends[-1] // tm).astype(jnp.int32)
    tile_ids = jnp.arange(n_tiles, dtype=jnp.int32)
    tile_row0 = jnp.minimum(tile_ids, n_used - 1) * tm
    tile_e = jnp.minimum(jnp.sum((pends[None, :] <= tile_row0[:, None]).astype(jnp.int32), axis=1),
                         n_experts - 1)
    partial_last = (pends - tm) * ((cnt % tm) != 0) - (cnt % tm == 0)
    zflag = (jnp.any(partial_last[None, :] == (tile_ids * tm)[:, None], axis=1)
             | (tile_ids >= n_used)).astype(jnp.int32)

    x_sorted = _dispatch(zflag, dest, h2, n_tiles * tm, tm)
    f = w_down.shape[1]
    y_sorted = _experts(tile_e, n_used.reshape(1), x_sorted, w_gate_up,
                        b_gate_up.reshape(n_experts, 1, 2 * f), w_down,
                        b_down.reshape(n_experts, 1, d), tm)
    return dest, y_sorted, x1, route


def kernel(x, norm_mix_w, w_in, conv_a_w, ssd_conv_w, ssd_conv_b, dt_bias_fw, dt_bias_bw, a_log_fw,
           a_log_bw, d_skip, ssd_norm_w, w_out, norm_ffn_w, w_router, b_router, w_gate_up, b_gate_up,
           w_down, b_down, norm_final_w):
    bsz, seq, d = x.shape
    depth = w_in.shape[0]
    x2 = x.reshape(bsz * seq, d)
    for layer in range(depth):
        dest, y_sorted, x1, route = _layer(
            x2, bsz, seq, norm_mix_w[layer], w_in[layer], conv_a_w[layer], ssd_conv_w[layer],
            ssd_conv_b[layer], dt_bias_fw[layer], dt_bias_bw[layer], a_log_fw[layer], a_log_bw[layer],
            d_skip[layer], ssd_norm_w[layer], w_out[layer], norm_ffn_w[layer], w_router[layer],
            b_router[layer], w_gate_up[layer], b_gate_up[layer], w_down[layer], b_down[layer])
        x2 = _combine(dest, y_sorted, x1, route, norm_final_w.reshape(1, d), layer == depth - 1)
    return x2.reshape(bsz, seq, d)
```

```python
import functools

import jax
import jax.numpy as jnp
from jax import lax
from jax.experimental import pallas as pl
from jax.experimental.pallas import tpu as pltpu

F32 = jnp.float32
BF16 = jnp.bfloat16

EPS = 1e-5
SSD_HEAD_DIM = 64
SSD_GROUPS = 4
SSD_STATE = 128
SSD_CHUNK = 128
TOP_K = 4
SWIGLU_LIMIT = 7.0
SWIGLU_ALPHA = 1.702

LANES = 128
SUBLANES = 8
VMEM_LIMIT_BYTES = 56 * 1024 * 1024


def _largest_divisor(n, candidates):
    for c in candidates:
        if n % c == 0:
            return c
    raise ValueError(f"no tile in {candidates} divides {n}")


def _cparams(*sem):
    return pltpu.CompilerParams(dimension_semantics=tuple(sem), vmem_limit_bytes=VMEM_LIMIT_BYTES)


def _inproj_kernel(x_ref, nw_ref, w_ref, wdt_ref, o_ref, dt_ref, h_scr):
    @pl.when(pl.program_id(1) == 0)
    def _():
        x = x_ref[...]
        ms = jnp.mean(x * x, axis=-1, keepdims=True)
        h = (x * lax.rsqrt(ms + EPS) * nw_ref[...]).astype(BF16)
        h_scr[...] = h
        dt_ref[...] = jnp.dot(h, wdt_ref[...], preferred_element_type=F32)

    o_ref[...] = jnp.dot(h_scr[...], w_ref[...], preferred_element_type=F32).astype(o_ref.dtype)


def _inproj(x2, norm_w, w_main, w_dt):
    t, d = x2.shape
    n = w_main.shape[1]
    tm = _largest_divisor(t, (1024, 512, 256, 128))
    tn = _largest_divisor(n, (1536, 1024, 512, 256, 128))
    return pl.pallas_call(
        _inproj_kernel,
        grid=(t // tm, n // tn),
        in_specs=[
            pl.BlockSpec((tm, d), lambda i, j: (i, 0)),
            pl.BlockSpec((1, d), lambda i, j: (0, 0)),
            pl.BlockSpec((d, tn), lambda i, j: (0, j)),
            pl.BlockSpec((d, LANES), lambda i, j: (0, 0)),
        ],
        out_specs=[
            pl.BlockSpec((tm, tn), lambda i, j: (i, j)),
            pl.BlockSpec((tm, LANES), lambda i, j: (i, 0)),
        ],
        out_shape=[
            jax.ShapeDtypeStruct((t, n), BF16),
            jax.ShapeDtypeStruct((t, LANES), F32),
        ],
        scratch_shapes=[pltpu.VMEM((tm, d), BF16)],
        compiler_params=_cparams("parallel", "arbitrary"),
        name="inproj",
    )(x2, norm_w, w_main, w_dt)


def _centred_conv(v, w_ref):
    s = v.shape[0]
    width = w_ref.shape[0]
    half = width // 2
    row = lax.broadcasted_iota(jnp.int32, v.shape, 0)
    acc = v * w_ref[half:half + 1, :]
    for k in range(width):
        off = k - half
        if off == 0:
            continue
        shifted = pltpu.roll(v, (-off) % s, 0)
        valid = (row + off >= 0) & (row + off < s)
        acc = acc + jnp.where(valid, shifted, 0.0) * w_ref[k:k + 1, :]
    return acc


def _conv_a_kernel(gb_ref, gc_ref, u_ref, w_ref, o_ref):
    v = gc_ref[0].astype(F32) * u_ref[0].astype(F32)
    o_ref[0] = (gb_ref[0].astype(F32) * _centred_conv(v, w_ref)).astype(o_ref.dtype)


def _conv_a(proj3, conv_w, dc):
    b, s, _ = proj3.shape
    tc = _largest_divisor(dc, (512, 256, 128))
    nb = dc // tc
    blk = lambda off: pl.BlockSpec((1, s, tc), lambda i, j: (i, 0, off + j))
    return pl.pallas_call(
        _conv_a_kernel,
        grid=(b, nb),
        in_specs=[blk(0), blk(nb), blk(2 * nb),
                  pl.BlockSpec((conv_w.shape[0], tc), lambda i, j: (0, j))],
        out_specs=pl.BlockSpec((1, s, tc), lambda i, j: (i, 0, j)),
        out_shape=jax.ShapeDtypeStruct((b, s, dc), BF16),
        compiler_params=_cparams("parallel", "parallel"),
        name="conv_a",
    )(proj3, proj3, proj3, conv_w)


def _conv_ssd_kernel(x_ref, w_ref, b_ref, o_ref):
    y = _centred_conv(x_ref[0].astype(F32), w_ref) + b_ref[...]
    o_ref[0] = (y * jax.nn.sigmoid(y)).astype(o_ref.dtype)


def _conv_ssd(proj3, conv_w, conv_b, col0, dxbc):
    b, s, _ = proj3.shape
    tc = _largest_divisor(dxbc, (512, 256, 128))
    assert col0 % tc == 0
    off = col0 // tc
    return pl.pallas_call(
        _conv_ssd_kernel,
        grid=(b, dxbc // tc),
        in_specs=[pl.BlockSpec((1, s, tc), lambda i, j: (i, 0, off + j)),
                  pl.BlockSpec((conv_w.shape[0], tc), lambda i, j: (0, j)),
                  pl.BlockSpec((1, tc), lambda i, j: (0, j))],
        out_specs=pl.BlockSpec((1, s, tc), lambda i, j: (i, 0, j)),
        out_shape=jax.ShapeDtypeStruct((b, s, dxbc), BF16),
        compiler_params=_cparams("parallel", "parallel"),
        name="conv_ssd",
    )(proj3, conv_w, conv_b)


def _split_rows(v, passes):
    parts = []
    rem = v
    for _ in range(passes):
        term = rem.astype(BF16).astype(F32)
        parts.append(term)
        rem = rem - term
    while len(parts) % 2:
        parts.append(jnp.zeros_like(v))
    return jnp.concatenate(parts, axis=0).astype(BF16)


def _expander(n_rows, lanes_per_row, n_cols):
    r = lax.broadcasted_iota(jnp.int32, (n_rows, n_cols), 0) & (SUBLANES - 1)
    c = lax.broadcasted_iota(jnp.int32, (n_rows, n_cols), 1)
    lo = r * lanes_per_row
    return jnp.where((c >= lo) & (c < lo + lanes_per_row), 1.0, 0.0).astype(BF16)


def _expand(rows_bf16, expander):
    return lax.dot_general(rows_bf16, expander, (((0,), (0,)), ((), ())), preferred_element_type=F32)


def _ssd_kernel(xs_ref, b_ref, c_ref, z_ref, dt_ref, bias_ref, alog_ref, dskip_ref, nw_ref, o_ref,
                a_scr, dt_scr, qx_scr, ex_scr, acol_scr, st_scr, *, hpg):
    s = xs_ref.shape[1]
    gw = xs_ref.shape[2]
    nc = s // SSD_CHUNK
    L = SSD_CHUNK
    hd = SSD_HEAD_DIM

    raw = dt_ref[0, 0] + bias_ref[0]
    dt = jnp.maximum(raw, 0.0) + jnp.log1p(jnp.exp(-jnp.abs(raw)))
    dta = dt * (-jnp.exp(alog_ref[0]))
    pos = lax.broadcasted_iota(jnp.int32, (SUBLANES, s), 1) & (L - 1)
    head_row = lax.broadcasted_iota(jnp.int32, (SUBLANES, s), 0)
    pre = dta
    suf = dta
    sh = 1
    while sh < L:
        pre = pre + jnp.where(pos >= sh, pltpu.roll(pre, sh, 1), 0.0)
        suf = suf + jnp.where(pos < L - sh, pltpu.roll(suf, s - sh, 1), 0.0)
        sh *= 2
    acum = jnp.where(head_row < hpg, pre, suf)
    tot = pre + suf - dta
    for c in range(nc):
        lanes = slice(c * L, (c + 1) * L)
        a_scr[c] = acum[:, lanes]
        dt_scr[c] = dt[:, lanes]

    exp_head = _expander(2 * SUBLANES, hd, 2 * gw)
    exp_col = _expander(4 * SUBLANES, L, 2 * hpg * L)
    qx_scr[...] = _expand(_split_rows(dt * jnp.exp(tot - acum), 2), exp_head)
    ex_scr[...] = _expand(_split_rows(jnp.exp(acum), 2), exp_head)
    acol_scr[...] = _expand(_split_rows(acum, 3), exp_col)

    def phase1(i, run_b):
        c = nc - 1 - i
        row0 = pl.multiple_of(c * L, L)
        rows = pl.ds(row0, L)
        x = xs_ref[0, rows, :].astype(F32)
        qx = qx_scr[rows, :]
        xw = jnp.concatenate([x * qx[:, 0:gw], x * qx[:, gw:2 * gw]], axis=1).astype(BF16)
        contrib = lax.dot_general(b_ref[0, rows, :], xw, (((0,), (0,)), ((), ())),
                                  preferred_element_type=F32)
        st_scr[c, :, 0:gw] = contrib[:, 0:gw]
        st_scr[c, :, gw:2 * gw] = run_b
        dec_b = ex_scr[pl.ds(row0, 1), gw:2 * gw]
        return run_b * dec_b + contrib[:, gw:2 * gw]

    lax.fori_loop(0, nc, phase1, jnp.zeros((SSD_STATE, gw), F32), unroll=4)

    li = lax.broadcasted_iota(jnp.int32, (L, L), 0)
    si = lax.broadcasted_iota(jnp.int32, (L, L), 1)
    causal = li >= si
    below = li > si
    above = si > li
    lane_l = lax.broadcasted_iota(jnp.int32, (L, LANES), 1)

    def phase2(c, run_f):
        rows = pl.ds(pl.multiple_of(c * L, L), L)
        xb = xs_ref[0, rows, :]
        cm = c_ref[0, rows, :]
        cb = lax.dot_general(cm, b_ref[0, rows, :], (((1,), (1,)), ((), ())), preferred_element_type=F32)
        at = a_scr[c]
        dtt = dt_scr[c]
        acol = acol_scr[rows, :]
        ys = []
        for p in range(hpg // 2):
            xpair = xb[:, p * LANES:(p + 1) * LANES]
            y_pair = None
            for q in range(2):
                kf = 2 * p + q
                kb = hpg + kf
                seg = jnp.where(causal, acol[:, kf * L:(kf + 1) * L] - at[kf:kf + 1, :],
                                acol[:, kb * L:(kb + 1) * L] - at[kb:kb + 1, :])
                dt_f = dtt[kf:kf + 1, :]
                dt_b = dtt[kb:kb + 1, :]
                w = jnp.where(below, dt_f, jnp.where(above, dt_b, dt_f + dt_b))
                m = (cb * jnp.exp(seg) * w).astype(BF16)
                in_head = (lane_l < hd) if q == 0 else (lane_l >= hd)
                part = jnp.dot(m, jnp.where(in_head, xpair, jnp.zeros_like(xpair)),
                               preferred_element_type=F32)
                y_pair = part if y_pair is None else y_pair + part
            ys.append(y_pair)
        y = ys[0] if len(ys) == 1 else jnp.concatenate(ys, axis=1)
        ex = ex_scr[rows, :]
        cs_f = jnp.dot(cm, run_f.astype(BF16), preferred_element_type=F32)
        cs_b = jnp.dot(cm, st_scr[c, :, gw:2 * gw].astype(BF16), preferred_element_type=F32)
        y = y + cs_f * ex[:, 0:gw] + cs_b * ex[:, gw:2 * gw]
        y = y + xb.astype(F32) * dskip_ref[...]
        z = z_ref[0, rows, :].astype(F32)
        g = y * (z * jax.nn.sigmoid(z))
        g = g * lax.rsqrt(jnp.mean(g * g, axis=-1, keepdims=True) + EPS)
        o_ref[0, rows, :] = (g * nw_ref[...]).astype(o_ref.dtype)
        dec_f = ex[L - 1:L, 0:gw]
        return run_f * dec_f + st_scr[c, :, 0:gw]

    lax.fori_loop(0, nc, phase2, jnp.zeros((SSD_STATE, gw), F32), unroll=2)


def _ssd(xbc_act, proj3, dt_rows, bias_col, alog_col, dskip, norm_w, ds, z_col0):
    b, s, _ = xbc_act.shape
    g = SSD_GROUPS
    gw = ds // g
    hpg = gw // SSD_HEAD_DIM
    nc = s // SSD_CHUNK
    n = SSD_STATE
    L = SSD_CHUNK
    assert z_col0 % gw == 0 and ds % n == 0 and hpg % 2 == 0 and 2 * hpg <= SUBLANES
    kernel = functools.partial(_ssd_kernel, hpg=hpg)
    return pl.pallas_call(
        kernel,
        grid=(b, g),
        in_specs=[
            pl.BlockSpec((1, s, gw), lambda i, j: (i, 0, j)),
            pl.BlockSpec((1, s, n), lambda i, j: (i, 0, ds // n + j)),
            pl.BlockSpec((1, s, n), lambda i, j: (i, 0, ds // n + g + j)),
            pl.BlockSpec((1, s, gw), lambda i, j: (i, 0, z_col0 // gw + j)),
            pl.BlockSpec((1, 1, SUBLANES, s), lambda i, j: (i, j, 0, 0)),
            pl.BlockSpec((1, SUBLANES, 1), lambda i, j: (j, 0, 0)),
            pl.BlockSpec((1, SUBLANES, 1), lambda i, j: (j, 0, 0)),
            pl.BlockSpec((1, gw), lambda i, j: (0, j)),
            pl.BlockSpec((1, gw), lambda i, j: (0, j)),
        ],
        out_specs=pl.BlockSpec((1, s, gw), lambda i, j: (i, 0, j)),
        out_shape=jax.ShapeDtypeStruct((b, s, ds), BF16),
        scratch_shapes=[
            pltpu.VMEM((nc, SUBLANES, L), F32),
            pltpu.VMEM((nc, SUBLANES, L), F32),
            pltpu.VMEM((s, 2 * gw), F32),
            pltpu.VMEM((s, 2 * gw), F32),
            pltpu.VMEM((s, 2 * hpg * L), F32),
            pltpu.VMEM((nc, n, 2 * gw), F32),
        ],
        compiler_params=_cparams("parallel", "parallel"),
        name="ssd",
    )(xbc_act, xbc_act, xbc_act, proj3, dt_rows, bias_col, alog_col, dskip, norm_w)


def _split_bf16(a):
    hi = a.astype(BF16)
    lo = (a - hi.astype(F32)).astype(BF16)
    return hi, lo


def _outproj_kernel(ya_ref, yb_ref, x_ref, wa_ref, wb_ref, nw_ref, wr_ref, br_ref,
                    x1_ref, h_ref, route_ref, cnt_ref, carry_scr, *, n_experts):
    i = pl.program_id(0)
    tm = x_ref.shape[0]

    @pl.when(i == 0)
    def _():
        carry_scr[...] = jnp.zeros_like(carry_scr)

    x1 = x_ref[...] + jnp.dot(ya_ref[...], wa_ref[...], preferred_element_type=F32) \
        + jnp.dot(yb_ref[...], wb_ref[...], preferred_element_type=F32)
    x1_ref[...] = x1
    h = x1 * lax.rsqrt(jnp.mean(x1 * x1, axis=-1, keepdims=True) + EPS) * nw_ref[...]
    h_ref[...] = h

    h_hi, h_lo = _split_bf16(h)
    w_hi, w_lo = _split_bf16(wr_ref[...])
    logits = (jnp.dot(h_hi, w_hi, preferred_element_type=F32)
              + jnp.dot(h_hi, w_lo, preferred_element_type=F32)
              + jnp.dot(h_lo, w_hi, preferred_element_type=F32)) + br_ref[...]

    lane = lax.broadcasted_iota(jnp.int32, (tm, LANES), 1)
    neg = jnp.finfo(F32).min
    work = jnp.where(lane < n_experts, logits, neg)
    tops, idxs, sels = [], [], []
    for _k in range(TOP_K):
        m = jnp.max(work, axis=-1, keepdims=True)
        idx = jnp.min(jnp.where(work == m, lane, LANES), axis=-1, keepdims=True)
        sel = lane == idx
        work = jnp.where(sel, neg, work)
        tops.append(m)
        idxs.append(idx)
        sels.append(sel)
    exps = [jnp.exp(t - tops[0]) for t in tops]
    denom = exps[0]
    for e in exps[1:]:
        denom = denom + e
    inv = 1.0 / denom

    onehot = jnp.zeros((tm, LANES), F32)
    for sel in sels:
        onehot = onehot + jnp.where(sel, 1.0, 0.0)
    ri = lax.broadcasted_iota(jnp.int32, (tm, tm), 0)
    ci = lax.broadcasted_iota(jnp.int32, (tm, tm), 1)
    tri = jnp.where(ri > ci, 1.0, 0.0).astype(BF16)
    carry = carry_scr[0:1, :]
    prefix = jnp.dot(tri, onehot.astype(BF16), preferred_element_type=F32) + carry
    new_carry = carry + jnp.sum(onehot, axis=0, keepdims=True)
    carry_scr[...] = jnp.broadcast_to(new_carry, carry_scr.shape)
    cnt_ref[...] = jnp.broadcast_to(new_carry, cnt_ref.shape)

    route = jnp.zeros((tm, LANES), F32)
    for k in range(TOP_K):
        rank = jnp.sum(jnp.where(sels[k], prefix, 0.0), axis=-1, keepdims=True)
        route = jnp.where(lane == k, idxs[k].astype(F32), route)
        route = jnp.where(lane == TOP_K + k, rank, route)
        route = jnp.where(lane == 2 * TOP_K + k, exps[k] * inv, route)
    route_ref[...] = route


def _outproj_route(ya, yb, x2, wa, wb, norm_w, w_router, b_router, n_experts):
    t, d = x2.shape
    dc = ya.shape[1]
    ds = yb.shape[1]
    tm = _largest_divisor(t, (512, 256, 128))
    kernel = functools.partial(_outproj_kernel, n_experts=n_experts)
    const = lambda shape: pl.BlockSpec(shape, lambda i: (0, 0))
    return pl.pallas_call(
        kernel,
        grid=(t // tm,),
        in_specs=[
            pl.BlockSpec((tm, dc), lambda i: (i, 0)),
            pl.BlockSpec((tm, ds), lambda i: (i, 0)),
            pl.BlockSpec((tm, d), lambda i: (i, 0)),
            const((dc, d)), const((ds, d)), const((1, d)), const((d, LANES)), const((1, LANES)),
        ],
        out_specs=[
            pl.BlockSpec((tm, d), lambda i: (i, 0)),
            pl.BlockSpec((tm, d), lambda i: (i, 0)),
            pl.BlockSpec((tm, LANES), lambda i: (i, 0)),
            pl.BlockSpec((8, LANES), lambda i: (0, 0)),
        ],
        out_shape=[
            jax.ShapeDtypeStruct((t, d), F32),
            jax.ShapeDtypeStruct((t, d), F32),
            jax.ShapeDtypeStruct((t, LANES), F32),
            jax.ShapeDtypeStruct((8, LANES), F32),
        ],
        scratch_shapes=[pltpu.VMEM((8, LANES), F32)],
        compiler_params=_cparams("arbitrary"),
        name="outproj_route",
    )(ya, yb, x2, wa, wb, norm_w, w_router, b_router)


ZERO_ROWS = 256


def _dispatch_kernel(zflag_ref, dest_ref, h_ref, o_ref, zbuf, sem, zsem, *, tm):
    tt = dest_ref.shape[0] // TOP_K
    n_tiles = o_ref.shape[0] // tm
    per_tile = tm // ZERO_ROWS

    def zero_copy(i, p):
        return pltpu.make_async_copy(zbuf, o_ref.at[pl.ds(i * tm + p * ZERO_ROWS, ZERO_ROWS)], zsem)

    @pl.when(pl.program_id(0) == 0)
    def _():
        zbuf[...] = jnp.zeros_like(zbuf)

        def start(i, carry):
            @pl.when(zflag_ref[i] != 0)
            def _():
                for p in range(per_tile):
                    zero_copy(i, p).start()
            return carry

        def wait(i, carry):
            @pl.when(zflag_ref[i] != 0)
            def _():
                for p in range(per_tile):
                    zero_copy(i, p).wait()
            return carry

        lax.fori_loop(0, n_tiles, start, 0)
        lax.fori_loop(0, n_tiles, wait, 0)

    step = pl.program_id(0)
    row0 = step * tt

    def issue(t, carry):
        for k in range(TOP_K):
            pltpu.make_async_copy(h_ref.at[pl.ds(row0 + t, 1)],
                                  o_ref.at[pl.ds(dest_ref[t * TOP_K + k], 1)], sem).start(priority=k % 2)
        return carry

    def wait_one_step():
        for _k in range(TOP_K):
            pltpu.make_async_copy(h_ref.at[pl.ds(0, tt)], o_ref.at[pl.ds(0, tt)], sem).wait()

    lax.fori_loop(0, tt, issue, 0)

    @pl.when(step > 0)
    def _():
        wait_one_step()

    @pl.when(step == pl.num_programs(0) - 1)
    def _():
        wait_one_step()


def _dispatch(zflag, dest_flat, h, n_rows, tm):
    t, d = h.shape
    tt = _largest_divisor(t, (512, 256))
    assert tm % ZERO_ROWS == 0
    grid_spec = pltpu.PrefetchScalarGridSpec(
        num_scalar_prefetch=1,
        grid=(t // tt,),
        in_specs=[
            pl.BlockSpec((tt * TOP_K,), lambda i, zf: (i,), memory_space=pltpu.SMEM),
            pl.BlockSpec(memory_space=pl.ANY),
        ],
        out_specs=pl.BlockSpec(memory_space=pl.ANY),
        scratch_shapes=[pltpu.VMEM((ZERO_ROWS, d), h.dtype), pltpu.SemaphoreType.DMA(()),
                        pltpu.SemaphoreType.DMA(())],
    )
    return pl.pallas_call(
        functools.partial(_dispatch_kernel, tm=tm),
        grid_spec=grid_spec,
        out_shape=jax.ShapeDtypeStruct((n_rows, d), h.dtype),
        compiler_params=_cparams("arbitrary"),
        name="dispatch",
    )(zflag, dest_flat, h)


def _load_cast(src, dst, stage, sem):
    n_slots, rows, width = stage.shape
    per = width // src.shape[1]
    w = src.shape[1]
    n = dst.shape[0] // (rows * per)

    def copies(k, slot):
        return [pltpu.make_async_copy(src.at[pl.ds((k * per + p) * rows, rows)],
                                      stage.at[slot, :, p * w:(p + 1) * w], sem.at[slot])
                for p in range(per)]

    for slot in range(n_slots):
        for c in copies(slot, slot):
            c.start()

    def body(kk, carry):
        for slot in range(n_slots):
            k = n_slots * kk + slot
            for c in copies(k, slot):
                c.wait()
            for p in range(per):
                row0 = pl.multiple_of((k * per + p) * rows, rows)
                dst[pl.ds(row0, rows), :] = stage[slot, :, p * w:(p + 1) * w].astype(BF16)

            @pl.when(k + n_slots < n)
            def _():
                for c in copies(k + n_slots, slot):
                    c.start()
        return carry

    lax.fori_loop(0, n // n_slots, body, 0)


def _expert_kernel(te_ref, nu_ref, x_ref, wgu_hbm, wd_hbm, bgu_ref, bd_ref, o_ref,
                   wgu_buf, wd_buf, xb_scr, act_scr, stage, sem, *, tf):
    i = pl.program_id(0)
    f = wd_buf.shape[0]
    e = te_ref[i]
    active = i < nu_ref[0]

    @pl.when(active & ((i == 0) | (e != te_ref[jnp.maximum(i - 1, 0)])))
    def _():
        _load_cast(wgu_hbm.at[e], wgu_buf, stage, sem)
        _load_cast(wd_hbm.at[e], wd_buf, stage, sem)

    @pl.when(active)
    def _():
        xb_scr[...] = x_ref[...].astype(BF16)
        xb = xb_scr[...]
        for j in range(f // tf):
            gcols = slice(j * tf, (j + 1) * tf)
            lcols = slice(f + j * tf, f + (j + 1) * tf)
            glu = jnp.dot(xb, wgu_buf[:, gcols], preferred_element_type=F32) + bgu_ref[:, gcols]
            lin = jnp.dot(xb, wgu_buf[:, lcols], preferred_element_type=F32) + bgu_ref[:, lcols]
            glu = jnp.minimum(glu, SWIGLU_LIMIT)
            lin = jnp.clip(lin, -SWIGLU_LIMIT, SWIGLU_LIMIT)
            act_scr[:, gcols] = (glu * jax.nn.sigmoid(SWIGLU_ALPHA * glu) * (lin + 1.0)).astype(BF16)
        o_ref[...] = (jnp.dot(act_scr[...], wd_buf[...], preferred_element_type=F32)
                      + bd_ref[...]).astype(o_ref.dtype)

    @pl.when(jnp.logical_not(active))
    def _():
        o_ref[...] = jnp.zeros_like(o_ref)


WEIGHT_STAGE_ROWS = 64
WEIGHT_STAGE_SLOTS = 8


def _experts(tile_e, n_used, x_sorted, w_gu, b_gu, w_d, b_d, tm):
    n_rows, d = x_sorted.shape
    f = w_d.shape[1]
    chunk = WEIGHT_STAGE_ROWS * WEIGHT_STAGE_SLOTS
    assert (2 * f) % d == 0 and d % chunk == 0 and f % (chunk * (2 * f // d)) == 0
    tf = _largest_divisor(f, (512, 256, 128))
    n_tiles = n_rows // tm
    grid_spec = pltpu.PrefetchScalarGridSpec(
        num_scalar_prefetch=2,
        grid=(n_tiles,),
        in_specs=[
            pl.BlockSpec((tm, d), lambda i, te, nu: (jnp.minimum(i, nu[0] - 1), 0)),
            pl.BlockSpec(memory_space=pl.ANY),
            pl.BlockSpec(memory_space=pl.ANY),
            pl.BlockSpec((None, 1, 2 * f), lambda i, te, nu: (te[i], 0, 0)),
            pl.BlockSpec((None, 1, d), lambda i, te, nu: (te[i], 0, 0)),
        ],
        out_specs=pl.BlockSpec((tm, d), lambda i, te, nu: (i, 0)),
        scratch_shapes=[
            pltpu.VMEM((d, 2 * f), BF16),
            pltpu.VMEM((f, d), BF16),
            pltpu.VMEM((tm, d), BF16),
            pltpu.VMEM((tm, f), BF16),
            pltpu.VMEM((WEIGHT_STAGE_SLOTS, WEIGHT_STAGE_ROWS, 2 * f), F32),
            pltpu.SemaphoreType.DMA((WEIGHT_STAGE_SLOTS,)),
        ],
    )
    return pl.pallas_call(
        functools.partial(_expert_kernel, tf=tf),
        grid_spec=grid_spec,
        out_shape=jax.ShapeDtypeStruct((n_rows, d), F32),
        compiler_params=_cparams("arbitrary"),
        name="experts",
    )(tile_e, n_used, x_sorted, w_gu, w_d, b_gu, b_d)


def _combine_kernel(dest_ref, dest_next_ref, y_ref, x1_ref, route_ref, nw_ref, o_ref, buf, sem,
                    *, final_norm):
    i = pl.program_id(0)
    n = pl.num_programs(0)
    tt = x1_ref.shape[0]
    slot = i % 2

    def gather(idx_ref, dst_slot):
        def issue(t, carry):
            for k in range(TOP_K):
                pltpu.make_async_copy(y_ref.at[pl.ds(idx_ref[t * TOP_K + k], 1)],
                                      buf.at[dst_slot, pl.ds(k * tt + t, 1)],
                                      sem.at[dst_slot]).start(priority=k % 2)
            return carry

        lax.fori_loop(0, tt, issue, 0)

    @pl.when(i == 0)
    def _():
        gather(dest_ref, slot)

    @pl.when(i + 1 < n)
    def _():
        gather(dest_next_ref, 1 - slot)

    pltpu.make_async_copy(y_ref.at[pl.ds(0, TOP_K * tt)], buf.at[slot], sem.at[slot]).wait()

    route = route_ref[...]
    x = x1_ref[...]
    for k in range(TOP_K):
        gate = route[:, 2 * TOP_K + k:2 * TOP_K + k + 1]
        x = x + gate * buf[slot, k * tt:(k + 1) * tt, :]
    if final_norm:
        x = x * lax.rsqrt(jnp.mean(x * x, axis=-1, keepdims=True) + EPS) * nw_ref[...]
    o_ref[...] = x


def _combine(dest_flat, y_sorted, x1, route, norm_w, final_norm):
    t, d = x1.shape
    tt = 256
    assert t % tt == 0
    n = t // tt
    return pl.pallas_call(
        functools.partial(_combine_kernel, final_norm=final_norm),
        grid=(n,),
        in_specs=[
            pl.BlockSpec((tt * TOP_K,), lambda i: (i,), memory_space=pltpu.SMEM),
            pl.BlockSpec((tt * TOP_K,), lambda i: (jnp.minimum(i + 1, n - 1),), memory_space=pltpu.SMEM),
            pl.BlockSpec(memory_space=pl.ANY),
            pl.BlockSpec((tt, d), lambda i: (i, 0)),
            pl.BlockSpec((tt, LANES), lambda i: (i, 0)),
            pl.BlockSpec((1, d), lambda i: (0, 0)),
        ],
        out_specs=pl.BlockSpec((tt, d), lambda i: (i, 0)),
        out_shape=jax.ShapeDtypeStruct((t, d), F32),
        scratch_shapes=[pltpu.VMEM((2, TOP_K * tt, d), F32), pltpu.SemaphoreType.DMA((2,))],
        compiler_params=_cparams("arbitrary"),
        name="combine",
    )(dest_flat, dest_flat, y_sorted, x1, route, norm_w)


MOE_ROW_TILE = 512


def _layer(x2, bsz, seq, norm_mix_w, w_in, conv_a_w, ssd_conv_w, ssd_conv_b, dt_bias_fw, dt_bias_bw,
           a_log_fw, a_log_bw, d_skip, ssd_norm_w, w_out, norm_ffn_w, w_router, b_router,
           w_gate_up, b_gate_up, w_down, b_down):
    t, d = x2.shape
    dc = conv_a_w.shape[1]
    ds = ssd_norm_w.shape[0]
    dxbc = ssd_conv_w.shape[1]
    heads = dt_bias_fw.shape[0]
    g = SSD_GROUPS
    hpg = heads // g
    n_main = 3 * dc + ds + dxbc
    n_experts = w_router.shape[1]

    w_main = w_in[:, :n_main].astype(BF16)
    w_dt = jnp.pad(w_in[:, n_main:], ((0, 0), (0, LANES - 2 * heads))).astype(BF16)
    proj, dt_raw = _inproj(x2, norm_mix_w.reshape(1, d), w_main, w_dt)
    proj3 = proj.reshape(bsz, seq, n_main)
    y_a = _conv_a(proj3, conv_a_w, dc)
    xbc_act = _conv_ssd(proj3, ssd_conv_w, ssd_conv_b.reshape(1, dxbc), 3 * dc + ds, dxbc)

    def head_rows(fw, bw):
        both = jnp.concatenate([fw.reshape(g, hpg), bw.reshape(g, hpg)], axis=1)
        return jnp.pad(both, ((0, 0), (0, SUBLANES - 2 * hpg))).reshape(g, SUBLANES, 1)

    dt_rows = dt_raw[:, :2 * heads].reshape(bsz, seq, 2, g, hpg)
    dt_rows = jnp.transpose(dt_rows, (0, 3, 2, 4, 1)).reshape(bsz, g, 2 * hpg, seq)
    dt_rows = jnp.pad(dt_rows, ((0, 0), (0, 0), (0, SUBLANES - 2 * hpg), (0, 0)))
    dskip = jnp.repeat(d_skip, SSD_HEAD_DIM).reshape(1, ds)
    y_b = _ssd(xbc_act, proj3, dt_rows, head_rows(dt_bias_fw, dt_bias_bw), head_rows(a_log_fw, a_log_bw),
               dskip, ssd_norm_w.reshape(1, ds), ds, 3 * dc)

    w_out_b = w_out.astype(BF16)
    w_r = jnp.pad(w_router, ((0, 0), (0, LANES - n_experts)))
    b_r = jnp.pad(b_router, (0, LANES - n_experts)).reshape(1, LANES)
    x1, h2, route, counts = _outproj_route(
        y_a.reshape(t, dc), y_b.reshape(t, ds), x2, w_out_b[:dc], w_out_b[dc:],
        norm_ffn_w.reshape(1, d), w_r, b_r, n_experts)

    tm = MOE_ROW_TILE
    n_slots = t * TOP_K
    n_tiles = n_slots // tm + n_experts
    cnt = counts[0, :n_experts].astype(jnp.int32)
    padded = (cnt + tm - 1) // tm * tm
    pends = jnp.cumsum(padded)
    pstarts = pends - padded
    e_idx = route[:, :TOP_K].astype(jnp.int32)
    rank = route[:, TOP_K:2 * TOP_K].astype(jnp.int32)
    dest = (pstarts[e_idx] + rank).reshape(n_slots)
    n_used = (pends[-1] // tm).astype(jnp.int32)
    tile_ids = jnp.arange(n_tiles, dtype=jnp.int32)
    tile_row0 = jnp.minimum(tile_ids, n_used - 1) * tm
    tile_e = jnp.minimum(jnp.sum((pends[None, :] <= tile_row0[:, None]).astype(jnp.int32), axis=1),
                         n_experts - 1)
    partial_last = (pends - tm) * ((cnt % tm) != 0) - (cnt % tm == 0)
    zflag = (jnp.any(partial_last[None, :] == (tile_ids * tm)[:, None], axis=1)
             | (tile_ids >= n_used)).astype(jnp.int32)

    x_sorted = _dispatch(zflag, dest, h2, n_tiles * tm, tm)
    f = w_down.shape[1]
    y_sorted = _experts(tile_e, n_used.reshape(1), x_sorted, w_gate_up,
                        b_gate_up.reshape(n_experts, 1, 2 * f), w_down,
                        b_down.reshape(n_experts, 1, d), tm)
    return dest, y_sorted, x1, route


def kernel(x, norm_mix_w, w_in, conv_a_w, ssd_conv_w, ssd_conv_b, dt_bias_fw, dt_bias_bw, a_log_fw,
           a_log_bw, d_skip, ssd_norm_w, w_out, norm_ffn_w, w_router, b_router, w_gate_up, b_gate_up,
           w_down, b_down, norm_final_w):
    bsz, seq, d = x.shape
    depth = w_in.shape[0]
    x2 = x.reshape(bsz * seq, d)
    for layer in range(depth):
        dest, y_sorted, x1, route = _layer(
            x2, bsz, seq, norm_mix_w[layer], w_in[layer], conv_a_w[layer], ssd_conv_w[layer],
            ssd_conv_b[layer], dt_bias_fw[layer], dt_bias_bw[layer], a_log_fw[layer], a_log_bw[layer],
            d_skip[layer], ssd_norm_w[layer], w_out[layer], norm_ffn_w[layer], w_router[layer],
            b_router[layer], w_gate_up[layer], b_gate_up[layer], w_down[layer], b_down[layer])
        x2 = _combine(dest, y_sorted, x1, route, norm_final_w.reshape(1, d), layer == depth - 1)
    return x2.reshape(bsz, seq, d)
```

```python
import functools

import jax
import jax.numpy as jnp
from jax import lax
from jax.experimental import pallas as pl
from jax.experimental.pallas import tpu as pltpu

F32 = jnp.float32
BF16 = jnp.bfloat16

EPS = 1e-5
SSD_HEAD_DIM = 64
SSD_GROUPS = 4
SSD_STATE = 128
SSD_CHUNK = 128
TOP_K = 4
SWIGLU_LIMIT = 7.0
SWIGLU_ALPHA = 1.702

LANES = 128
SUBLANES = 8
VMEM_LIMIT_BYTES = 56 * 1024 * 1024


def _largest_divisor(n, candidates):
    for c in candidates:
        if n % c == 0:
            return c
    raise ValueError(f"no tile in {candidates} divides {n}")


def _cparams(*sem):
    return pltpu.CompilerParams(dimension_semantics=tuple(sem), vmem_limit_bytes=VMEM_LIMIT_BYTES)


def _inproj_kernel(x_ref, nw_ref, w_ref, wdt_ref, o_ref, dt_ref, h_scr):
    @pl.when(pl.program_id(1) == 0)
    def _():
        x = x_ref[...]
        ms = jnp.mean(x * x, axis=-1, keepdims=True)
        h = (x * lax.rsqrt(ms + EPS) * nw_ref[...]).astype(BF16)
        h_scr[...] = h
        dt_ref[...] = jnp.dot(h, wdt_ref[...], preferred_element_type=F32)

    o_ref[...] = jnp.dot(h_scr[...], w_ref[...], preferred_element_type=F32).astype(o_ref.dtype)


def _inproj(x2, norm_w, w_main, w_dt):
    t, d = x2.shape
    n = w_main.shape[1]
    tm = _largest_divisor(t, (1024, 512, 256, 128))
    tn = _largest_divisor(n, (2048, 1536, 1024, 512, 256, 128))
    return pl.pallas_call(
        _inproj_kernel,
        grid=(t // tm, n // tn),
        in_specs=[
            pl.BlockSpec((tm, d), lambda i, j: (i, 0)),
            pl.BlockSpec((1, d), lambda i, j: (0, 0)),
            pl.BlockSpec((d, tn), lambda i, j: (0, j)),
            pl.BlockSpec((d, LANES), lambda i, j: (0, 0)),
        ],
        out_specs=[
            pl.BlockSpec((tm, tn), lambda i, j: (i, j)),
            pl.BlockSpec((tm, LANES), lambda i, j: (i, 0)),
        ],
        out_shape=[
            jax.ShapeDtypeStruct((t, n), BF16),
            jax.ShapeDtypeStruct((t, LANES), F32),
        ],
        scratch_shapes=[pltpu.VMEM((tm, d), BF16)],
        compiler_params=_cparams("parallel", "arbitrary"),
        name="inproj",
    )(x2, norm_w, w_main, w_dt)


def _centred_conv(v, w_ref):
    s = v.shape[0]
    width = w_ref.shape[0]
    half = width // 2
    row = lax.broadcasted_iota(jnp.int32, v.shape, 0)
    acc = v * w_ref[half:half + 1, :]
    for k in range(width):
        off = k - half
        if off == 0:
            continue
        shifted = pltpu.roll(v, (-off) % s, 0)
        valid = (row + off >= 0) & (row + off < s)
        acc = acc + jnp.where(valid, shifted, 0.0) * w_ref[k:k + 1, :]
    return acc


def _conv_a_kernel(gb_ref, gc_ref, u_ref, w_ref, o_ref):
    v = gc_ref[0].astype(F32) * u_ref[0].astype(F32)
    o_ref[0] = (gb_ref[0].astype(F32) * _centred_conv(v, w_ref)).astype(o_ref.dtype)


def _conv_a(proj3, conv_w, dc):
    b, s, _ = proj3.shape
    tc = _largest_divisor(dc, (512, 256, 128))
    nb = dc // tc
    blk = lambda off: pl.BlockSpec((1, s, tc), lambda i, j: (i, 0, off + j))
    return pl.pallas_call(
        _conv_a_kernel,
        grid=(b, nb),
        in_specs=[blk(0), blk(nb), blk(2 * nb),
                  pl.BlockSpec((conv_w.shape[0], tc), lambda i, j: (0, j))],
        out_specs=pl.BlockSpec((1, s, tc), lambda i, j: (i, 0, j)),
        out_shape=jax.ShapeDtypeStruct((b, s, dc), BF16),
        compiler_params=_cparams("parallel", "parallel"),
        name="conv_a",
    )(proj3, proj3, proj3, conv_w)


def _conv_ssd_kernel(x_ref, w_ref, b_ref, o_ref):
    y = _centred_conv(x_ref[0].astype(F32), w_ref) + b_ref[...]
    o_ref[0] = (y * jax.nn.sigmoid(y)).astype(o_ref.dtype)


def _conv_ssd(proj3, conv_w, conv_b, col0, dxbc):
    b, s, _ = proj3.shape
    tc = _largest_divisor(dxbc, (512, 256, 128))
    assert col0 % tc == 0
    off = col0 // tc
    return pl.pallas_call(
        _conv_ssd_kernel,
        grid=(b, dxbc // tc),
        in_specs=[pl.BlockSpec((1, s, tc), lambda i, j: (i, 0, off + j)),
                  pl.BlockSpec((conv_w.shape[0], tc), lambda i, j: (0, j)),
                  pl.BlockSpec((1, tc), lambda i, j: (0, j))],
        out_specs=pl.BlockSpec((1, s, tc), lambda i, j: (i, 0, j)),
        out_shape=jax.ShapeDtypeStruct((b, s, dxbc), BF16),
        compiler_params=_cparams("parallel", "parallel"),
        name="conv_ssd",
    )(proj3, conv_w, conv_b)


def _split_rows(v, passes):
    parts = []
    rem = v
    for _ in range(passes):
        term = rem.astype(BF16).astype(F32)
        parts.append(term)
        rem = rem - term
    while len(parts) % 2:
        parts.append(jnp.zeros_like(v))
    return jnp.concatenate(parts, axis=0).astype(BF16)


def _expander(n_rows, lanes_per_row, n_cols):
    r = lax.broadcasted_iota(jnp.int32, (n_rows, n_cols), 0) & (SUBLANES - 1)
    c = lax.broadcasted_iota(jnp.int32, (n_rows, n_cols), 1)
    lo = r * lanes_per_row
    return jnp.where((c >= lo) & (c < lo + lanes_per_row), 1.0, 0.0).astype(BF16)


def _expand(rows_bf16, expander):
    return lax.dot_general(rows_bf16, expander, (((0,), (0,)), ((), ())), preferred_element_type=F32)


def _ssd_kernel(xs_ref, b_ref, c_ref, z_ref, dt_ref, bias_ref, alog_ref, dskip_ref, nw_ref, o_ref,
                a_scr, dt_scr, qx_scr, ex_scr, acol_scr, st_scr, *, hpg):
    s = xs_ref.shape[1]
    gw = xs_ref.shape[2]
    nc = s // SSD_CHUNK
    L = SSD_CHUNK
    hd = SSD_HEAD_DIM

    raw = dt_ref[0, 0] + bias_ref[0]
    dt = jnp.maximum(raw, 0.0) + jnp.log1p(jnp.exp(-jnp.abs(raw)))
    dta = dt * (-jnp.exp(alog_ref[0]))
    pos = lax.broadcasted_iota(jnp.int32, (SUBLANES, s), 1) & (L - 1)
    head_row = lax.broadcasted_iota(jnp.int32, (SUBLANES, s), 0)
    pre = dta
    suf = dta
    sh = 1
    while sh < L:
        pre = pre + jnp.where(pos >= sh, pltpu.roll(pre, sh, 1), 0.0)
        suf = suf + jnp.where(pos < L - sh, pltpu.roll(suf, s - sh, 1), 0.0)
        sh *= 2
    acum = jnp.where(head_row < hpg, pre, suf)
    tot = pre + suf - dta
    for c in range(nc):
        lanes = slice(c * L, (c + 1) * L)
        a_scr[c] = acum[:, lanes]
        dt_scr[c] = dt[:, lanes]

    exp_head = _expander(2 * SUBLANES, hd, 2 * gw)
    exp_col = _expander(4 * SUBLANES, L, 2 * hpg * L)
    qx_scr[...] = _expand(_split_rows(dt * jnp.exp(tot - acum), 2), exp_head)
    ex_scr[...] = _expand(_split_rows(jnp.exp(acum), 2), exp_head)
    acol_scr[...] = _expand(_split_rows(acum, 3), exp_col)

    def phase1(i, run_b):
        c = nc - 1 - i
        row0 = pl.multiple_of(c * L, L)
        rows = pl.ds(row0, L)
        x = xs_ref[0, rows, :].astype(F32)
        qx = qx_scr[rows, :]
        xw = jnp.concatenate([x * qx[:, 0:gw], x * qx[:, gw:2 * gw]], axis=1).astype(BF16)
        contrib = lax.dot_general(b_ref[0, rows, :], xw, (((0,), (0,)), ((), ())),
                                  preferred_element_type=F32)
        st_scr[c, :, 0:gw] = contrib[:, 0:gw]
        st_scr[c, :, gw:2 * gw] = run_b
        dec_b = ex_scr[pl.ds(row0, 1), gw:2 * gw]
        return run_b * dec_b + contrib[:, gw:2 * gw]

    lax.fori_loop(0, nc, phase1, jnp.zeros((SSD_STATE, gw), F32), unroll=4)

    li = lax.broadcasted_iota(jnp.int32, (L, L), 0)
    si = lax.broadcasted_iota(jnp.int32, (L, L), 1)
    causal = li >= si
    below = li > si
    above = si > li
    lane_l = lax.broadcasted_iota(jnp.int32, (L, LANES), 1)

    def phase2(c, run_f):
        rows = pl.ds(pl.multiple_of(c * L, L), L)
        xb = xs_ref[0, rows, :]
        cm = c_ref[0, rows, :]
        cb = lax.dot_general(cm, b_ref[0, rows, :], (((1,), (1,)), ((), ())), preferred_element_type=F32)
        at = a_scr[c]
        dtt = dt_scr[c]
        acol = acol_scr[rows, :]
        ys = []
        for p in range(hpg // 2):
            xpair = xb[:, p * LANES:(p + 1) * LANES]
            y_pair = None
            for q in range(2):
                kf = 2 * p + q
                kb = hpg + kf
                seg = jnp.where(causal, acol[:, kf * L:(kf + 1) * L] - at[kf:kf + 1, :],
                                acol[:, kb * L:(kb + 1) * L] - at[kb:kb + 1, :])
                dt_f = dtt[kf:kf + 1, :]
                dt_b = dtt[kb:kb + 1, :]
                w = jnp.where(below, dt_f, jnp.where(above, dt_b, dt_f + dt_b))
                m = (cb * jnp.exp(seg) * w).astype(BF16)
                in_head = (lane_l < hd) if q == 0 else (lane_l >= hd)
                part = jnp.dot(m, jnp.where(in_head, xpair, jnp.zeros_like(xpair)),
                               preferred_element_type=F32)
                y_pair = part if y_pair is None else y_pair + part
            ys.append(y_pair)
        y = ys[0] if len(ys) == 1 else jnp.concatenate(ys, axis=1)
        ex = ex_scr[rows, :]
        cs_f = jnp.dot(cm, run_f.astype(BF16), preferred_element_type=F32)
        cs_b = jnp.dot(cm, st_scr[c, :, gw:2 * gw].astype(BF16), preferred_element_type=F32)
        y = y + cs_f * ex[:, 0:gw] + cs_b * ex[:, gw:2 * gw]
        y = y + xb.astype(F32) * dskip_ref[...]
        z = z_ref[0, rows, :].astype(F32)
        g = y * (z * jax.nn.sigmoid(z))
        g = g * lax.rsqrt(jnp.mean(g * g, axis=-1, keepdims=True) + EPS)
        o_ref[0, rows, :] = (g * nw_ref[...]).astype(o_ref.dtype)
        dec_f = ex[L - 1:L, 0:gw]
        return run_f * dec_f + st_scr[c, :, 0:gw]

    lax.fori_loop(0, nc, phase2, jnp.zeros((SSD_STATE, gw), F32), unroll=2)


def _ssd(xbc_act, proj3, dt_rows, bias_col, alog_col, dskip, norm_w, ds, z_col0):
    b, s, _ = xbc_act.shape
    g = SSD_GROUPS
    gw = ds // g
    hpg = gw // SSD_HEAD_DIM
    nc = s // SSD_CHUNK
    n = SSD_STATE
    L = SSD_CHUNK
    assert z_col0 % gw == 0 and ds % n == 0 and hpg % 2 == 0 and 2 * hpg <= SUBLANES
    kernel = functools.partial(_ssd_kernel, hpg=hpg)
    return pl.pallas_call(
        kernel,
        grid=(b, g),
        in_specs=[
            pl.BlockSpec((1, s, gw), lambda i, j: (i, 0, j)),
            pl.BlockSpec((1, s, n), lambda i, j: (i, 0, ds // n + j)),
            pl.BlockSpec((1, s, n), lambda i, j: (i, 0, ds // n + g + j)),
            pl.BlockSpec((1, s, gw), lambda i, j: (i, 0, z_col0 // gw + j)),
            pl.BlockSpec((1, 1, SUBLANES, s), lambda i, j: (i, j, 0, 0)),
            pl.BlockSpec((1, SUBLANES, 1), lambda i, j: (j, 0, 0)),
            pl.BlockSpec((1, SUBLANES, 1), lambda i, j: (j, 0, 0)),
            pl.BlockSpec((1, gw), lambda i, j: (0, j)),
            pl.BlockSpec((1, gw), lambda i, j: (0, j)),
        ],
        out_specs=pl.BlockSpec((1, s, gw), lambda i, j: (i, 0, j)),
        out_shape=jax.ShapeDtypeStruct((b, s, ds), BF16),
        scratch_shapes=[
            pltpu.VMEM((nc, SUBLANES, L), F32),
            pltpu.VMEM((nc, SUBLANES, L), F32),
            pltpu.VMEM((s, 2 * gw), F32),
            pltpu.VMEM((s, 2 * gw), F32),
            pltpu.VMEM((s, 2 * hpg * L), F32),
            pltpu.VMEM((nc, n, 2 * gw), F32),
        ],
        compiler_params=_cparams("parallel", "parallel"),
        name="ssd",
    )(xbc_act, xbc_act, xbc_act, proj3, dt_rows, bias_col, alog_col, dskip, norm_w)


def _split_bf16(a):
    hi = a.astype(BF16)
    lo = (a - hi.astype(F32)).astype(BF16)
    return hi, lo


def _outproj_kernel(ya_ref, yb_ref, x_ref, wa_ref, wb_ref, nw_ref, wr_ref, br_ref,
                    x1_ref, h_ref, route_ref, cnt_ref, carry_scr, *, n_experts):
    i = pl.program_id(0)
    tm = x_ref.shape[0]

    @pl.when(i == 0)
    def _():
        carry_scr[...] = jnp.zeros_like(carry_scr)

    x1 = x_ref[...] + jnp.dot(ya_ref[...], wa_ref[...], preferred_element_type=F32) \
        + jnp.dot(yb_ref[...], wb_ref[...], preferred_element_type=F32)
    x1_ref[...] = x1
    h = x1 * lax.rsqrt(jnp.mean(x1 * x1, axis=-1, keepdims=True) + EPS) * nw_ref[...]
    h_ref[...] = h

    h_hi, h_lo = _split_bf16(h)
    w_hi, w_lo = _split_bf16(wr_ref[...])
    logits = (jnp.dot(h_hi, w_hi, preferred_element_type=F32)
              + jnp.dot(h_hi, w_lo, preferred_element_type=F32)
              + jnp.dot(h_lo, w_hi, preferred_element_type=F32)) + br_ref[...]

    lane = lax.broadcasted_iota(jnp.int32, (tm, LANES), 1)
    neg = jnp.finfo(F32).min
    work = jnp.where(lane < n_experts, logits, neg)
    tops, idxs, sels = [], [], []
    for _k in range(TOP_K):
        m = jnp.max(work, axis=-1, keepdims=True)
        idx = jnp.min(jnp.where(work == m, lane, LANES), axis=-1, keepdims=True)
        sel = lane == idx
        work = jnp.where(sel, neg, work)
        tops.append(m)
        idxs.append(idx)
        sels.append(sel)
    exps = [jnp.exp(t - tops[0]) for t in tops]
    denom = exps[0]
    for e in exps[1:]:
        denom = denom + e
    inv = 1.0 / denom

    onehot = jnp.zeros((tm, LANES), F32)
    for sel in sels:
        onehot = onehot + jnp.where(sel, 1.0, 0.0)
    ri = lax.broadcasted_iota(jnp.int32, (tm, tm), 0)
    ci = lax.broadcasted_iota(jnp.int32, (tm, tm), 1)
    tri = jnp.where(ri > ci, 1.0, 0.0).astype(BF16)
    carry = carry_scr[0:1, :]
    prefix = jnp.dot(tri, onehot.astype(BF16), preferred_element_type=F32) + carry
    new_carry = carry + jnp.sum(onehot, axis=0, keepdims=True)
    carry_scr[...] = jnp.broadcast_to(new_carry, carry_scr.shape)
    cnt_ref[...] = jnp.broadcast_to(new_carry, cnt_ref.shape)

    route = jnp.zeros((tm, LANES), F32)
    for k in range(TOP_K):
        rank = jnp.sum(jnp.where(sels[k], prefix, 0.0), axis=-1, keepdims=True)
        route = jnp.where(lane == k, idxs[k].astype(F32), route)
        route = jnp.where(lane == TOP_K + k, rank, route)
        route = jnp.where(lane == 2 * TOP_K + k, exps[k] * inv, route)
    route_ref[...] = route


def _outproj_route(ya, yb, x2, wa, wb, norm_w, w_router, b_router, n_experts):
    t, d = x2.shape
    dc = ya.shape[1]
    ds = yb.shape[1]
    tm = _largest_divisor(t, (512, 256, 128))
    kernel = functools.partial(_outproj_kernel, n_experts=n_experts)
    const = lambda shape: pl.BlockSpec(shape, lambda i: (0, 0))
    return pl.pallas_call(
        kernel,
        grid=(t // tm,),
        in_specs=[
            pl.BlockSpec((tm, dc), lambda i: (i, 0)),
            pl.BlockSpec((tm, ds), lambda i: (i, 0)),
            pl.BlockSpec((tm, d), lambda i: (i, 0)),
            const((dc, d)), const((ds, d)), const((1, d)), const((d, LANES)), const((1, LANES)),
        ],
        out_specs=[
            pl.BlockSpec((tm, d), lambda i: (i, 0)),
            pl.BlockSpec((tm, d), lambda i: (i, 0)),
            pl.BlockSpec((tm, LANES), lambda i: (i, 0)),
            pl.BlockSpec((8, LANES), lambda i: (0, 0)),
        ],
        out_shape=[
            jax.ShapeDtypeStruct((t, d), F32),
            jax.ShapeDtypeStruct((t, d), F32),
            jax.ShapeDtypeStruct((t, LANES), F32),
            jax.ShapeDtypeStruct((8, LANES), F32),
        ],
        scratch_shapes=[pltpu.VMEM((8, LANES), F32)],
        compiler_params=_cparams("arbitrary"),
        name="outproj_route",
    )(ya, yb, x2, wa, wb, norm_w, w_router, b_router)


ZERO_ROWS = 256


def _dispatch_kernel(zflag_ref, dest_ref, h_ref, o_ref, zbuf, sem, zsem, *, tm):
    tt = h_ref.shape[0]
    n_tiles = o_ref.shape[0] // tm
    per_tile = tm // ZERO_ROWS

    def zero_copy(i, p):
        return pltpu.make_async_copy(zbuf, o_ref.at[pl.ds(i * tm + p * ZERO_ROWS, ZERO_ROWS)], zsem)

    @pl.when(pl.program_id(0) == 0)
    def _():
        zbuf[...] = jnp.zeros_like(zbuf)

        def start(i, carry):
            @pl.when(zflag_ref[i] != 0)
            def _():
                for p in range(per_tile):
                    zero_copy(i, p).start()
            return carry

        def wait(i, carry):
            @pl.when(zflag_ref[i] != 0)
            def _():
                for p in range(per_tile):
                    zero_copy(i, p).wait()
            return carry

        lax.fori_loop(0, n_tiles, start, 0)
        lax.fori_loop(0, n_tiles, wait, 0)

    def issue(t, carry):
        for k in range(TOP_K):
            pltpu.make_async_copy(h_ref.at[pl.ds(t, 1)], o_ref.at[pl.ds(dest_ref[t * TOP_K + k], 1)],
                                  sem).start(priority=k % 2)
        return carry

    lax.fori_loop(0, tt, issue, 0)
    for _k in range(TOP_K):
        pltpu.make_async_copy(h_ref, o_ref.at[pl.ds(0, tt)], sem).wait()


def _dispatch(zflag, dest_flat, h, n_rows, tm):
    t, d = h.shape
    tt = _largest_divisor(t, (512, 256))
    assert tm % ZERO_ROWS == 0
    grid_spec = pltpu.PrefetchScalarGridSpec(
        num_scalar_prefetch=1,
        grid=(t // tt,),
        in_specs=[
            pl.BlockSpec((tt * TOP_K,), lambda i, zf: (i,), memory_space=pltpu.SMEM),
            pl.BlockSpec((tt, d), lambda i, zf: (i, 0)),
        ],
        out_specs=pl.BlockSpec(memory_space=pl.ANY),
        scratch_shapes=[pltpu.VMEM((ZERO_ROWS, d), h.dtype), pltpu.SemaphoreType.DMA(()),
                        pltpu.SemaphoreType.DMA(())],
    )
    return pl.pallas_call(
        functools.partial(_dispatch_kernel, tm=tm),
        grid_spec=grid_spec,
        out_shape=jax.ShapeDtypeStruct((n_rows, d), h.dtype),
        compiler_params=_cparams("arbitrary"),
        name="dispatch",
    )(zflag, dest_flat, h)


def _load_cast(src, dst, stage, sem):
    n_slots, rows, width = stage.shape
    per = width // src.shape[1]
    w = src.shape[1]
    n = dst.shape[0] // (rows * per)

    def copies(k, slot):
        return [pltpu.make_async_copy(src.at[pl.ds((k * per + p) * rows, rows)],
                                      stage.at[slot, :, p * w:(p + 1) * w], sem.at[slot])
                for p in range(per)]

    for slot in range(n_slots):
        for c in copies(slot, slot):
            c.start()

    def body(kk, carry):
        for slot in range(n_slots):
            k = n_slots * kk + slot
            for c in copies(k, slot):
                c.wait()
            for p in range(per):
                row0 = pl.multiple_of((k * per + p) * rows, rows)
                dst[pl.ds(row0, rows), :] = stage[slot, :, p * w:(p + 1) * w].astype(BF16)

            @pl.when(k + n_slots < n)
            def _():
                for c in copies(k + n_slots, slot):
                    c.start()
        return carry

    lax.fori_loop(0, n // n_slots, body, 0)


def _expert_kernel(te_ref, nu_ref, x_ref, wgu_hbm, wd_hbm, bgu_ref, bd_ref, o_ref,
                   wgu_buf, wd_buf, xb_scr, act_scr, stage, sem, *, tf):
    i = pl.program_id(0)
    f = wd_buf.shape[0]
    e = te_ref[i]
    active = i < nu_ref[0]

    @pl.when(active & ((i == 0) | (e != te_ref[jnp.maximum(i - 1, 0)])))
    def _():
        _load_cast(wgu_hbm.at[e], wgu_buf, stage, sem)
        _load_cast(wd_hbm.at[e], wd_buf, stage, sem)

    @pl.when(active)
    def _():
        xb_scr[...] = x_ref[...].astype(BF16)
        xb = xb_scr[...]
        for j in range(f // tf):
            gcols = slice(j * tf, (j + 1) * tf)
            lcols = slice(f + j * tf, f + (j + 1) * tf)
            glu = jnp.dot(xb, wgu_buf[:, gcols], preferred_element_type=F32) + bgu_ref[:, gcols]
            lin = jnp.dot(xb, wgu_buf[:, lcols], preferred_element_type=F32) + bgu_ref[:, lcols]
            glu = jnp.minimum(glu, SWIGLU_LIMIT)
            lin = jnp.clip(lin, -SWIGLU_LIMIT, SWIGLU_LIMIT)
            act_scr[:, gcols] = (glu * jax.nn.sigmoid(SWIGLU_ALPHA * glu) * (lin + 1.0)).astype(BF16)
        o_ref[...] = (jnp.dot(act_scr[...], wd_buf[...], preferred_element_type=F32)
                      + bd_ref[...]).astype(o_ref.dtype)

    @pl.when(jnp.logical_not(active))
    def _():
        o_ref[...] = jnp.zeros_like(o_ref)


WEIGHT_STAGE_ROWS = 64
WEIGHT_STAGE_SLOTS = 8


def _experts(tile_e, n_used, x_sorted, w_gu, b_gu, w_d, b_d, tm):
    n_rows, d = x_sorted.shape
    f = w_d.shape[1]
    chunk = WEIGHT_STAGE_ROWS * WEIGHT_STAGE_SLOTS
    assert (2 * f) % d == 0 and d % chunk == 0 and f % (chunk * (2 * f // d)) == 0
    tf = _largest_divisor(f, (1024, 512, 256, 128))
    n_tiles = n_rows // tm
    grid_spec = pltpu.PrefetchScalarGridSpec(
        num_scalar_prefetch=2,
        grid=(n_tiles,),
        in_specs=[
            pl.BlockSpec((tm, d), lambda i, te, nu: (jnp.minimum(i, nu[0] - 1), 0)),
            pl.BlockSpec(memory_space=pl.ANY),
            pl.BlockSpec(memory_space=pl.ANY),
            pl.BlockSpec((None, 1, 2 * f), lambda i, te, nu: (te[i], 0, 0)),
            pl.BlockSpec((None, 1, d), lambda i, te, nu: (te[i], 0, 0)),
        ],
        out_specs=pl.BlockSpec((tm, d), lambda i, te, nu: (i, 0)),
        scratch_shapes=[
            pltpu.VMEM((d, 2 * f), BF16),
            pltpu.VMEM((f, d), BF16),
            pltpu.VMEM((tm, d), BF16),
            pltpu.VMEM((tm, f), BF16),
            pltpu.VMEM((WEIGHT_STAGE_SLOTS, WEIGHT_STAGE_ROWS, 2 * f), F32),
            pltpu.SemaphoreType.DMA((WEIGHT_STAGE_SLOTS,)),
        ],
    )
    return pl.pallas_call(
        functools.partial(_expert_kernel, tf=tf),
        grid_spec=grid_spec,
        out_shape=jax.ShapeDtypeStruct((n_rows, d), F32),
        compiler_params=_cparams("arbitrary"),
        name="experts",
    )(tile_e, n_used, x_sorted, w_gu, w_d, b_gu, b_d)


def _combine_kernel(dest_ref, dest_next_ref, y_ref, x1_ref, route_ref, nw_ref, o_ref, buf, sem,
                    *, final_norm):
    i = pl.program_id(0)
    n = pl.num_programs(0)
    tt = x1_ref.shape[0]
    slot = i % 2

    def gather(idx_ref, dst_slot):
        def issue(t, carry):
            for k in range(TOP_K):
                pltpu.make_async_copy(y_ref.at[pl.ds(idx_ref[t * TOP_K + k], 1)],
                                      buf.at[dst_slot, pl.ds(k * tt + t, 1)],
                                      sem.at[dst_slot]).start(priority=k % 2)
            return carry

        lax.fori_loop(0, tt, issue, 0)

    @pl.when(i == 0)
    def _():
        gather(dest_ref, slot)

    @pl.when(i + 1 < n)
    def _():
        gather(dest_next_ref, 1 - slot)

    pltpu.make_async_copy(y_ref.at[pl.ds(0, TOP_K * tt)], buf.at[slot], sem.at[slot]).wait()

    route = route_ref[...]
    x = x1_ref[...]
    for k in range(TOP_K):
        gate = route[:, 2 * TOP_K + k:2 * TOP_K + k + 1]
        x = x + gate * buf[slot, k * tt:(k + 1) * tt, :]
    if final_norm:
        x = x * lax.rsqrt(jnp.mean(x * x, axis=-1, keepdims=True) + EPS) * nw_ref[...]
    o_ref[...] = x


def _combine(dest_flat, y_sorted, x1, route, norm_w, final_norm):
    t, d = x1.shape
    tt = _largest_divisor(t, (512, 256))
    n = t // tt
    return pl.pallas_call(
        functools.partial(_combine_kernel, final_norm=final_norm),
        grid=(n,),
        in_specs=[
            pl.BlockSpec((tt * TOP_K,), lambda i: (i,), memory_space=pltpu.SMEM),
            pl.BlockSpec((tt * TOP_K,), lambda i: (jnp.minimum(i + 1, n - 1),), memory_space=pltpu.SMEM),
            pl.BlockSpec(memory_space=pl.ANY),
            pl.BlockSpec((tt, d), lambda i: (i, 0)),
            pl.BlockSpec((tt, LANES), lambda i: (i, 0)),
            pl.BlockSpec((1, d), lambda i: (0, 0)),
        ],
        out_specs=pl.BlockSpec((tt, d), lambda i: (i, 0)),
        out_shape=jax.ShapeDtypeStruct((t, d), F32),
        scratch_shapes=[pltpu.VMEM((2, TOP_K * tt, d), F32), pltpu.SemaphoreType.DMA((2,))],
        compiler_params=_cparams("arbitrary"),
        name="combine",
    )(dest_flat, dest_flat, y_sorted, x1, route, norm_w)


MOE_ROW_TILE = 512


def _layer(x2, bsz, seq, norm_mix_w, w_in, conv_a_w, ssd_conv_w, ssd_conv_b, dt_bias_fw, dt_bias_bw,
           a_log_fw, a_log_bw, d_skip, ssd_norm_w, w_out, norm_ffn_w, w_router, b_router,
           w_gate_up, b_gate_up, w_down, b_down):
    t, d = x2.shape
    dc = conv_a_w.shape[1]
    ds = ssd_norm_w.shape[0]
    dxbc = ssd_conv_w.shape[1]
    heads = dt_bias_fw.shape[0]
    g = SSD_GROUPS
    hpg = heads // g
    n_main = 3 * dc + ds + dxbc
    n_experts = w_router.shape[1]

    w_main = w_in[:, :n_main].astype(BF16)
    w_dt = jnp.pad(w_in[:, n_main:], ((0, 0), (0, LANES - 2 * heads))).astype(BF16)
    proj, dt_raw = _inproj(x2, norm_mix_w.reshape(1, d), w_main, w_dt)
    proj3 = proj.reshape(bsz, seq, n_main)
    y_a = _conv_a(proj3, conv_a_w, dc)
    xbc_act = _conv_ssd(proj3, ssd_conv_w, ssd_conv_b.reshape(1, dxbc), 3 * dc + ds, dxbc)

    def head_rows(fw, bw):
        both = jnp.concatenate([fw.reshape(g, hpg), bw.reshape(g, hpg)], axis=1)
        return jnp.pad(both, ((0, 0), (0, SUBLANES - 2 * hpg))).reshape(g, SUBLANES, 1)

    dt_rows = dt_raw[:, :2 * heads].reshape(bsz, seq, 2, g, hpg)
    dt_rows = jnp.transpose(dt_rows, (0, 3, 2, 4, 1)).reshape(bsz, g, 2 * hpg, seq)
    dt_rows = jnp.pad(dt_rows, ((0, 0), (0, 0), (0, SUBLANES - 2 * hpg), (0, 0)))
    dskip = jnp.repeat(d_skip, SSD_HEAD_DIM).reshape(1, ds)
    y_b = _ssd(xbc_act, proj3, dt_rows, head_rows(dt_bias_fw, dt_bias_bw), head_rows(a_log_fw, a_log_bw),
               dskip, ssd_norm_w.reshape(1, ds), ds, 3 * dc)

    w_out_b = w_out.astype(BF16)
    w_r = jnp.pad(w_router, ((0, 0), (0, LANES - n_experts)))
    b_r = jnp.pad(b_router, (0, LANES - n_experts)).reshape(1, LANES)
    x1, h2, route, counts = _outproj_route(
        y_a.reshape(t, dc), y_b.reshape(t, ds), x2, w_out_b[:dc], w_out_b[dc:],
        norm_ffn_w.reshape(1, d), w_r, b_r, n_experts)

    tm = MOE_ROW_TILE
    n_slots = t * TOP_K
    n_tiles = n_slots // tm + n_experts
    cnt = counts[0, :n_experts].astype(jnp.int32)
    padded = (cnt + tm - 1) // tm * tm
    pends = jnp.cumsum(padded)
    pstarts = pends - padded
    e_idx = route[:, :TOP_K].astype(jnp.int32)
    rank = route[:, TOP_K:2 * TOP_K].astype(jnp.int32)
    dest = (pstarts[e_idx] + rank).reshape(n_slots)
    n_used = (pends[-1] // tm).astype(jnp.int32)
    tile_ids = jnp.arange(n_tiles, dtype=jnp.int32)
    tile_row0 = jnp.minimum(tile_ids, n_used - 1) * tm
    tile_e = jnp.minimum(jnp.sum((pends[None, :] <= tile_row0[:, None]).astype(jnp.int32), axis=1),
                         n_experts - 1)
    partial_last = (pends - tm) * ((cnt % tm) != 0) - (cnt % tm == 0)
    zflag = (jnp.any(partial_last[None, :] == (tile_ids * tm)[:, None], axis=1)
             | (tile_ids >= n_used)).astype(jnp.int32)

    x_sorted = _dispatch(zflag, dest, h2, n_tiles * tm, tm)
    f = w_down.shape[1]
    y_sorted = _experts(tile_e, n_used.reshape(1), x_sorted, w_gate_up,
                        b_gate_up.reshape(n_experts, 1, 2 * f), w_down,
                        b_down.reshape(n_experts, 1, d), tm)
    return dest, y_sorted, x1, route


def kernel(x, norm_mix_w, w_in, conv_a_w, ssd_conv_w, ssd_conv_b, dt_bias_fw, dt_bias_bw, a_log_fw,
           a_log_bw, d_skip, ssd_norm_w, w_out, norm_ffn_w, w_router, b_router, w_gate_up, b_gate_up,
           w_down, b_down, norm_final_w):
    bsz, seq, d = x.shape
    depth = w_in.shape[0]
    x2 = x.reshape(bsz * seq, d)
    for layer in range(depth):
        dest, y_sorted, x1, route = _layer(
            x2, bsz, seq, norm_mix_w[layer], w_in[layer], conv_a_w[layer], ssd_conv_w[layer],
            ssd_conv_b[layer], dt_bias_fw[layer], dt_bias_bw[layer], a_log_fw[layer], a_log_bw[layer],
            d_skip[layer], ssd_norm_w[layer], w_out[layer], norm_ffn_w[layer], w_router[layer],
            b_router[layer], w_gate_up[layer], b_gate_up[layer], w_down[layer], b_down[layer])
        x2 = _combine(dest, y_sorted, x1, route, norm_final_w.reshape(1, d), layer == depth - 1)
    return x2.reshape(bsz, seq, d)
```

```python
import functools

import jax
import jax.numpy as jnp
from jax import lax
from jax.experimental import pallas as pl
from jax.experimental.pallas import tpu as pltpu

F32 = jnp.float32
BF16 = jnp.bfloat16

EPS = 1e-5
SSD_HEAD_DIM = 64
SSD_GROUPS = 4
SSD_STATE = 128
SSD_CHUNK = 128
TOP_K = 4
SWIGLU_LIMIT = 7.0
SWIGLU_ALPHA = 1.702

LANES = 128
SUBLANES = 8
VMEM_LIMIT_BYTES = 56 * 1024 * 1024


def _largest_divisor(n, candidates):
    for c in candidates:
        if n % c == 0:
            return c
    raise ValueError(f"no tile in {candidates} divides {n}")


def _cparams(*sem):
    return pltpu.CompilerParams(dimension_semantics=tuple(sem), vmem_limit_bytes=VMEM_LIMIT_BYTES)


def _inproj_kernel(x_ref, nw_ref, w_ref, wdt_ref, o_ref, dt_ref, h_scr):
    @pl.when(pl.program_id(1) == 0)
    def _():
        x = x_ref[...]
        ms = jnp.mean(x * x, axis=-1, keepdims=True)
        h = (x * lax.rsqrt(ms + EPS) * nw_ref[...]).astype(BF16)
        h_scr[...] = h
        dt_ref[...] = jnp.dot(h, wdt_ref[...], preferred_element_type=F32).T

    o_ref[...] = jnp.dot(h_scr[...], w_ref[...], preferred_element_type=F32).astype(o_ref.dtype)


def _inproj(x2, norm_w, w_main, w_dt):
    t, d = x2.shape
    n = w_main.shape[1]
    tm = _largest_divisor(t, (1024, 512, 256, 128))
    tn = _largest_divisor(n, (2048, 1536, 1024, 512, 256, 128))
    return pl.pallas_call(
        _inproj_kernel,
        grid=(t // tm, n // tn),
        in_specs=[
            pl.BlockSpec((tm, d), lambda i, j: (i, 0)),
            pl.BlockSpec((1, d), lambda i, j: (0, 0)),
            pl.BlockSpec((d, tn), lambda i, j: (0, j)),
            pl.BlockSpec((d, LANES), lambda i, j: (0, 0)),
        ],
        out_specs=[
            pl.BlockSpec((tm, tn), lambda i, j: (i, j)),
            pl.BlockSpec((LANES, tm), lambda i, j: (0, i)),
        ],
        out_shape=[
            jax.ShapeDtypeStruct((t, n), BF16),
            jax.ShapeDtypeStruct((LANES, t), F32),
        ],
        scratch_shapes=[pltpu.VMEM((tm, d), BF16)],
        compiler_params=_cparams("parallel", "arbitrary"),
        name="inproj",
    )(x2, norm_w, w_main, w_dt)


def _centred_conv(v, w_ref):
    s = v.shape[0]
    width = w_ref.shape[0]
    half = width // 2
    row = lax.broadcasted_iota(jnp.int32, v.shape, 0)
    acc = v * w_ref[half:half + 1, :]
    for k in range(width):
        off = k - half
        if off == 0:
            continue
        shifted = pltpu.roll(v, (-off) % s, 0)
        valid = (row + off >= 0) & (row + off < s)
        acc = acc + jnp.where(valid, shifted, 0.0) * w_ref[k:k + 1, :]
    return acc


def _conv_a_kernel(gb_ref, gc_ref, u_ref, w_ref, o_ref):
    v = gc_ref[0].astype(F32) * u_ref[0].astype(F32)
    o_ref[0] = (gb_ref[0].astype(F32) * _centred_conv(v, w_ref)).astype(o_ref.dtype)


def _conv_a(proj3, conv_w, dc):
    b, s, _ = proj3.shape
    tc = _largest_divisor(dc, (512, 256, 128))
    nb = dc // tc
    blk = lambda off: pl.BlockSpec((1, s, tc), lambda i, j: (i, 0, off + j))
    return pl.pallas_call(
        _conv_a_kernel,
        grid=(b, nb),
        in_specs=[blk(0), blk(nb), blk(2 * nb),
                  pl.BlockSpec((conv_w.shape[0], tc), lambda i, j: (0, j))],
        out_specs=pl.BlockSpec((1, s, tc), lambda i, j: (i, 0, j)),
        out_shape=jax.ShapeDtypeStruct((b, s, dc), BF16),
        compiler_params=_cparams("parallel", "parallel"),
        name="conv_a",
    )(proj3, proj3, proj3, conv_w)


def _conv_ssd_kernel(x_ref, w_ref, b_ref, o_ref):
    y = _centred_conv(x_ref[0].astype(F32), w_ref) + b_ref[...]
    o_ref[0] = (y * jax.nn.sigmoid(y)).astype(o_ref.dtype)


def _conv_ssd(proj3, conv_w, conv_b, col0, dxbc):
    b, s, _ = proj3.shape
    tc = _largest_divisor(dxbc, (512, 256, 128))
    assert col0 % tc == 0
    off = col0 // tc
    return pl.pallas_call(
        _conv_ssd_kernel,
        grid=(b, dxbc // tc),
        in_specs=[pl.BlockSpec((1, s, tc), lambda i, j: (i, 0, off + j)),
                  pl.BlockSpec((conv_w.shape[0], tc), lambda i, j: (0, j)),
                  pl.BlockSpec((1, tc), lambda i, j: (0, j))],
        out_specs=pl.BlockSpec((1, s, tc), lambda i, j: (i, 0, j)),
        out_shape=jax.ShapeDtypeStruct((b, s, dxbc), BF16),
        compiler_params=_cparams("parallel", "parallel"),
        name="conv_ssd",
    )(proj3, conv_w, conv_b)


def _split_rows(v, passes):
    parts = []
    rem = v
    for _ in range(passes):
        term = rem.astype(BF16).astype(F32)
        parts.append(term)
        rem = rem - term
    while len(parts) % 2:
        parts.append(jnp.zeros_like(v))
    return jnp.concatenate(parts, axis=0).astype(BF16)


def _expander(n_rows, lanes_per_row, n_cols):
    r = lax.broadcasted_iota(jnp.int32, (n_rows, n_cols), 0) & (SUBLANES - 1)
    c = lax.broadcasted_iota(jnp.int32, (n_rows, n_cols), 1)
    lo = r * lanes_per_row
    return jnp.where((c >= lo) & (c < lo + lanes_per_row), 1.0, 0.0).astype(BF16)


def _expand(rows_bf16, expander):
    return lax.dot_general(rows_bf16, expander, (((0,), (0,)), ((), ())), preferred_element_type=F32)


def _ssd_kernel(xs_ref, b_ref, c_ref, z_ref, dt_ref, bias_ref, alog_ref, dskip_ref, nw_ref, o_ref,
                a_scr, dt_scr, qx_scr, ex_scr, acol_scr, st_scr, *, hpg):
    s = xs_ref.shape[1]
    gw = xs_ref.shape[2]
    nc = s // SSD_CHUNK
    L = SSD_CHUNK
    hd = SSD_HEAD_DIM

    raw = dt_ref[...] + bias_ref[0]
    dt = jnp.maximum(raw, 0.0) + jnp.log1p(jnp.exp(-jnp.abs(raw)))
    dta = dt * (-jnp.exp(alog_ref[0]))
    pos = lax.broadcasted_iota(jnp.int32, (SUBLANES, s), 1) & (L - 1)
    head_row = lax.broadcasted_iota(jnp.int32, (SUBLANES, s), 0)
    pre = dta
    suf = dta
    sh = 1
    while sh < L:
        pre = pre + jnp.where(pos >= sh, pltpu.roll(pre, sh, 1), 0.0)
        suf = suf + jnp.where(pos < L - sh, pltpu.roll(suf, s - sh, 1), 0.0)
        sh *= 2
    acum = jnp.where(head_row < hpg, pre, suf)
    tot = pre + suf - dta
    for c in range(nc):
        lanes = slice(c * L, (c + 1) * L)
        a_scr[c] = acum[:, lanes]
        dt_scr[c] = dt[:, lanes]

    exp_head = _expander(2 * SUBLANES, hd, 2 * gw)
    exp_col = _expander(4 * SUBLANES, L, 2 * hpg * L)
    qx_scr[...] = _expand(_split_rows(dt * jnp.exp(tot - acum), 2), exp_head)
    ex_scr[...] = _expand(_split_rows(jnp.exp(acum), 2), exp_head)
    acol_scr[...] = _expand(_split_rows(acum, 3), exp_col)

    def phase1(i, run_b):
        c = nc - 1 - i
        row0 = pl.multiple_of(c * L, L)
        rows = pl.ds(row0, L)
        x = xs_ref[0, rows, :].astype(F32)
        qx = qx_scr[rows, :]
        xw = jnp.concatenate([x * qx[:, 0:gw], x * qx[:, gw:2 * gw]], axis=1).astype(BF16)
        contrib = lax.dot_general(b_ref[0, rows, :], xw, (((0,), (0,)), ((), ())),
                                  preferred_element_type=F32)
        st_scr[c, :, 0:gw] = contrib[:, 0:gw]
        st_scr[c, :, gw:2 * gw] = run_b
        dec_b = ex_scr[pl.ds(row0, 1), gw:2 * gw]
        return run_b * dec_b + contrib[:, gw:2 * gw]

    lax.fori_loop(0, nc, phase1, jnp.zeros((SSD_STATE, gw), F32), unroll=4)

    li = lax.broadcasted_iota(jnp.int32, (L, L), 0)
    si = lax.broadcasted_iota(jnp.int32, (L, L), 1)
    causal = li >= si
    below = li > si
    above = si > li
    lane_l = lax.broadcasted_iota(jnp.int32, (L, LANES), 1)

    def phase2(c, run_f):
        rows = pl.ds(pl.multiple_of(c * L, L), L)
        xb = xs_ref[0, rows, :]
        cm = c_ref[0, rows, :]
        cb = lax.dot_general(cm, b_ref[0, rows, :], (((1,), (1,)), ((), ())), preferred_element_type=F32)
        at = a_scr[c]
        dtt = dt_scr[c]
        acol = acol_scr[rows, :]
        ys = []
        for p in range(hpg // 2):
            xpair = xb[:, p * LANES:(p + 1) * LANES]
            y_pair = None
            for q in range(2):
                kf = 2 * p + q
                kb = hpg + kf
                seg = jnp.where(causal, acol[:, kf * L:(kf + 1) * L] - at[kf:kf + 1, :],
                                acol[:, kb * L:(kb + 1) * L] - at[kb:kb + 1, :])
                dt_f = dtt[kf:kf + 1, :]
                dt_b = dtt[kb:kb + 1, :]
                w = jnp.where(below, dt_f, jnp.where(above, dt_b, dt_f + dt_b))
                m = (cb * jnp.exp(seg) * w).astype(BF16)
                in_head = (lane_l < hd) if q == 0 else (lane_l >= hd)
                part = jnp.dot(m, jnp.where(in_head, xpair, jnp.zeros_like(xpair)),
                               preferred_element_type=F32)
                y_pair = part if y_pair is None else y_pair + part
            ys.append(y_pair)
        y = ys[0] if len(ys) == 1 else jnp.concatenate(ys, axis=1)
        ex = ex_scr[rows, :]
        cs_f = jnp.dot(cm, run_f.astype(BF16), preferred_element_type=F32)
        cs_b = jnp.dot(cm, st_scr[c, :, gw:2 * gw].astype(BF16), preferred_element_type=F32)
        y = y + cs_f * ex[:, 0:gw] + cs_b * ex[:, gw:2 * gw]
        y = y + xb.astype(F32) * dskip_ref[...]
        z = z_ref[0, rows, :].astype(F32)
        g = y * (z * jax.nn.sigmoid(z))
        g = g * lax.rsqrt(jnp.mean(g * g, axis=-1, keepdims=True) + EPS)
        o_ref[0, rows, :] = (g * nw_ref[...]).astype(o_ref.dtype)
        dec_f = ex[L - 1:L, 0:gw]
        return run_f * dec_f + st_scr[c, :, 0:gw]

    lax.fori_loop(0, nc, phase2, jnp.zeros((SSD_STATE, gw), F32), unroll=2)


def _ssd(xbc_act, proj3, dt_rows, bias_col, alog_col, dskip, norm_w, ds, z_col0):
    b, s, _ = xbc_act.shape
    g = SSD_GROUPS
    gw = ds // g
    hpg = gw // SSD_HEAD_DIM
    nc = s // SSD_CHUNK
    n = SSD_STATE
    L = SSD_CHUNK
    assert z_col0 % gw == 0 and ds % n == 0 and hpg % 2 == 0 and 2 * hpg <= SUBLANES
    kernel = functools.partial(_ssd_kernel, hpg=hpg)
    return pl.pallas_call(
        kernel,
        grid=(b, g),
        in_specs=[
            pl.BlockSpec((1, s, gw), lambda i, j: (i, 0, j)),
            pl.BlockSpec((1, s, n), lambda i, j: (i, 0, ds // n + j)),
            pl.BlockSpec((1, s, n), lambda i, j: (i, 0, ds // n + g + j)),
            pl.BlockSpec((1, s, gw), lambda i, j: (i, 0, z_col0 // gw + j)),
            pl.BlockSpec((SUBLANES, s), lambda i, j: (j, i)),
            pl.BlockSpec((1, SUBLANES, 1), lambda i, j: (j, 0, 0)),
            pl.BlockSpec((1, SUBLANES, 1), lambda i, j: (j, 0, 0)),
            pl.BlockSpec((1, gw), lambda i, j: (0, j)),
            pl.BlockSpec((1, gw), lambda i, j: (0, j)),
        ],
        out_specs=pl.BlockSpec((1, s, gw), lambda i, j: (i, 0, j)),
        out_shape=jax.ShapeDtypeStruct((b, s, ds), BF16),
        scratch_shapes=[
            pltpu.VMEM((nc, SUBLANES, L), F32),
            pltpu.VMEM((nc, SUBLANES, L), F32),
            pltpu.VMEM((s, 2 * gw), F32),
            pltpu.VMEM((s, 2 * gw), F32),
            pltpu.VMEM((s, 2 * hpg * L), F32),
            pltpu.VMEM((nc, n, 2 * gw), F32),
        ],
        compiler_params=_cparams("parallel", "parallel"),
        name="ssd",
    )(xbc_act, xbc_act, xbc_act, proj3, dt_rows, bias_col, alog_col, dskip, norm_w)


def _split_bf16(a):
    hi = a.astype(BF16)
    lo = (a - hi.astype(F32)).astype(BF16)
    return hi, lo


def _outproj_kernel(ya_ref, yb_ref, x_ref, wa_ref, wb_ref, nw_ref, wr_ref, br_ref,
                    x1_ref, h_ref, route_ref, cnt_ref, carry_scr, *, n_experts):
    i = pl.program_id(0)
    tm = x_ref.shape[0]

    @pl.when(i == 0)
    def _():
        carry_scr[...] = jnp.zeros_like(carry_scr)

    x1 = x_ref[...] + jnp.dot(ya_ref[...], wa_ref[...], preferred_element_type=F32) \
        + jnp.dot(yb_ref[...], wb_ref[...], preferred_element_type=F32)
    x1_ref[...] = x1
    h = x1 * lax.rsqrt(jnp.mean(x1 * x1, axis=-1, keepdims=True) + EPS) * nw_ref[...]
    h_ref[...] = h

    h_hi, h_lo = _split_bf16(h)
    w_hi, w_lo = _split_bf16(wr_ref[...])
    logits = (jnp.dot(h_hi, w_hi, preferred_element_type=F32)
              + jnp.dot(h_hi, w_lo, preferred_element_type=F32)
              + jnp.dot(h_lo, w_hi, preferred_element_type=F32)) + br_ref[...]

    lane = lax.broadcasted_iota(jnp.int32, (tm, LANES), 1)
    neg = jnp.finfo(F32).min
    work = jnp.where(lane < n_experts, logits, neg)
    tops, idxs, sels = [], [], []
    for _k in range(TOP_K):
        m = jnp.max(work, axis=-1, keepdims=True)
        idx = jnp.min(jnp.where(work == m, lane, LANES), axis=-1, keepdims=True)
        sel = lane == idx
        work = jnp.where(sel, neg, work)
        tops.append(m)
        idxs.append(idx)
        sels.append(sel)
    exps = [jnp.exp(t - tops[0]) for t in tops]
    denom = exps[0]
    for e in exps[1:]:
        denom = denom + e
    inv = 1.0 / denom

    onehot = jnp.zeros((tm, LANES), F32)
    for sel in sels:
        onehot = onehot + jnp.where(sel, 1.0, 0.0)
    ri = lax.broadcasted_iota(jnp.int32, (tm, tm), 0)
    ci = lax.broadcasted_iota(jnp.int32, (tm, tm), 1)
    tri = jnp.where(ri > ci, 1.0, 0.0).astype(BF16)
    carry = carry_scr[0:1, :]
    prefix = jnp.dot(tri, onehot.astype(BF16), preferred_element_type=F32) + carry
    new_carry = carry + jnp.sum(onehot, axis=0, keepdims=True)
    carry_scr[...] = jnp.broadcast_to(new_carry, carry_scr.shape)
    cnt_ref[...] = jnp.broadcast_to(new_carry, cnt_ref.shape)

    route = jnp.zeros((tm, LANES), F32)
    for k in range(TOP_K):
        rank = jnp.sum(jnp.where(sels[k], prefix, 0.0), axis=-1, keepdims=True)
        route = jnp.where(lane == k, idxs[k].astype(F32), route)
        route = jnp.where(lane == TOP_K + k, rank, route)
        route = jnp.where(lane == 2 * TOP_K + k, exps[k] * inv, route)
    route_ref[...] = route


def _outproj_route(ya, yb, x2, wa, wb, norm_w, w_router, b_router, n_experts):
    t, d = x2.shape
    dc = ya.shape[1]
    ds = yb.shape[1]
    tm = _largest_divisor(t, (512, 256, 128))
    kernel = functools.partial(_outproj_kernel, n_experts=n_experts)
    const = lambda shape: pl.BlockSpec(shape, lambda i: (0, 0))
    return pl.pallas_call(
        kernel,
        grid=(t // tm,),
        in_specs=[
            pl.BlockSpec((tm, dc), lambda i: (i, 0)),
            pl.BlockSpec((tm, ds), lambda i: (i, 0)),
            pl.BlockSpec((tm, d), lambda i: (i, 0)),
            const((dc, d)), const((ds, d)), const((1, d)), const((d, LANES)), const((1, LANES)),
        ],
        out_specs=[
            pl.BlockSpec((tm, d), lambda i: (i, 0)),
            pl.BlockSpec((tm, d), lambda i: (i, 0)),
            pl.BlockSpec((tm, LANES), lambda i: (i, 0)),
            pl.BlockSpec((8, LANES), lambda i: (0, 0)),
        ],
        out_shape=[
            jax.ShapeDtypeStruct((t, d), F32),
            jax.ShapeDtypeStruct((t, d), F32),
            jax.ShapeDtypeStruct((t, LANES), F32),
            jax.ShapeDtypeStruct((8, LANES), F32),
        ],
        scratch_shapes=[pltpu.VMEM((8, LANES), F32)],
        compiler_params=_cparams("arbitrary"),
        name="outproj_route",
    )(ya, yb, x2, wa, wb, norm_w, w_router, b_router)


ZERO_ROWS = 256


def _dispatch_kernel(zflag_ref, dest_ref, h_ref, o_ref, zbuf, sem, zsem, *, tm):
    tt = h_ref.shape[0]
    n_tiles = o_ref.shape[0] // tm
    per_tile = tm // ZERO_ROWS

    def zero_copy(i, p):
        return pltpu.make_async_copy(zbuf, o_ref.at[pl.ds(i * tm + p * ZERO_ROWS, ZERO_ROWS)], zsem)

    @pl.when(pl.program_id(0) == 0)
    def _():
        zbuf[...] = jnp.zeros_like(zbuf)

        def start(i, carry):
            @pl.when(zflag_ref[i] != 0)
            def _():
                for p in range(per_tile):
                    zero_copy(i, p).start()
            return carry

        def wait(i, carry):
            @pl.when(zflag_ref[i] != 0)
            def _():
                for p in range(per_tile):
                    zero_copy(i, p).wait()
            return carry

        lax.fori_loop(0, n_tiles, start, 0)
        lax.fori_loop(0, n_tiles, wait, 0)

    def issue(t, carry):
        for k in range(TOP_K):
            pltpu.make_async_copy(h_ref.at[pl.ds(t, 1)], o_ref.at[pl.ds(dest_ref[t * TOP_K + k], 1)],
                                  sem).start(priority=k % 2)
        return carry

    lax.fori_loop(0, tt, issue, 0)
    for _k in range(TOP_K):
        pltpu.make_async_copy(h_ref, o_ref.at[pl.ds(0, tt)], sem).wait()


def _dispatch(zflag, dest_flat, h, n_rows, tm):
    t, d = h.shape
    tt = _largest_divisor(t, (512, 256))
    assert tm % ZERO_ROWS == 0
    grid_spec = pltpu.PrefetchScalarGridSpec(
        num_scalar_prefetch=1,
        grid=(t // tt,),
        in_specs=[
            pl.BlockSpec((tt * TOP_K,), lambda i, zf: (i,), memory_space=pltpu.SMEM),
            pl.BlockSpec((tt, d), lambda i, zf: (i, 0)),
        ],
        out_specs=pl.BlockSpec(memory_space=pl.ANY),
        scratch_shapes=[pltpu.VMEM((ZERO_ROWS, d), h.dtype), pltpu.SemaphoreType.DMA(()),
                        pltpu.SemaphoreType.DMA(())],
    )
    return pl.pallas_call(
        functools.partial(_dispatch_kernel, tm=tm),
        grid_spec=grid_spec,
        out_shape=jax.ShapeDtypeStruct((n_rows, d), h.dtype),
        compiler_params=_cparams("arbitrary"),
        name="dispatch",
    )(zflag, dest_flat, h)


def _load_cast(src, dst, stage, sem):
    n_slots, rows, width = stage.shape
    per = width // src.shape[1]
    w = src.shape[1]
    n = dst.shape[0] // (rows * per)

    def copies(k, slot):
        return [pltpu.make_async_copy(src.at[pl.ds((k * per + p) * rows, rows)],
                                      stage.at[slot, :, p * w:(p + 1) * w], sem.at[slot])
                for p in range(per)]

    for slot in range(n_slots):
        for c in copies(slot, slot):
            c.start()

    def body(kk, carry):
        for slot in range(n_slots):
            k = n_slots * kk + slot
            for c in copies(k, slot):
                c.wait()
            for p in range(per):
                row0 = pl.multiple_of((k * per + p) * rows, rows)
                dst[pl.ds(row0, rows), :] = stage[slot, :, p * w:(p + 1) * w].astype(BF16)

            @pl.when(k + n_slots < n)
            def _():
                for c in copies(k + n_slots, slot):
                    c.start()
        return carry

    lax.fori_loop(0, n // n_slots, body, 0)


def _expert_kernel(te_ref, nu_ref, nb_ref, x_ref, wgu_hbm, wd_hbm, bgu_ref, bd_ref, o_ref,
                   wgu_buf, wd_buf, xb_scr, act_scr, stage, sem, *, tf):
    i = pl.program_id(0)
    tm = x_ref.shape[0]
    f = wd_buf.shape[0]
    e = te_ref[i]
    active = i < nu_ref[0]

    @pl.when(active & ((i == 0) | (e != te_ref[jnp.maximum(i - 1, 0)])))
    def _():
        _load_cast(wgu_hbm.at[e], wgu_buf, stage, sem)
        _load_cast(wd_hbm.at[e], wd_buf, stage, sem)

    def mlp(rows):
        xb_scr[0:rows, :] = x_ref[0:rows, :].astype(BF16)
        xb = xb_scr[0:rows, :]
        for j in range(f // tf):
            gcols = slice(j * tf, (j + 1) * tf)
            lcols = slice(f + j * tf, f + (j + 1) * tf)
            glu = jnp.dot(xb, wgu_buf[:, gcols], preferred_element_type=F32) + bgu_ref[:, gcols]
            lin = jnp.dot(xb, wgu_buf[:, lcols], preferred_element_type=F32) + bgu_ref[:, lcols]
            glu = jnp.minimum(glu, SWIGLU_LIMIT)
            lin = jnp.clip(lin, -SWIGLU_LIMIT, SWIGLU_LIMIT)
            act_scr[0:rows, gcols] = (glu * jax.nn.sigmoid(SWIGLU_ALPHA * glu) * (lin + 1.0)).astype(BF16)
        o_ref[0:rows, :] = (jnp.dot(act_scr[0:rows, :], wd_buf[...], preferred_element_type=F32)
                            + bd_ref[...]).astype(o_ref.dtype)
        if rows < tm:
            o_ref[rows:tm, :] = jnp.zeros((tm - rows, o_ref.shape[1]), o_ref.dtype)

    for blocks in range(1, tm // MOE_ROW_BLOCK + 1):
        pl.when(active & (nb_ref[i] == blocks))(functools.partial(mlp, blocks * MOE_ROW_BLOCK))

    @pl.when(jnp.logical_not(active))
    def _():
        o_ref[...] = jnp.zeros_like(o_ref)


WEIGHT_STAGE_ROWS = 64
WEIGHT_STAGE_SLOTS = 8


MOE_ROW_BLOCK = 128


def _experts(tile_e, n_used, tile_blocks, x_sorted, w_gu, b_gu, w_d, b_d, tm):
    n_rows, d = x_sorted.shape
    f = w_d.shape[1]
    chunk = WEIGHT_STAGE_ROWS * WEIGHT_STAGE_SLOTS
    assert (2 * f) % d == 0 and d % chunk == 0 and f % (chunk * (2 * f // d)) == 0
    assert tm % MOE_ROW_BLOCK == 0
    tf = _largest_divisor(f, (1024, 512, 256, 128))
    n_tiles = n_rows // tm
    grid_spec = pltpu.PrefetchScalarGridSpec(
        num_scalar_prefetch=3,
        grid=(n_tiles,),
        in_specs=[
            pl.BlockSpec((tm, d), lambda i, te, nu, nb: (jnp.minimum(i, nu[0] - 1), 0)),
            pl.BlockSpec(memory_space=pl.ANY),
            pl.BlockSpec(memory_space=pl.ANY),
            pl.BlockSpec((None, 1, 2 * f), lambda i, te, nu, nb: (te[i], 0, 0)),
            pl.BlockSpec((None, 1, d), lambda i, te, nu, nb: (te[i], 0, 0)),
        ],
        out_specs=pl.BlockSpec((tm, d), lambda i, te, nu, nb: (i, 0)),
        scratch_shapes=[
            pltpu.VMEM((d, 2 * f), BF16),
            pltpu.VMEM((f, d), BF16),
            pltpu.VMEM((tm, d), BF16),
            pltpu.VMEM((tm, f), BF16),
            pltpu.VMEM((WEIGHT_STAGE_SLOTS, WEIGHT_STAGE_ROWS, 2 * f), F32),
            pltpu.SemaphoreType.DMA((WEIGHT_STAGE_SLOTS,)),
        ],
    )
    return pl.pallas_call(
        functools.partial(_expert_kernel, tf=tf),
        grid_spec=grid_spec,
        out_shape=jax.ShapeDtypeStruct((n_rows, d), F32),
        compiler_params=_cparams("arbitrary"),
        name="experts",
    )(tile_e, n_used, tile_blocks, x_sorted, w_gu, w_d, b_gu, b_d)


def _combine_kernel(dest_ref, dest_next_ref, y_ref, x1_ref, route_ref, nw_ref, o_ref, buf, sem,
                    *, final_norm):
    i = pl.program_id(0)
    n = pl.num_programs(0)
    tt = x1_ref.shape[0]
    slot = i % 2

    def gather(idx_ref, dst_slot):
        def issue(t, carry):
            for k in range(TOP_K):
                pltpu.make_async_copy(y_ref.at[pl.ds(idx_ref[t * TOP_K + k], 1)],
                                      buf.at[dst_slot, pl.ds(k * tt + t, 1)],
                                      sem.at[dst_slot]).start(priority=k % 2)
            return carry

        lax.fori_loop(0, tt, issue, 0)

    @pl.when(i == 0)
    def _():
        gather(dest_ref, slot)

    @pl.when(i + 1 < n)
    def _():
        gather(dest_next_ref, 1 - slot)

    pltpu.make_async_copy(y_ref.at[pl.ds(0, TOP_K * tt)], buf.at[slot], sem.at[slot]).wait()

    route = route_ref[...]
    x = x1_ref[...]
    for k in range(TOP_K):
        gate = route[:, 2 * TOP_K + k:2 * TOP_K + k + 1]
        x = x + gate * buf[slot, k * tt:(k + 1) * tt, :]
    if final_norm:
        x = x * lax.rsqrt(jnp.mean(x * x, axis=-1, keepdims=True) + EPS) * nw_ref[...]
    o_ref[...] = x


def _combine(dest_flat, y_sorted, x1, route, norm_w, final_norm):
    t, d = x1.shape
    tt = _largest_divisor(t, (512, 256))
    n = t // tt
    return pl.pallas_call(
        functools.partial(_combine_kernel, final_norm=final_norm),
        grid=(n,),
        in_specs=[
            pl.BlockSpec((tt * TOP_K,), lambda i: (i,), memory_space=pltpu.SMEM),
            pl.BlockSpec((tt * TOP_K,), lambda i: (jnp.minimum(i + 1, n - 1),), memory_space=pltpu.SMEM),
            pl.BlockSpec(memory_space=pl.ANY),
            pl.BlockSpec((tt, d), lambda i: (i, 0)),
            pl.BlockSpec((tt, LANES), lambda i: (i, 0)),
            pl.BlockSpec((1, d), lambda i: (0, 0)),
        ],
        out_specs=pl.BlockSpec((tt, d), lambda i: (i, 0)),
        out_shape=jax.ShapeDtypeStruct((t, d), F32),
        scratch_shapes=[pltpu.VMEM((2, TOP_K * tt, d), F32), pltpu.SemaphoreType.DMA((2,))],
        compiler_params=_cparams("arbitrary"),
        name="combine",
    )(dest_flat, dest_flat, y_sorted, x1, route, norm_w)


MOE_ROW_TILE = 512


def _layer(x2, bsz, seq, norm_mix_w, w_in, conv_a_w, ssd_conv_w, ssd_conv_b, dt_bias_fw, dt_bias_bw,
           a_log_fw, a_log_bw, d_skip, ssd_norm_w, w_out, norm_ffn_w, w_router, b_router,
           w_gate_up, b_gate_up, w_down, b_down):
    t, d = x2.shape
    dc = conv_a_w.shape[1]
    ds = ssd_norm_w.shape[0]
    dxbc = ssd_conv_w.shape[1]
    heads = dt_bias_fw.shape[0]
    g = SSD_GROUPS
    hpg = heads // g
    n_main = 3 * dc + ds + dxbc
    n_experts = w_router.shape[1]

    w_main = w_in[:, :n_main].astype(BF16)
    w_dt = jnp.transpose(w_in[:, n_main:].reshape(d, 2, g, hpg), (0, 2, 1, 3)).reshape(d, g, 2 * hpg)
    w_dt = jnp.pad(w_dt, ((0, 0), (0, 0), (0, SUBLANES - 2 * hpg))).reshape(d, g * SUBLANES)
    w_dt = jnp.pad(w_dt, ((0, 0), (0, LANES - g * SUBLANES))).astype(BF16)
    proj, dt_rows = _inproj(x2, norm_mix_w.reshape(1, d), w_main, w_dt)
    proj3 = proj.reshape(bsz, seq, n_main)
    y_a = _conv_a(proj3, conv_a_w, dc)
    xbc_act = _conv_ssd(proj3, ssd_conv_w, ssd_conv_b.reshape(1, dxbc), 3 * dc + ds, dxbc)

    def head_rows(fw, bw):
        both = jnp.concatenate([fw.reshape(g, hpg), bw.reshape(g, hpg)], axis=1)
        return jnp.pad(both, ((0, 0), (0, SUBLANES - 2 * hpg))).reshape(g, SUBLANES, 1)

    dskip = jnp.repeat(d_skip, SSD_HEAD_DIM).reshape(1, ds)
    y_b = _ssd(xbc_act, proj3, dt_rows, head_rows(dt_bias_fw, dt_bias_bw), head_rows(a_log_fw, a_log_bw),
               dskip, ssd_norm_w.reshape(1, ds), ds, 3 * dc)

    w_out_b = w_out.astype(BF16)
    w_r = jnp.pad(w_router, ((0, 0), (0, LANES - n_experts)))
    b_r = jnp.pad(b_router, (0, LANES - n_experts)).reshape(1, LANES)
    x1, h2, route, counts = _outproj_route(
        y_a.reshape(t, dc), y_b.reshape(t, ds), x2, w_out_b[:dc], w_out_b[dc:],
        norm_ffn_w.reshape(1, d), w_r, b_r, n_experts)

    tm = MOE_ROW_TILE
    n_slots = t * TOP_K
    n_tiles = n_slots // tm + n_experts
    cnt = counts[0, :n_experts].astype(jnp.int32)
    padded = (cnt + tm - 1) // tm * tm
    pends = jnp.cumsum(padded)
    pstarts = pends - padded
    e_idx = route[:, :TOP_K].astype(jnp.int32)
    rank = route[:, TOP_K:2 * TOP_K].astype(jnp.int32)
    experts = jnp.arange(n_experts, dtype=jnp.int32)
    dest = (jnp.sum((e_idx[..., None] == experts) * pstarts, axis=-1) + rank).reshape(n_slots)
    n_used = (pends[-1] // tm).astype(jnp.int32)
    tile_ids = jnp.arange(n_tiles, dtype=jnp.int32)
    tile_row0 = jnp.minimum(tile_ids, n_used - 1) * tm
    tile_e = jnp.minimum(jnp.sum((pends[None, :] <= tile_row0[:, None]).astype(jnp.int32), axis=1),
                         n_experts - 1)
    seg_end = jnp.sum((tile_e[:, None] == experts) * (pstarts + cnt), axis=-1)
    tile_valid = jnp.clip(seg_end - tile_row0, 1, tm)
    tile_blocks = ((tile_valid + MOE_ROW_BLOCK - 1) // MOE_ROW_BLOCK).astype(jnp.int32)
    partial_last = (pends - tm) * ((cnt % tm) != 0) - (cnt % tm == 0)
    zflag = (jnp.any(partial_last[None, :] == (tile_ids * tm)[:, None], axis=1)
             | (tile_ids >= n_used)).astype(jnp.int32)

    x_sorted = _dispatch(zflag, dest, h2, n_tiles * tm, tm)
    f = w_down.shape[1]
    y_sorted = _experts(tile_e, n_used.reshape(1), tile_blocks, x_sorted, w_gate_up,
                        b_gate_up.reshape(n_experts, 1, 2 * f), w_down,
                        b_down.reshape(n_experts, 1, d), tm)
    return dest, y_sorted, x1, route


def kernel(x, norm_mix_w, w_in, conv_a_w, ssd_conv_w, ssd_conv_b, dt_bias_fw, dt_bias_bw, a_log_fw,
           a_log_bw, d_skip, ssd_norm_w, w_out, norm_ffn_w, w_router, b_router, w_gate_up, b_gate_up,
           w_down, b_down, norm_final_w):
    bsz, seq, d = x.shape
    depth = w_in.shape[0]
    x2 = x.reshape(bsz * seq, d)
    for layer in range(depth):
        dest, y_sorted, x1, route = _layer(
            x2, bsz, seq, norm_mix_w[layer], w_in[layer], conv_a_w[layer], ssd_conv_w[layer],
            ssd_conv_b[layer], dt_bias_fw[layer], dt_bias_bw[layer], a_log_fw[layer], a_log_bw[layer],
            d_skip[layer], ssd_norm_w[layer], w_out[layer], norm_ffn_w[layer], w_router[layer],
            b_router[layer], w_gate_up[layer], b_gate_up[layer], w_down[layer], b_down[layer])
        x2 = _combine(dest, y_sorted, x1, route, norm_final_w.reshape(1, d), layer == depth - 1)
    return x2.reshape(bsz, seq, d)
```

```python
import functools

import jax
import jax.numpy as jnp
from jax import lax
from jax.experimental import pallas as pl
from jax.experimental.pallas import tpu as pltpu

F32 = jnp.float32
BF16 = jnp.bfloat16

EPS = 1e-5
SSD_HEAD_DIM = 64
SSD_GROUPS = 4
SSD_STATE = 128
SSD_CHUNK = 128
TOP_K = 4
SWIGLU_LIMIT = 7.0
SWIGLU_ALPHA = 1.702

LANES = 128
SUBLANES = 8
VMEM_LIMIT_BYTES = 56 * 1024 * 1024


def _largest_divisor(n, candidates):
    for c in candidates:
        if n % c == 0:
            return c
    raise ValueError(f"no tile in {candidates} divides {n}")


def _cparams(*sem):
    return pltpu.CompilerParams(dimension_semantics=tuple(sem), vmem_limit_bytes=VMEM_LIMIT_BYTES)


def _inproj_kernel(x_ref, nw_ref, w_ref, wdt_ref, o_ref, dt_ref, h_scr):
    @pl.when(pl.program_id(1) == 0)
    def _():
        x = x_ref[...]
        ms = jnp.mean(x * x, axis=-1, keepdims=True)
        h = (x * lax.rsqrt(ms + EPS) * nw_ref[...]).astype(BF16)
        h_scr[...] = h
        dt_ref[...] = jnp.dot(h, wdt_ref[...], preferred_element_type=F32).T

    o_ref[...] = jnp.dot(h_scr[...], w_ref[...], preferred_element_type=F32).astype(o_ref.dtype)


def _inproj(x2, norm_w, w_main, w_dt):
    t, d = x2.shape
    n = w_main.shape[1]
    tm = _largest_divisor(t, (1024, 512, 256, 128))
    tn = _largest_divisor(n, (2048, 1536, 1024, 512, 256, 128))
    return pl.pallas_call(
        _inproj_kernel,
        grid=(t // tm, n // tn),
        in_specs=[
            pl.BlockSpec((tm, d), lambda i, j: (i, 0)),
            pl.BlockSpec((1, d), lambda i, j: (0, 0)),
            pl.BlockSpec((d, tn), lambda i, j: (0, j)),
            pl.BlockSpec((d, LANES), lambda i, j: (0, 0)),
        ],
        out_specs=[
            pl.BlockSpec((tm, tn), lambda i, j: (i, j)),
            pl.BlockSpec((LANES, tm), lambda i, j: (0, i)),
        ],
        out_shape=[
            jax.ShapeDtypeStruct((t, n), BF16),
            jax.ShapeDtypeStruct((LANES, t), F32),
        ],
        scratch_shapes=[pltpu.VMEM((tm, d), BF16)],
        compiler_params=_cparams("parallel", "arbitrary"),
        name="inproj",
    )(x2, norm_w, w_main, w_dt)


def _centred_conv(v, w_ref):
    s = v.shape[0]
    width = w_ref.shape[0]
    half = width // 2
    edge = 2 * SUBLANES
    assert half <= SUBLANES and s >= 2 * edge

    def taps(u, mask_rows):
        n = u.shape[0]
        row = lax.broadcasted_iota(jnp.int32, u.shape, 0)
        acc = u * w_ref[half:half + 1, :]
        for k in range(width):
            off = k - half
            if off == 0:
                continue
            shifted = pltpu.roll(u, (-off) % n, 0)
            if mask_rows:
                shifted = jnp.where((row + off >= 0) & (row + off < n), shifted, 0.0)
            acc = acc + shifted * w_ref[k:k + 1, :]
        return acc

    body = taps(v, False)
    head = taps(v[0:edge, :], True)[0:SUBLANES, :]
    tail = taps(v[s - edge:s, :], True)[SUBLANES:edge, :]
    return jnp.concatenate([head, body[SUBLANES:s - SUBLANES, :], tail], axis=0)


def _conv_a_kernel(gb_ref, gc_ref, u_ref, w_ref, o_ref):
    v = gc_ref[0].astype(F32) * u_ref[0].astype(F32)
    o_ref[0] = (gb_ref[0].astype(F32) * _centred_conv(v, w_ref)).astype(o_ref.dtype)


def _conv_a(proj3, conv_w, dc):
    b, s, _ = proj3.shape
    tc = _largest_divisor(dc, (512, 256, 128))
    nb = dc // tc
    blk = lambda off: pl.BlockSpec((1, s, tc), lambda i, j: (i, 0, off + j))
    return pl.pallas_call(
        _conv_a_kernel,
        grid=(b, nb),
        in_specs=[blk(0), blk(nb), blk(2 * nb),
                  pl.BlockSpec((conv_w.shape[0], tc), lambda i, j: (0, j))],
        out_specs=pl.BlockSpec((1, s, tc), lambda i, j: (i, 0, j)),
        out_shape=jax.ShapeDtypeStruct((b, s, dc), BF16),
        compiler_params=_cparams("parallel", "parallel"),
        name="conv_a",
    )(proj3, proj3, proj3, conv_w)


CONV_CHUNK = 128
CONV_HALO = 16


def _conv_ssd_kernel(x_ref, w_ref, b_ref, o_ref):
    s = x_ref.shape[1]
    width = w_ref.shape[0]
    half = width // 2
    win_rows = CONV_CHUNK + 2 * CONV_HALO
    assert half <= CONV_HALO and s % CONV_CHUNK == 0 and s >= win_rows

    r = lax.broadcasted_iota(jnp.int32, (CONV_CHUNK, win_rows), 0)
    j = lax.broadcasted_iota(jnp.int32, (CONV_CHUNK, win_rows), 1)
    shift = {}

    def shift_matrix(lead, off):
        if (lead, off) not in shift:
            shift[(lead, off)] = jnp.where(j == r + (lead + off), 1.0, 0.0).astype(x_ref.dtype)
        return shift[(lead, off)]

    for c in range(s // CONV_CHUNK):
        r0 = c * CONV_CHUNK
        w0 = min(max(r0 - CONV_HALO, 0), s - win_rows)
        win = x_ref[0, w0:w0 + win_rows, :]
        acc = x_ref[0, r0:r0 + CONV_CHUNK, :].astype(F32) * w_ref[half:half + 1, :]
        for k in range(width):
            off = k - half
            if off != 0:
                acc = acc + jnp.dot(shift_matrix(r0 - w0, off), win,
                                    preferred_element_type=F32) * w_ref[k:k + 1, :]
        y = acc + b_ref[...]
        o_ref[0, r0:r0 + CONV_CHUNK, :] = (y * jax.nn.sigmoid(y)).astype(o_ref.dtype)


def _conv_ssd(proj3, conv_w, conv_b, col0, dxbc):
    b, s, _ = proj3.shape
    tc = _largest_divisor(dxbc, (512, 256, 128))
    assert col0 % tc == 0
    off = col0 // tc
    return pl.pallas_call(
        _conv_ssd_kernel,
        grid=(b, dxbc // tc),
        in_specs=[pl.BlockSpec((1, s, tc), lambda i, j: (i, 0, off + j)),
                  pl.BlockSpec((conv_w.shape[0], tc), lambda i, j: (0, j)),
                  pl.BlockSpec((1, tc), lambda i, j: (0, j))],
        out_specs=pl.BlockSpec((1, s, tc), lambda i, j: (i, 0, j)),
        out_shape=jax.ShapeDtypeStruct((b, s, dxbc), BF16),
        compiler_params=_cparams("parallel", "parallel"),
        name="conv_ssd",
    )(proj3, conv_w, conv_b)


def _split_rows(v, passes):
    parts = []
    rem = v
    for _ in range(passes):
        term = rem.astype(BF16).astype(F32)
        parts.append(term)
        rem = rem - term
    while len(parts) % 2:
        parts.append(jnp.zeros_like(v))
    return jnp.concatenate(parts, axis=0).astype(BF16)


def _expander(n_rows, lanes_per_row, n_cols):
    r = lax.broadcasted_iota(jnp.int32, (n_rows, n_cols), 0) & (SUBLANES - 1)
    c = lax.broadcasted_iota(jnp.int32, (n_rows, n_cols), 1)
    lo = r * lanes_per_row
    return jnp.where((c >= lo) & (c < lo + lanes_per_row), 1.0, 0.0).astype(BF16)


def _expand(rows_bf16, expander):
    return lax.dot_general(rows_bf16, expander, (((0,), (0,)), ((), ())), preferred_element_type=F32)


def _ssd_kernel(xs_ref, b_ref, c_ref, z_ref, dt_ref, bias_ref, alog_ref, dskip_ref, nw_ref, o_ref,
                a_scr, dt_scr, qx_scr, ex_scr, acol_scr, st_scr, *, hpg):
    s = xs_ref.shape[1]
    gw = xs_ref.shape[2]
    nc = s // SSD_CHUNK
    L = SSD_CHUNK
    hd = SSD_HEAD_DIM

    raw = dt_ref[...] + bias_ref[0]
    dt = jnp.maximum(raw, 0.0) + jnp.log1p(jnp.exp(-jnp.abs(raw)))
    dta = dt * (-jnp.exp(alog_ref[0]))
    pos = lax.broadcasted_iota(jnp.int32, (SUBLANES, s), 1) & (L - 1)
    head_row = lax.broadcasted_iota(jnp.int32, (SUBLANES, s), 0)
    pre = dta
    suf = dta
    sh = 1
    while sh < L:
        pre = pre + jnp.where(pos >= sh, pltpu.roll(pre, sh, 1), 0.0)
        suf = suf + jnp.where(pos < L - sh, pltpu.roll(suf, s - sh, 1), 0.0)
        sh *= 2
    acum = jnp.where(head_row < hpg, pre, suf)
    tot = pre + suf - dta
    for c in range(nc):
        lanes = slice(c * L, (c + 1) * L)
        a_scr[c] = acum[:, lanes]
        dt_scr[c] = dt[:, lanes]

    exp_head = _expander(2 * SUBLANES, hd, 2 * gw)
    exp_col = _expander(4 * SUBLANES, L, 2 * hpg * L)
    qx_scr[...] = _expand(_split_rows(dt * jnp.exp(tot - acum), 2), exp_head)
    ex_scr[...] = _expand(_split_rows(jnp.exp(acum), 2), exp_head)
    acol_scr[...] = _expand(_split_rows(acum, 3), exp_col)

    def phase1(i, run_b):
        c = nc - 1 - i
        row0 = pl.multiple_of(c * L, L)
        rows = pl.ds(row0, L)
        x = xs_ref[0, rows, :].astype(F32)
        qx = qx_scr[rows, :]
        xw = jnp.concatenate([x * qx[:, 0:gw], x * qx[:, gw:2 * gw]], axis=1).astype(BF16)
        contrib = lax.dot_general(b_ref[0, rows, :], xw, (((0,), (0,)), ((), ())),
                                  preferred_element_type=F32)
        st_scr[c, :, 0:gw] = contrib[:, 0:gw]
        st_scr[c, :, gw:2 * gw] = run_b
        dec_b = ex_scr[pl.ds(row0, 1), gw:2 * gw]
        return run_b * dec_b + contrib[:, gw:2 * gw]

    lax.fori_loop(0, nc, phase1, jnp.zeros((SSD_STATE, gw), F32), unroll=4)

    li = lax.broadcasted_iota(jnp.int32, (L, L), 0)
    si = lax.broadcasted_iota(jnp.int32, (L, L), 1)
    causal = li >= si
    below = li > si
    above = si > li
    lane_l = lax.broadcasted_iota(jnp.int32, (L, LANES), 1)

    def phase2(c, run_f):
        rows = pl.ds(pl.multiple_of(c * L, L), L)
        xb = xs_ref[0, rows, :]
        cm = c_ref[0, rows, :]
        cb = lax.dot_general(cm, b_ref[0, rows, :], (((1,), (1,)), ((), ())), preferred_element_type=F32)
        at = a_scr[c]
        dtt = dt_scr[c]
        acol = acol_scr[rows, :]
        ys = []
        for p in range(hpg // 2):
            xpair = xb[:, p * LANES:(p + 1) * LANES]
            y_pair = None
            for q in range(2):
                kf = 2 * p + q
                kb = hpg + kf
                seg = jnp.where(causal, acol[:, kf * L:(kf + 1) * L] - at[kf:kf + 1, :],
                                acol[:, kb * L:(kb + 1) * L] - at[kb:kb + 1, :])
                dt_f = dtt[kf:kf + 1, :]
                dt_b = dtt[kb:kb + 1, :]
                w = jnp.where(below, dt_f, jnp.where(above, dt_b, dt_f + dt_b))
                m = (cb * jnp.exp(seg) * w).astype(BF16)
                in_head = (lane_l < hd) if q == 0 else (lane_l >= hd)
                part = jnp.dot(m, jnp.where(in_head, xpair, jnp.zeros_like(xpair)),
                               preferred_element_type=F32)
                y_pair = part if y_pair is None else y_pair + part
            ys.append(y_pair)
        y = ys[0] if len(ys) == 1 else jnp.concatenate(ys, axis=1)
        ex = ex_scr[rows, :]
        cs_f = jnp.dot(cm, run_f.astype(BF16), preferred_element_type=F32)
        cs_b = jnp.dot(cm, st_scr[c, :, gw:2 * gw].astype(BF16), preferred_element_type=F32)
        y = y + cs_f * ex[:, 0:gw] + cs_b * ex[:, gw:2 * gw]
        y = y + xb.astype(F32) * dskip_ref[...]
        z = z_ref[0, rows, :].astype(F32)
        g = y * (z * jax.nn.sigmoid(z))
        g = g * lax.rsqrt(jnp.mean(g * g, axis=-1, keepdims=True) + EPS)
        o_ref[0, rows, :] = (g * nw_ref[...]).astype(o_ref.dtype)
        dec_f = ex[L - 1:L, 0:gw]
        return run_f * dec_f + st_scr[c, :, 0:gw]

    lax.fori_loop(0, nc, phase2, jnp.zeros((SSD_STATE, gw), F32), unroll=4)


def _ssd(xbc_act, proj3, dt_rows, bias_col, alog_col, dskip, norm_w, ds, z_col0):
    b, s, _ = xbc_act.shape
    g = SSD_GROUPS
    gw = ds // g
    hpg = gw // SSD_HEAD_DIM
    nc = s // SSD_CHUNK
    n = SSD_STATE
    L = SSD_CHUNK
    assert z_col0 % gw == 0 and ds % n == 0 and hpg % 2 == 0 and 2 * hpg <= SUBLANES
    kernel = functools.partial(_ssd_kernel, hpg=hpg)
    return pl.pallas_call(
        kernel,
        grid=(b, g),
        in_specs=[
            pl.BlockSpec((1, s, gw), lambda i, j: (i, 0, j)),
            pl.BlockSpec((1, s, n), lambda i, j: (i, 0, ds // n + j)),
            pl.BlockSpec((1, s, n), lambda i, j: (i, 0, ds // n + g + j)),
            pl.BlockSpec((1, s, gw), lambda i, j: (i, 0, z_col0 // gw + j)),
            pl.BlockSpec((SUBLANES, s), lambda i, j: (j, i)),
            pl.BlockSpec((1, SUBLANES, 1), lambda i, j: (j, 0, 0)),
            pl.BlockSpec((1, SUBLANES, 1), lambda i, j: (j, 0, 0)),
            pl.BlockSpec((1, gw), lambda i, j: (0, j)),
            pl.BlockSpec((1, gw), lambda i, j: (0, j)),
        ],
        out_specs=pl.BlockSpec((1, s, gw), lambda i, j: (i, 0, j)),
        out_shape=jax.ShapeDtypeStruct((b, s, ds), BF16),
        scratch_shapes=[
            pltpu.VMEM((nc, SUBLANES, L), F32),
            pltpu.VMEM((nc, SUBLANES, L), F32),
            pltpu.VMEM((s, 2 * gw), F32),
            pltpu.VMEM((s, 2 * gw), F32),
            pltpu.VMEM((s, 2 * hpg * L), F32),
            pltpu.VMEM((nc, n, 2 * gw), F32),
        ],
        compiler_params=_cparams("parallel", "parallel"),
        name="ssd",
    )(xbc_act, xbc_act, xbc_act, proj3, dt_rows, bias_col, alog_col, dskip, norm_w)


def _split_bf16(a):
    hi = a.astype(BF16)
    lo = (a - hi.astype(F32)).astype(BF16)
    return hi, lo


def _outproj_kernel(ya_ref, yb_ref, x_ref, wa_ref, wb_ref, nw_ref, wr_ref, br_ref,
                    x1_ref, h_ref, route_ref, cnt_ref, carry_scr, *, n_experts):
    i = pl.program_id(0)
    tm = x_ref.shape[0]

    @pl.when(i == 0)
    def _():
        carry_scr[...] = jnp.zeros_like(carry_scr)

    x1 = x_ref[...] + jnp.dot(ya_ref[...], wa_ref[...], preferred_element_type=F32) \
        + jnp.dot(yb_ref[...], wb_ref[...], preferred_element_type=F32)
    x1_ref[...] = x1
    h = x1 * lax.rsqrt(jnp.mean(x1 * x1, axis=-1, keepdims=True) + EPS) * nw_ref[...]
    h_ref[...] = h

    h_hi, h_lo = _split_bf16(h)
    w_hi, w_lo = _split_bf16(wr_ref[...])
    logits = (jnp.dot(h_hi, w_hi, preferred_element_type=F32)
              + jnp.dot(h_hi, w_lo, preferred_element_type=F32)
              + jnp.dot(h_lo, w_hi, preferred_element_type=F32)) + br_ref[...]

    lane = lax.broadcasted_iota(jnp.int32, (tm, LANES), 1)
    neg = jnp.finfo(F32).min
    work = jnp.where(lane < n_experts, logits, neg)
    tops, idxs, sels = [], [], []
    for _k in range(TOP_K):
        m = jnp.max(work, axis=-1, keepdims=True)
        idx = jnp.min(jnp.where(work == m, lane, LANES), axis=-1, keepdims=True)
        sel = lane == idx
        work = jnp.where(sel, neg, work)
        tops.append(m)
        idxs.append(idx)
        sels.append(sel)
    exps = [jnp.exp(t - tops[0]) for t in tops]
    denom = exps[0]
    for e in exps[1:]:
        denom = denom + e
    inv = 1.0 / denom

    onehot = jnp.zeros((tm, LANES), F32)
    for sel in sels:
        onehot = onehot + jnp.where(sel, 1.0, 0.0)
    ri = lax.broadcasted_iota(jnp.int32, (tm, tm), 0)
    ci = lax.broadcasted_iota(jnp.int32, (tm, tm), 1)
    tri = jnp.where(ri > ci, 1.0, 0.0).astype(BF16)
    carry = carry_scr[0:1, :]
    prefix = jnp.dot(tri, onehot.astype(BF16), preferred_element_type=F32) + carry
    new_carry = carry + jnp.sum(onehot, axis=0, keepdims=True)
    carry_scr[...] = jnp.broadcast_to(new_carry, carry_scr.shape)
    cnt_ref[...] = jnp.broadcast_to(new_carry, cnt_ref.shape)

    route = jnp.zeros((tm, LANES), F32)
    for k in range(TOP_K):
        rank = jnp.sum(jnp.where(sels[k], prefix, 0.0), axis=-1, keepdims=True)
        route = jnp.where(lane == k, idxs[k].astype(F32), route)
        route = jnp.where(lane == TOP_K + k, rank, route)
        route = jnp.where(lane == 2 * TOP_K + k, exps[k] * inv, route)
    route_ref[...] = route


def _outproj_route(ya, yb, x2, wa, wb, norm_w, w_router, b_router, n_experts):
    t, d = x2.shape
    dc = ya.shape[1]
    ds = yb.shape[1]
    tm = _largest_divisor(t, (512, 256, 128))
    kernel = functools.partial(_outproj_kernel, n_experts=n_experts)
    const = lambda shape: pl.BlockSpec(shape, lambda i: (0, 0))
    return pl.pallas_call(
        kernel,
        grid=(t // tm,),
        in_specs=[
            pl.BlockSpec((tm, dc), lambda i: (i, 0)),
            pl.BlockSpec((tm, ds), lambda i: (i, 0)),
            pl.BlockSpec((tm, d), lambda i: (i, 0)),
            const((dc, d)), const((ds, d)), const((1, d)), const((d, LANES)), const((1, LANES)),
        ],
        out_specs=[
            pl.BlockSpec((tm, d), lambda i: (i, 0)),
            pl.BlockSpec((tm, d), lambda i: (i, 0)),
            pl.BlockSpec((tm, LANES), lambda i: (i, 0)),
            pl.BlockSpec((8, LANES), lambda i: (0, 0)),
        ],
        out_shape=[
            jax.ShapeDtypeStruct((t, d), F32),
            jax.ShapeDtypeStruct((t, d), F32),
            jax.ShapeDtypeStruct((t, LANES), F32),
            jax.ShapeDtypeStruct((8, LANES), F32),
        ],
        scratch_shapes=[pltpu.VMEM((8, LANES), F32)],
        compiler_params=_cparams("arbitrary"),
        name="outproj_route",
    )(ya, yb, x2, wa, wb, norm_w, w_router, b_router)


ZERO_ROWS = 256


def _dispatch_kernel(zflag_ref, dest_ref, h_ref, o_ref, zbuf, sem, zsem, *, tm):
    tt = h_ref.shape[0]
    n_tiles = o_ref.shape[0] // tm
    per_tile = tm // ZERO_ROWS

    def zero_copy(i, p):
        return pltpu.make_async_copy(zbuf, o_ref.at[pl.ds(i * tm + p * ZERO_ROWS, ZERO_ROWS)], zsem)

    @pl.when(pl.program_id(0) == 0)
    def _():
        zbuf[...] = jnp.zeros_like(zbuf)

        def start(i, carry):
            @pl.when(zflag_ref[i] != 0)
            def _():
                for p in range(per_tile):
                    zero_copy(i, p).start()
            return carry

        def wait(i, carry):
            @pl.when(zflag_ref[i] != 0)
            def _():
                for p in range(per_tile):
                    zero_copy(i, p).wait()
            return carry

        lax.fori_loop(0, n_tiles, start, 0)
        lax.fori_loop(0, n_tiles, wait, 0)

    def issue(t, carry):
        for k in range(TOP_K):
            pltpu.make_async_copy(h_ref.at[pl.ds(t, 1)], o_ref.at[pl.ds(dest_ref[t * TOP_K + k], 1)],
                                  sem).start(priority=k % 2)
        return carry

    lax.fori_loop(0, tt, issue, 0)
    for _k in range(TOP_K):
        pltpu.make_async_copy(h_ref, o_ref.at[pl.ds(0, tt)], sem).wait()


def _dispatch(zflag, dest_flat, h, n_rows, tm):
    t, d = h.shape
    tt = _largest_divisor(t, (512, 256))
    assert tm % ZERO_ROWS == 0
    grid_spec = pltpu.PrefetchScalarGridSpec(
        num_scalar_prefetch=1,
        grid=(t // tt,),
        in_specs=[
            pl.BlockSpec((tt * TOP_K,), lambda i, zf: (i,), memory_space=pltpu.SMEM),
            pl.BlockSpec((tt, d), lambda i, zf: (i, 0)),
        ],
        out_specs=pl.BlockSpec(memory_space=pl.ANY),
        scratch_shapes=[pltpu.VMEM((ZERO_ROWS, d), h.dtype), pltpu.SemaphoreType.DMA(()),
                        pltpu.SemaphoreType.DMA(())],
    )
    return pl.pallas_call(
        functools.partial(_dispatch_kernel, tm=tm),
        grid_spec=grid_spec,
        out_shape=jax.ShapeDtypeStruct((n_rows, d), h.dtype),
        compiler_params=_cparams("arbitrary"),
        name="dispatch",
    )(zflag, dest_flat, h)


def _load_cast(src, dst, stage, sem):
    n_slots, rows, width = stage.shape
    per = width // src.shape[1]
    w = src.shape[1]
    n = dst.shape[0] // (rows * per)

    def copies(k, slot):
        return [pltpu.make_async_copy(src.at[pl.ds((k * per + p) * rows, rows)],
                                      stage.at[slot, :, p * w:(p + 1) * w], sem.at[slot])
                for p in range(per)]

    for slot in range(n_slots):
        for c in copies(slot, slot):
            c.start()

    def body(kk, carry):
        for slot in range(n_slots):
            k = n_slots * kk + slot
            for c in copies(k, slot):
                c.wait()
            for p in range(per):
                row0 = pl.multiple_of((k * per + p) * rows, rows)
                dst[pl.ds(row0, rows), :] = stage[slot, :, p * w:(p + 1) * w].astype(BF16)

            @pl.when(k + n_slots < n)
            def _():
                for c in copies(k + n_slots, slot):
                    c.start()
        return carry

    lax.fori_loop(0, n // n_slots, body, 0)


def _expert_kernel(te_ref, nu_ref, nb_ref, x_ref, wgu_hbm, wd_hbm, bgu_ref, bd_ref, o_ref,
                   wgu_buf, wd_buf, xb_scr, act_scr, stage, sem, *, tf):
    i = pl.program_id(0)
    tm = x_ref.shape[0]
    f = wd_buf.shape[0]
    e = te_ref[i]
    active = i < nu_ref[0]

    @pl.when(active & ((i == 0) | (e != te_ref[jnp.maximum(i - 1, 0)])))
    def _():
        _load_cast(wgu_hbm.at[e], wgu_buf, stage, sem)
        _load_cast(wd_hbm.at[e], wd_buf, stage, sem)

    def mlp(rows):
        xb_scr[0:rows, :] = x_ref[0:rows, :].astype(BF16)
        xb = xb_scr[0:rows, :]
        for j in range(f // tf):
            gcols = slice(j * tf, (j + 1) * tf)
            lcols = slice(f + j * tf, f + (j + 1) * tf)
            glu = jnp.dot(xb, wgu_buf[:, gcols], preferred_element_type=F32) + bgu_ref[:, gcols]
            lin = jnp.dot(xb, wgu_buf[:, lcols], preferred_element_type=F32) + bgu_ref[:, lcols]
            glu = jnp.minimum(glu, SWIGLU_LIMIT)
            lin = jnp.clip(lin, -SWIGLU_LIMIT, SWIGLU_LIMIT)
            act_scr[0:rows, gcols] = (glu * jax.nn.sigmoid(SWIGLU_ALPHA * glu) * (lin + 1.0)).astype(BF16)
        o_ref[0:rows, :] = (jnp.dot(act_scr[0:rows, :], wd_buf[...], preferred_element_type=F32)
                            + bd_ref[...]).astype(o_ref.dtype)
        if rows < tm:
            o_ref[rows:tm, :] = jnp.zeros((tm - rows, o_ref.shape[1]), o_ref.dtype)

    for blocks in range(1, tm // MOE_ROW_BLOCK + 1):
        pl.when(active & (nb_ref[i] == blocks))(functools.partial(mlp, blocks * MOE_ROW_BLOCK))

    @pl.when(jnp.logical_not(active))
    def _():
        o_ref[...] = jnp.zeros_like(o_ref)


WEIGHT_STAGE_ROWS = 64
WEIGHT_STAGE_SLOTS = 8


MOE_ROW_BLOCK = 128


def _experts(tile_e, n_used, tile_blocks, x_sorted, w_gu, b_gu, w_d, b_d, tm):
    n_rows, d = x_sorted.shape
    f = w_d.shape[1]
    chunk = WEIGHT_STAGE_ROWS * WEIGHT_STAGE_SLOTS
    assert (2 * f) % d == 0 and d % chunk == 0 and f % (chunk * (2 * f // d)) == 0
    assert tm % MOE_ROW_BLOCK == 0
    tf = _largest_divisor(f, (1024, 512, 256, 128))
    n_tiles = n_rows // tm
    grid_spec = pltpu.PrefetchScalarGridSpec(
        num_scalar_prefetch=3,
        grid=(n_tiles,),
        in_specs=[
            pl.BlockSpec((tm, d), lambda i, te, nu, nb: (jnp.minimum(i, nu[0] - 1), 0)),
            pl.BlockSpec(memory_space=pl.ANY),
            pl.BlockSpec(memory_space=pl.ANY),
            pl.BlockSpec((None, 1, 2 * f), lambda i, te, nu, nb: (te[i], 0, 0)),
            pl.BlockSpec((None, 1, d), lambda i, te, nu, nb: (te[i], 0, 0)),
        ],
        out_specs=pl.BlockSpec((tm, d), lambda i, te, nu, nb: (i, 0)),
        scratch_shapes=[
            pltpu.VMEM((d, 2 * f), BF16),
            pltpu.VMEM((f, d), BF16),
            pltpu.VMEM((tm, d), BF16),
            pltpu.VMEM((tm, f), BF16),
            pltpu.VMEM((WEIGHT_STAGE_SLOTS, WEIGHT_STAGE_ROWS, 2 * f), F32),
            pltpu.SemaphoreType.DMA((WEIGHT_STAGE_SLOTS,)),
        ],
    )
    return pl.pallas_call(
        functools.partial(_expert_kernel, tf=tf),
        grid_spec=grid_spec,
        out_shape=jax.ShapeDtypeStruct((n_rows, d), F32),
        compiler_params=_cparams("arbitrary"),
        name="experts",
    )(tile_e, n_used, tile_blocks, x_sorted, w_gu, w_d, b_gu, b_d)


def _combine_kernel(dest_ref, dest_next_ref, y_ref, x1_ref, route_ref, nw_ref, o_ref, buf, sem,
                    *, final_norm):
    i = pl.program_id(0)
    n = pl.num_programs(0)
    tt = x1_ref.shape[0]
    slot = i % 2

    def gather(idx_ref, dst_slot):
        def issue(t, carry):
            for k in range(TOP_K):
                pltpu.make_async_copy(y_ref.at[pl.ds(idx_ref[t * TOP_K + k], 1)],
                                      buf.at[dst_slot, pl.ds(k * tt + t, 1)],
                                      sem.at[dst_slot]).start(priority=k % 2)
            return carry

        lax.fori_loop(0, tt, issue, 0)

    @pl.when(i == 0)
    def _():
        gather(dest_ref, slot)

    @pl.when(i + 1 < n)
    def _():
        gather(dest_next_ref, 1 - slot)

    pltpu.make_async_copy(y_ref.at[pl.ds(0, TOP_K * tt)], buf.at[slot], sem.at[slot]).wait()

    route = route_ref[...]
    x = x1_ref[...]
    for k in range(TOP_K):
        gate = route[:, 2 * TOP_K + k:2 * TOP_K + k + 1]
        x = x + gate * buf[slot, k * tt:(k + 1) * tt, :]
    if final_norm:
        x = x * lax.rsqrt(jnp.mean(x * x, axis=-1, keepdims=True) + EPS) * nw_ref[...]
    o_ref[...] = x


def _combine(dest_flat, y_sorted, x1, route, norm_w, final_norm):
    t, d = x1.shape
    tt = _largest_divisor(t, (512, 256))
    n = t // tt
    return pl.pallas_call(
        functools.partial(_combine_kernel, final_norm=final_norm),
        grid=(n,),
        in_specs=[
            pl.BlockSpec((tt * TOP_K,), lambda i: (i,), memory_space=pltpu.SMEM),
            pl.BlockSpec((tt * TOP_K,), lambda i: (jnp.minimum(i + 1, n - 1),), memory_space=pltpu.SMEM),
            pl.BlockSpec(memory_space=pl.ANY),
            pl.BlockSpec((tt, d), lambda i: (i, 0)),
            pl.BlockSpec((tt, LANES), lambda i: (i, 0)),
            pl.BlockSpec((1, d), lambda i: (0, 0)),
        ],
        out_specs=pl.BlockSpec((tt, d), lambda i: (i, 0)),
        out_shape=jax.ShapeDtypeStruct((t, d), F32),
        scratch_shapes=[pltpu.VMEM((2, TOP_K * tt, d), F32), pltpu.SemaphoreType.DMA((2,))],
        compiler_params=_cparams("arbitrary"),
        name="combine",
    )(dest_flat, dest_flat, y_sorted, x1, route, norm_w)


MOE_ROW_TILE = 512


def _layer(x2, bsz, seq, norm_mix_w, w_in, conv_a_w, ssd_conv_w, ssd_conv_b, dt_bias_fw, dt_bias_bw,
           a_log_fw, a_log_bw, d_skip, ssd_norm_w, w_out, norm_ffn_w, w_router, b_router,
           w_gate_up, b_gate_up, w_down, b_down):
    t, d = x2.shape
    dc = conv_a_w.shape[1]
    ds = ssd_norm_w.shape[0]
    dxbc = ssd_conv_w.shape[1]
    heads = dt_bias_fw.shape[0]
    g = SSD_GROUPS
    hpg = heads // g
    n_main = 3 * dc + ds + dxbc
    n_experts = w_router.shape[1]

    w_main = w_in[:, :n_main].astype(BF16)
    w_dt = jnp.transpose(w_in[:, n_main:].reshape(d, 2, g, hpg), (0, 2, 1, 3)).reshape(d, g, 2 * hpg)
    w_dt = jnp.pad(w_dt, ((0, 0), (0, 0), (0, SUBLANES - 2 * hpg))).reshape(d, g * SUBLANES)
    w_dt = jnp.pad(w_dt, ((0, 0), (0, LANES - g * SUBLANES))).astype(BF16)
    proj, dt_rows = _inproj(x2, norm_mix_w.reshape(1, d), w_main, w_dt)
    proj3 = proj.reshape(bsz, seq, n_main)
    y_a = _conv_a(proj3, conv_a_w, dc)
    xbc_act = _conv_ssd(proj3, ssd_conv_w, ssd_conv_b.reshape(1, dxbc), 3 * dc + ds, dxbc)

    def head_rows(fw, bw):
        both = jnp.concatenate([fw.reshape(g, hpg), bw.reshape(g, hpg)], axis=1)
        return jnp.pad(both, ((0, 0), (0, SUBLANES - 2 * hpg))).reshape(g, SUBLANES, 1)

    dskip = jnp.repeat(d_skip, SSD_HEAD_DIM).reshape(1, ds)
    y_b = _ssd(xbc_act, proj3, dt_rows, head_rows(dt_bias_fw, dt_bias_bw), head_rows(a_log_fw, a_log_bw),
               dskip, ssd_norm_w.reshape(1, ds), ds, 3 * dc)

    w_out_b = w_out.astype(BF16)
    w_r = jnp.pad(w_router, ((0, 0), (0, LANES - n_experts)))
    b_r = jnp.pad(b_router, (0, LANES - n_experts)).reshape(1, LANES)
    x1, h2, route, counts = _outproj_route(
        y_a.reshape(t, dc), y_b.reshape(t, ds), x2, w_out_b[:dc], w_out_b[dc:],
        norm_ffn_w.reshape(1, d), w_r, b_r, n_experts)

    tm = MOE_ROW_TILE
    n_slots = t * TOP_K
    n_tiles = n_slots // tm + n_experts
    cnt = counts[0, :n_experts].astype(jnp.int32)
    padded = (cnt + tm - 1) // tm * tm
    pends = jnp.cumsum(padded)
    pstarts = pends - padded
    e_idx = route[:, :TOP_K].astype(jnp.int32)
    rank = route[:, TOP_K:2 * TOP_K].astype(jnp.int32)
    experts = jnp.arange(n_experts, dtype=jnp.int32)
    dest = (jnp.sum((e_idx[..., None] == experts) * pstarts, axis=-1) + rank).reshape(n_slots)
    n_used = (pends[-1] // tm).astype(jnp.int32)
    tile_ids = jnp.arange(n_tiles, dtype=jnp.int32)
    tile_row0 = jnp.minimum(tile_ids, n_used - 1) * tm
    tile_e = jnp.minimum(jnp.sum((pends[None, :] <= tile_row0[:, None]).astype(jnp.int32), axis=1),
                         n_experts - 1)
    seg_end = jnp.sum((tile_e[:, None] == experts) * (pstarts + cnt), axis=-1)
    tile_valid = jnp.clip(seg_end - tile_row0, 1, tm)
    tile_blocks = ((tile_valid + MOE_ROW_BLOCK - 1) // MOE_ROW_BLOCK).astype(jnp.int32)
    partial_last = (pends - tm) * ((cnt % tm) != 0) - (cnt % tm == 0)
    zflag = (jnp.any(partial_last[None, :] == (tile_ids * tm)[:, None], axis=1)
             | (tile_ids >= n_used)).astype(jnp.int32)

    x_sorted = _dispatch(zflag, dest, h2, n_tiles * tm, tm)
    f = w_down.shape[1]
    y_sorted = _experts(tile_e, n_used.reshape(1), tile_blocks, x_sorted, w_gate_up,
                        b_gate_up.reshape(n_experts, 1, 2 * f), w_down,
                        b_down.reshape(n_experts, 1, d), tm)
    return dest, y_sorted, x1, route


def kernel(x, norm_mix_w, w_in, conv_a_w, ssd_conv_w, ssd_conv_b, dt_bias_fw, dt_bias_bw, a_log_fw,
           a_log_bw, d_skip, ssd_norm_w, w_out, norm_ffn_w, w_router, b_router, w_gate_up, b_gate_up,
           w_down, b_down, norm_final_w):
    bsz, seq, d = x.shape
    depth = w_in.shape[0]
    x2 = x.reshape(bsz * seq, d)
    for layer in range(depth):
        dest, y_sorted, x1, route = _layer(
            x2, bsz, seq, norm_mix_w[layer], w_in[layer], conv_a_w[layer], ssd_conv_w[layer],
            ssd_conv_b[layer], dt_bias_fw[layer], dt_bias_bw[layer], a_log_fw[layer], a_log_bw[layer],
            d_skip[layer], ssd_norm_w[layer], w_out[layer], norm_ffn_w[layer], w_router[layer],
            b_router[layer], w_gate_up[layer], b_gate_up[layer], w_down[layer], b_down[layer])
        x2 = _combine(dest, y_sorted, x1, route, norm_final_w.reshape(1, d), layer == depth - 1)
    return x2.reshape(bsz, seq, d)
```

```python
import functools

import jax
import jax.numpy as jnp
from jax import lax
from jax.experimental import pallas as pl
from jax.experimental.pallas import tpu as pltpu

F32 = jnp.float32
BF16 = jnp.bfloat16

EPS = 1e-5
SSD_HEAD_DIM = 64
SSD_GROUPS = 4
SSD_STATE = 128
SSD_CHUNK = 128
TOP_K = 4
SWIGLU_LIMIT = 7.0
SWIGLU_ALPHA = 1.702

LANES = 128
SUBLANES = 8
VMEM_LIMIT_BYTES = 56 * 1024 * 1024


def _largest_divisor(n, candidates):
    for c in candidates:
        if n % c == 0:
            return c
    raise ValueError(f"no tile in {candidates} divides {n}")


def _cparams(*sem):
    return pltpu.CompilerParams(dimension_semantics=tuple(sem), vmem_limit_bytes=VMEM_LIMIT_BYTES)


def _inproj_kernel(x_ref, nw_ref, w_ref, wdt_ref, o_ref, dt_ref, h_scr):
    @pl.when(pl.program_id(1) == 0)
    def _():
        x = x_ref[...]
        ms = jnp.mean(x * x, axis=-1, keepdims=True)
        h = (x * lax.rsqrt(ms + EPS) * nw_ref[...]).astype(BF16)
        h_scr[...] = h
        dt_ref[...] = jnp.dot(h, wdt_ref[...], preferred_element_type=F32).T

    o_ref[...] = jnp.dot(h_scr[...], w_ref[...], preferred_element_type=F32).astype(o_ref.dtype)


def _inproj(x2, norm_w, w_all, n, w_dt):
    t, d = x2.shape
    tm = _largest_divisor(t, (1024, 512, 256, 128))
    tn = _largest_divisor(n, (2048, 1536, 1024, 512, 256, 128))
    return pl.pallas_call(
        _inproj_kernel,
        grid=(t // tm, n // tn),
        in_specs=[
            pl.BlockSpec((tm, d), lambda i, j: (i, 0)),
            pl.BlockSpec((1, d), lambda i, j: (0, 0)),
            pl.BlockSpec((d, tn), lambda i, j: (0, j)),
            pl.BlockSpec((d, LANES), lambda i, j: (0, 0)),
        ],
        out_specs=[
            pl.BlockSpec((tm, tn), lambda i, j: (i, j)),
            pl.BlockSpec((LANES, tm), lambda i, j: (0, i)),
        ],
        out_shape=[
            jax.ShapeDtypeStruct((t, n), BF16),
            jax.ShapeDtypeStruct((LANES, t), F32),
        ],
        scratch_shapes=[pltpu.VMEM((tm, d), BF16)],
        compiler_params=_cparams("parallel", "arbitrary"),
        name="inproj",
    )(x2, norm_w, w_all, w_dt)


def _centred_conv(v, w_ref):
    s = v.shape[0]
    width = w_ref.shape[0]
    half = width // 2
    edge = 2 * SUBLANES
    assert half <= SUBLANES and s >= 2 * edge

    def taps(u, mask_rows):
        n = u.shape[0]
        row = lax.broadcasted_iota(jnp.int32, u.shape, 0)
        acc = u * w_ref[half:half + 1, :]
        for k in range(width):
            off = k - half
            if off == 0:
                continue
            shifted = pltpu.roll(u, (-off) % n, 0)
            if mask_rows:
                shifted = jnp.where((row + off >= 0) & (row + off < n), shifted, 0.0)
            acc = acc + shifted * w_ref[k:k + 1, :]
        return acc

    body = taps(v, False)
    head = taps(v[0:edge, :], True)[0:SUBLANES, :]
    tail = taps(v[s - edge:s, :], True)[SUBLANES:edge, :]
    return jnp.concatenate([head, body[SUBLANES:s - SUBLANES, :], tail], axis=0)


def _conv_a_kernel(gb_ref, gc_ref, u_ref, w_ref, o_ref):
    v = gc_ref[0].astype(F32) * u_ref[0].astype(F32)
    o_ref[0] = (gb_ref[0].astype(F32) * _centred_conv(v, w_ref)).astype(o_ref.dtype)


def _conv_a(proj3, conv_w, dc):
    b, s, _ = proj3.shape
    tc = _largest_divisor(dc, (512, 256, 128))
    nb = dc // tc
    blk = lambda off: pl.BlockSpec((1, s, tc), lambda i, j: (i, 0, off + j))
    return pl.pallas_call(
        _conv_a_kernel,
        grid=(b, nb),
        in_specs=[blk(0), blk(nb), blk(2 * nb),
                  pl.BlockSpec((conv_w.shape[0], tc), lambda i, j: (0, j))],
        out_specs=pl.BlockSpec((1, s, tc), lambda i, j: (i, 0, j)),
        out_shape=jax.ShapeDtypeStruct((b, s, dc), BF16),
        compiler_params=_cparams("parallel", "parallel"),
        name="conv_a",
    )(proj3, proj3, proj3, conv_w)


CONV_CHUNK = 128
CONV_HALO = 16


def _conv_ssd_kernel(x_ref, w_ref, b_ref, o_ref):
    s = x_ref.shape[1]
    width = w_ref.shape[0]
    half = width // 2
    win_rows = CONV_CHUNK + 2 * CONV_HALO
    assert half <= CONV_HALO and s % CONV_CHUNK == 0 and s >= win_rows

    r = lax.broadcasted_iota(jnp.int32, (CONV_CHUNK, win_rows), 0)
    j = lax.broadcasted_iota(jnp.int32, (CONV_CHUNK, win_rows), 1)
    shift = {}

    def shift_matrix(lead, off):
        if (lead, off) not in shift:
            shift[(lead, off)] = jnp.where(j == r + (lead + off), 1.0, 0.0).astype(x_ref.dtype)
        return shift[(lead, off)]

    for c in range(s // CONV_CHUNK):
        r0 = c * CONV_CHUNK
        w0 = min(max(r0 - CONV_HALO, 0), s - win_rows)
        win = x_ref[0, w0:w0 + win_rows, :]
        acc = x_ref[0, r0:r0 + CONV_CHUNK, :].astype(F32) * w_ref[half:half + 1, :]
        for k in range(width):
            off = k - half
            if off != 0:
                acc = acc + jnp.dot(shift_matrix(r0 - w0, off), win,
                                    preferred_element_type=F32) * w_ref[k:k + 1, :]
        y = acc + b_ref[...]
        o_ref[0, r0:r0 + CONV_CHUNK, :] = (y * jax.nn.sigmoid(y)).astype(o_ref.dtype)


def _conv_ssd(proj3, conv_w, conv_b, col0, dxbc):
    b, s, _ = proj3.shape
    tc = _largest_divisor(dxbc, (512, 256, 128))
    assert col0 % tc == 0
    off = col0 // tc
    return pl.pallas_call(
        _conv_ssd_kernel,
        grid=(b, dxbc // tc),
        in_specs=[pl.BlockSpec((1, s, tc), lambda i, j: (i, 0, off + j)),
                  pl.BlockSpec((conv_w.shape[0], tc), lambda i, j: (0, j)),
                  pl.BlockSpec((1, tc), lambda i, j: (0, j))],
        out_specs=pl.BlockSpec((1, s, tc), lambda i, j: (i, 0, j)),
        out_shape=jax.ShapeDtypeStruct((b, s, dxbc), BF16),
        compiler_params=_cparams("parallel", "parallel"),
        name="conv_ssd",
    )(proj3, conv_w, conv_b)


def _split_rows(v, passes):
    parts = []
    rem = v
    for _ in range(passes):
        term = rem.astype(BF16).astype(F32)
        parts.append(term)
        rem = rem - term
    while len(parts) % 2:
        parts.append(jnp.zeros_like(v))
    return jnp.concatenate(parts, axis=0).astype(BF16)


def _expander(n_rows, lanes_per_row, n_cols):
    r = lax.broadcasted_iota(jnp.int32, (n_rows, n_cols), 0) & (SUBLANES - 1)
    c = lax.broadcasted_iota(jnp.int32, (n_rows, n_cols), 1)
    lo = r * lanes_per_row
    return jnp.where((c >= lo) & (c < lo + lanes_per_row), 1.0, 0.0).astype(BF16)


def _expand(rows_bf16, expander):
    return lax.dot_general(rows_bf16, expander, (((0,), (0,)), ((), ())), preferred_element_type=F32)


def _ssd_kernel(xs_ref, b_ref, c_ref, z_ref, dt_ref, bias_ref, alog_ref, dskip_ref, nw_ref, o_ref,
                a_scr, dt_scr, qx_scr, ex_scr, acol_scr, st_scr, *, hpg):
    s = xs_ref.shape[1]
    gw = xs_ref.shape[2]
    nc = s // SSD_CHUNK
    L = SSD_CHUNK
    hd = SSD_HEAD_DIM

    raw = dt_ref[...] + bias_ref[0]
    dt = jnp.maximum(raw, 0.0) + jnp.log1p(jnp.exp(-jnp.abs(raw)))
    dta = dt * (-jnp.exp(alog_ref[0]))
    pos = lax.broadcasted_iota(jnp.int32, (SUBLANES, s), 1) & (L - 1)
    head_row = lax.broadcasted_iota(jnp.int32, (SUBLANES, s), 0)
    pre = dta
    suf = dta
    sh = 1
    while sh < L:
        pre = pre + jnp.where(pos >= sh, pltpu.roll(pre, sh, 1), 0.0)
        suf = suf + jnp.where(pos < L - sh, pltpu.roll(suf, s - sh, 1), 0.0)
        sh *= 2
    acum = jnp.where(head_row < hpg, pre, suf)
    tot = pre + suf - dta
    for c in range(nc):
        lanes = slice(c * L, (c + 1) * L)
        a_scr[c] = acum[:, lanes]
        dt_scr[c] = dt[:, lanes]

    exp_head = _expander(2 * SUBLANES, hd, 2 * gw)
    exp_col = _expander(4 * SUBLANES, L, 2 * hpg * L)
    qx_scr[...] = _expand(_split_rows(dt * jnp.exp(tot - acum), 2), exp_head)
    ex_scr[...] = _expand(_split_rows(jnp.exp(acum), 2), exp_head)
    acol_scr[...] = _expand(_split_rows(acum, 3), exp_col)

    def phase1(i, run_b):
        c = nc - 1 - i
        row0 = pl.multiple_of(c * L, L)
        rows = pl.ds(row0, L)
        x = xs_ref[0, rows, :].astype(F32)
        qx = qx_scr[rows, :]
        xw = jnp.concatenate([x * qx[:, 0:gw], x * qx[:, gw:2 * gw]], axis=1).astype(BF16)
        contrib = lax.dot_general(b_ref[0, rows, :], xw, (((0,), (0,)), ((), ())),
                                  preferred_element_type=F32)
        st_scr[c, :, 0:gw] = contrib[:, 0:gw]
        st_scr[c, :, gw:2 * gw] = run_b
        dec_b = ex_scr[pl.ds(row0, 1), gw:2 * gw]
        return run_b * dec_b + contrib[:, gw:2 * gw]

    lax.fori_loop(0, nc, phase1, jnp.zeros((SSD_STATE, gw), F32), unroll=4)

    li = lax.broadcasted_iota(jnp.int32, (L, L), 0)
    si = lax.broadcasted_iota(jnp.int32, (L, L), 1)
    causal = li >= si
    below = li > si
    above = si > li
    lane_l = lax.broadcasted_iota(jnp.int32, (L, LANES), 1)

    def phase2(c, run_f):
        rows = pl.ds(pl.multiple_of(c * L, L), L)
        xb = xs_ref[0, rows, :]
        cm = c_ref[0, rows, :]
        cb = lax.dot_general(cm, b_ref[0, rows, :], (((1,), (1,)), ((), ())), preferred_element_type=F32)
        at = a_scr[c]
        dtt = dt_scr[c]
        acol = acol_scr[rows, :]
        ys = []
        for p in range(hpg // 2):
            xpair = xb[:, p * LANES:(p + 1) * LANES]
            y_pair = None
            for q in range(2):
                kf = 2 * p + q
                kb = hpg + kf
                seg = jnp.where(causal, acol[:, kf * L:(kf + 1) * L] - at[kf:kf + 1, :],
                                acol[:, kb * L:(kb + 1) * L] - at[kb:kb + 1, :])
                dt_f = dtt[kf:kf + 1, :]
                dt_b = dtt[kb:kb + 1, :]
                w = jnp.where(below, dt_f, jnp.where(above, dt_b, dt_f + dt_b))
                m = (cb * jnp.exp(seg) * w).astype(BF16)
                in_head = (lane_l < hd) if q == 0 else (lane_l >= hd)
                part = jnp.dot(m, jnp.where(in_head, xpair, jnp.zeros_like(xpair)),
                               preferred_element_type=F32)
                y_pair = part if y_pair is None else y_pair + part
            ys.append(y_pair)
        y = ys[0] if len(ys) == 1 else jnp.concatenate(ys, axis=1)
        ex = ex_scr[rows, :]
        cs_f = jnp.dot(cm, run_f.astype(BF16), preferred_element_type=F32)
        cs_b = jnp.dot(cm, st_scr[c, :, gw:2 * gw].astype(BF16), preferred_element_type=F32)
        y = y + cs_f * ex[:, 0:gw] + cs_b * ex[:, gw:2 * gw]
        y = y + xb.astype(F32) * dskip_ref[...]
        z = z_ref[0, rows, :].astype(F32)
        g = y * (z * jax.nn.sigmoid(z))
        g = g * lax.rsqrt(jnp.mean(g * g, axis=-1, keepdims=True) + EPS)
        o_ref[0, rows, :] = (g * nw_ref[...]).astype(o_ref.dtype)
        dec_f = ex[L - 1:L, 0:gw]
        return run_f * dec_f + st_scr[c, :, 0:gw]

    lax.fori_loop(0, nc, phase2, jnp.zeros((SSD_STATE, gw), F32), unroll=4)


def _ssd(xbc_act, proj3, dt_rows, bias_col, alog_col, dskip, norm_w, ds, z_col0):
    b, s, _ = xbc_act.shape
    g = SSD_GROUPS
    gw = ds // g
    hpg = gw // SSD_HEAD_DIM
    nc = s // SSD_CHUNK
    n = SSD_STATE
    L = SSD_CHUNK
    assert z_col0 % gw == 0 and ds % n == 0 and hpg % 2 == 0 and 2 * hpg <= SUBLANES
    kernel = functools.partial(_ssd_kernel, hpg=hpg)
    return pl.pallas_call(
        kernel,
        grid=(b, g),
        in_specs=[
            pl.BlockSpec((1, s, gw), lambda i, j: (i, 0, j)),
            pl.BlockSpec((1, s, n), lambda i, j: (i, 0, ds // n + j)),
            pl.BlockSpec((1, s, n), lambda i, j: (i, 0, ds // n + g + j)),
            pl.BlockSpec((1, s, gw), lambda i, j: (i, 0, z_col0 // gw + j)),
            pl.BlockSpec((SUBLANES, s), lambda i, j: (j, i)),
            pl.BlockSpec((1, SUBLANES, 1), lambda i, j: (j, 0, 0)),
            pl.BlockSpec((1, SUBLANES, 1), lambda i, j: (j, 0, 0)),
            pl.BlockSpec((1, gw), lambda i, j: (0, j)),
            pl.BlockSpec((1, gw), lambda i, j: (0, j)),
        ],
        out_specs=pl.BlockSpec((1, s, gw), lambda i, j: (i, 0, j)),
        out_shape=jax.ShapeDtypeStruct((b, s, ds), BF16),
        scratch_shapes=[
            pltpu.VMEM((nc, SUBLANES, L), F32),
            pltpu.VMEM((nc, SUBLANES, L), F32),
            pltpu.VMEM((s, 2 * gw), F32),
            pltpu.VMEM((s, 2 * gw), F32),
            pltpu.VMEM((s, 2 * hpg * L), F32),
            pltpu.VMEM((nc, n, 2 * gw), F32),
        ],
        compiler_params=_cparams("parallel", "parallel"),
        name="ssd",
    )(xbc_act, xbc_act, xbc_act, proj3, dt_rows, bias_col, alog_col, dskip, norm_w)


def _split_bf16(a):
    hi = a.astype(BF16)
    lo = (a - hi.astype(F32)).astype(BF16)
    return hi, lo


def _outproj_kernel(ya_ref, yb_ref, x_ref, wa_ref, wb_ref, nw_ref, wr_ref, br_ref,
                    x1_ref, h_ref, route_ref, cnt_ref, carry_scr, *, n_experts):
    i = pl.program_id(0)
    tm = x_ref.shape[0]

    @pl.when(i == 0)
    def _():
        carry_scr[...] = jnp.zeros_like(carry_scr)

    x1 = x_ref[...] + jnp.dot(ya_ref[...], wa_ref[...], preferred_element_type=F32) \
        + jnp.dot(yb_ref[...], wb_ref[...], preferred_element_type=F32)
    x1_ref[...] = x1
    h = x1 * lax.rsqrt(jnp.mean(x1 * x1, axis=-1, keepdims=True) + EPS) * nw_ref[...]
    h_ref[...] = h

    h_hi, h_lo = _split_bf16(h)
    w_hi, w_lo = _split_bf16(wr_ref[...])
    logits = (jnp.dot(h_hi, w_hi, preferred_element_type=F32)
              + jnp.dot(h_hi, w_lo, preferred_element_type=F32)
              + jnp.dot(h_lo, w_hi, preferred_element_type=F32)) + br_ref[...]

    lane = lax.broadcasted_iota(jnp.int32, (tm, LANES), 1)
    neg = jnp.finfo(F32).min
    work = jnp.where(lane < n_experts, logits, neg)
    tops, idxs, sels = [], [], []
    for _k in range(TOP_K):
        m = jnp.max(work, axis=-1, keepdims=True)
        idx = jnp.min(jnp.where(work == m, lane, LANES), axis=-1, keepdims=True)
        sel = lane == idx
        work = jnp.where(sel, neg, work)
        tops.append(m)
        idxs.append(idx)
        sels.append(sel)
    exps = [jnp.exp(t - tops[0]) for t in tops]
    denom = exps[0]
    for e in exps[1:]:
        denom = denom + e
    inv = 1.0 / denom

    onehot = jnp.zeros((tm, LANES), F32)
    for sel in sels:
        onehot = onehot + jnp.where(sel, 1.0, 0.0)
    ri = lax.broadcasted_iota(jnp.int32, (tm, tm), 0)
    ci = lax.broadcasted_iota(jnp.int32, (tm, tm), 1)
    tri = jnp.where(ri > ci, 1.0, 0.0).astype(BF16)
    carry = carry_scr[0:1, :]
    prefix = jnp.dot(tri, onehot.astype(BF16), preferred_element_type=F32) + carry
    new_carry = carry + jnp.sum(onehot, axis=0, keepdims=True)
    carry_scr[...] = jnp.broadcast_to(new_carry, carry_scr.shape)
    cnt_ref[...] = jnp.broadcast_to(new_carry, cnt_ref.shape)

    route = jnp.zeros((tm, LANES), F32)
    for k in range(TOP_K):
        rank = jnp.sum(jnp.where(sels[k], prefix, 0.0), axis=-1, keepdims=True)
        route = jnp.where(lane == k, idxs[k].astype(F32), route)
        route = jnp.where(lane == TOP_K + k, rank, route)
        route = jnp.where(lane == 2 * TOP_K + k, exps[k] * inv, route)
    route_ref[...] = route


def _outproj_route(ya, yb, x2, wa, wb, norm_w, w_router, b_router, n_experts):
    t, d = x2.shape
    dc = ya.shape[1]
    ds = yb.shape[1]
    tm = _largest_divisor(t, (512, 256, 128))
    kernel = functools.partial(_outproj_kernel, n_experts=n_experts)
    const = lambda shape: pl.BlockSpec(shape, lambda i: (0, 0))
    return pl.pallas_call(
        kernel,
        grid=(t // tm,),
        in_specs=[
            pl.BlockSpec((tm, dc), lambda i: (i, 0)),
            pl.BlockSpec((tm, ds), lambda i: (i, 0)),
            pl.BlockSpec((tm, d), lambda i: (i, 0)),
            const((dc, d)), const((ds, d)), const((1, d)), const((d, LANES)), const((1, LANES)),
        ],
        out_specs=[
            pl.BlockSpec((tm, d), lambda i: (i, 0)),
            pl.BlockSpec((tm, d), lambda i: (i, 0)),
            pl.BlockSpec((tm, LANES), lambda i: (i, 0)),
            pl.BlockSpec((8, LANES), lambda i: (0, 0)),
        ],
        out_shape=[
            jax.ShapeDtypeStruct((t, d), F32),
            jax.ShapeDtypeStruct((t, d), F32),
            jax.ShapeDtypeStruct((t, LANES), F32),
            jax.ShapeDtypeStruct((8, LANES), F32),
        ],
        scratch_shapes=[pltpu.VMEM((8, LANES), F32)],
        compiler_params=_cparams("arbitrary"),
        name="outproj_route",
    )(ya, yb, x2, wa, wb, norm_w, w_router, b_router)


ZERO_ROWS = 256


def _dispatch_kernel(zflag_ref, dest_ref, h_ref, o_ref, zbuf, sem, zsem, *, tm):
    tt = h_ref.shape[0]
    n_tiles = o_ref.shape[0] // tm
    per_tile = tm // ZERO_ROWS

    def zero_copy(i, p):
        return pltpu.make_async_copy(zbuf, o_ref.at[pl.ds(i * tm + p * ZERO_ROWS, ZERO_ROWS)], zsem)

    @pl.when(pl.program_id(0) == 0)
    def _():
        zbuf[...] = jnp.zeros_like(zbuf)

        def start(i, carry):
            @pl.when(zflag_ref[i] != 0)
            def _():
                for p in range(per_tile):
                    zero_copy(i, p).start()
            return carry

        def wait(i, carry):
            @pl.when(zflag_ref[i] != 0)
            def _():
                for p in range(per_tile):
                    zero_copy(i, p).wait()
            return carry

        lax.fori_loop(0, n_tiles, start, 0)
        lax.fori_loop(0, n_tiles, wait, 0)

    def issue(t, carry):
        for k in range(TOP_K):
            pltpu.make_async_copy(h_ref.at[pl.ds(t, 1)], o_ref.at[pl.ds(dest_ref[t * TOP_K + k], 1)],
                                  sem).start(priority=k % 2)
        return carry

    lax.fori_loop(0, tt, issue, 0, unroll=2)
    for _k in range(TOP_K):
        pltpu.make_async_copy(h_ref, o_ref.at[pl.ds(0, tt)], sem).wait()


def _dispatch(zflag, dest_flat, h, n_rows, tm):
    t, d = h.shape
    tt = _largest_divisor(t, (1024, 512, 256))
    assert tm % ZERO_ROWS == 0
    grid_spec = pltpu.PrefetchScalarGridSpec(
        num_scalar_prefetch=1,
        grid=(t // tt,),
        in_specs=[
            pl.BlockSpec((tt * TOP_K,), lambda i, zf: (i,), memory_space=pltpu.SMEM),
            pl.BlockSpec((tt, d), lambda i, zf: (i, 0)),
        ],
        out_specs=pl.BlockSpec(memory_space=pl.ANY),
        scratch_shapes=[pltpu.VMEM((ZERO_ROWS, d), h.dtype), pltpu.SemaphoreType.DMA(()),
                        pltpu.SemaphoreType.DMA(())],
    )
    return pl.pallas_call(
        functools.partial(_dispatch_kernel, tm=tm),
        grid_spec=grid_spec,
        out_shape=jax.ShapeDtypeStruct((n_rows, d), h.dtype),
        compiler_params=_cparams("arbitrary"),
        name="dispatch",
    )(zflag, dest_flat, h)


def _ring_copies(src, stage, sem, k, slot):
    _, rows, width = stage.shape
    w = src.shape[1]
    per = width // w
    return [pltpu.make_async_copy(src.at[pl.ds((k * per + p) * rows, rows)],
                                  stage.at[slot, :, p * w:(p + 1) * w], sem.at[slot])
            for p in range(per)]


def _ring_fill(src, stage, sem):
    for slot in range(stage.shape[0]):
        for c in _ring_copies(src, stage, sem, slot, slot):
            c.start()


def _load_cast(src, dst, stage, sem, ring_filled):
    n_slots, rows, width = stage.shape
    per = width // src.shape[1]
    w = src.shape[1]
    n = dst.shape[0] // (rows * per)
    copies = functools.partial(_ring_copies, src, stage, sem)

    @pl.when(jnp.logical_not(ring_filled))
    def _():
        _ring_fill(src, stage, sem)

    def body(kk, carry):
        for slot in range(n_slots):
            k = n_slots * kk + slot
            for c in copies(k, slot):
                c.wait()
            for p in range(per):
                row0 = pl.multiple_of((k * per + p) * rows, rows)
                dst[pl.ds(row0, rows), :] = stage[slot, :, p * w:(p + 1) * w].astype(BF16)

            @pl.when(k + n_slots < n)
            def _():
                for c in copies(k + n_slots, slot):
                    c.start()
        return carry

    lax.fori_loop(0, n // n_slots, body, 0)


def _expert_kernel(te_ref, nu_ref, nb_ref, x_ref, wgu_hbm, wd_hbm, bgu_ref, bd_ref, o_ref,
                   wgu_buf, wd_buf, xb_scr, act_scr, stage, sem, *, tf):
    i = pl.program_id(0)
    tm = x_ref.shape[0]
    f = wd_buf.shape[0]
    e = te_ref[i]
    active = i < nu_ref[0]

    @pl.when(active & ((i == 0) | (e != te_ref[jnp.maximum(i - 1, 0)])))
    def _():
        _load_cast(wgu_hbm.at[e], wgu_buf, stage, sem, ring_filled=i > 0)
        _load_cast(wd_hbm.at[e], wd_buf, stage, sem, ring_filled=False)

    e_next = te_ref[jnp.minimum(i + 1, pl.num_programs(0) - 1)]

    @pl.when((i + 1 < nu_ref[0]) & (e_next != e))
    def _():
        _ring_fill(wgu_hbm.at[e_next], stage, sem)

    def mlp(rows):
        xb_scr[0:rows, :] = x_ref[0:rows, :].astype(BF16)
        xb = xb_scr[0:rows, :]
        for j in range(f // tf):
            gcols = slice(j * tf, (j + 1) * tf)
            lcols = slice(f + j * tf, f + (j + 1) * tf)
            glu = jnp.dot(xb, wgu_buf[:, gcols], preferred_element_type=F32) + bgu_ref[:, gcols]
            lin = jnp.dot(xb, wgu_buf[:, lcols], preferred_element_type=F32) + bgu_ref[:, lcols]
            glu = jnp.minimum(glu, SWIGLU_LIMIT)
            lin = jnp.clip(lin, -SWIGLU_LIMIT, SWIGLU_LIMIT)
            act_scr[0:rows, gcols] = (glu * jax.nn.sigmoid(SWIGLU_ALPHA * glu) * (lin + 1.0)).astype(BF16)
        o_ref[0:rows, :] = (jnp.dot(act_scr[0:rows, :], wd_buf[...], preferred_element_type=F32)
                            + bd_ref[...]).astype(o_ref.dtype)
        if rows < tm:
            o_ref[rows:tm, :] = jnp.zeros((tm - rows, o_ref.shape[1]), o_ref.dtype)

    for blocks in range(1, tm // MOE_ROW_BLOCK + 1):
        pl.when(active & (nb_ref[i] == blocks))(functools.partial(mlp, blocks * MOE_ROW_BLOCK))

    @pl.when(jnp.logical_not(active))
    def _():
        o_ref[...] = jnp.zeros_like(o_ref)


WEIGHT_STAGE_ROWS = 64
WEIGHT_STAGE_SLOTS = 8


MOE_ROW_BLOCK = 128


def _experts(tile_e, n_used, tile_blocks, x_sorted, w_gu, b_gu, w_d, b_d, tm):
    n_rows, d = x_sorted.shape
    f = w_d.shape[1]
    chunk = WEIGHT_STAGE_ROWS * WEIGHT_STAGE_SLOTS
    assert (2 * f) % d == 0 and d % chunk == 0 and f % (chunk * (2 * f // d)) == 0
    assert tm % MOE_ROW_BLOCK == 0
    tf = _largest_divisor(f, (1024, 512, 256, 128))
    n_tiles = n_rows // tm
    grid_spec = pltpu.PrefetchScalarGridSpec(
        num_scalar_prefetch=3,
        grid=(n_tiles,),
        in_specs=[
            pl.BlockSpec((tm, d), lambda i, te, nu, nb: (jnp.minimum(i, nu[0] - 1), 0)),
            pl.BlockSpec(memory_space=pl.ANY),
            pl.BlockSpec(memory_space=pl.ANY),
            pl.BlockSpec((None, 1, 2 * f), lambda i, te, nu, nb: (te[i], 0, 0)),
            pl.BlockSpec((None, 1, d), lambda i, te, nu, nb: (te[i], 0, 0)),
        ],
        out_specs=pl.BlockSpec((tm, d), lambda i, te, nu, nb: (i, 0)),
        scratch_shapes=[
            pltpu.VMEM((d, 2 * f), BF16),
            pltpu.VMEM((f, d), BF16),
            pltpu.VMEM((tm, d), BF16),
            pltpu.VMEM((tm, f), BF16),
            pltpu.VMEM((WEIGHT_STAGE_SLOTS, WEIGHT_STAGE_ROWS, 2 * f), F32),
            pltpu.SemaphoreType.DMA((WEIGHT_STAGE_SLOTS,)),
        ],
    )
    return pl.pallas_call(
        functools.partial(_expert_kernel, tf=tf),
        grid_spec=grid_spec,
        out_shape=jax.ShapeDtypeStruct((n_rows, d), F32),
        compiler_params=_cparams("arbitrary"),
        name="experts",
    )(tile_e, n_used, tile_blocks, x_sorted, w_gu, w_d, b_gu, b_d)


def _combine_kernel(dest_ref, dest_next_ref, y_ref, x1_ref, route_ref, nw_ref, o_ref, buf, sem,
                    *, final_norm):
    i = pl.program_id(0)
    n = pl.num_programs(0)
    tt = x1_ref.shape[0]
    slot = i % 2

    def gather(idx_ref, dst_slot):
        def issue(t, carry):
            for k in range(TOP_K):
                pltpu.make_async_copy(y_ref.at[pl.ds(idx_ref[t * TOP_K + k], 1)],
                                      buf.at[dst_slot, pl.ds(k * tt + t, 1)],
                                      sem.at[dst_slot]).start(priority=k % 2)
            return carry

        lax.fori_loop(0, tt, issue, 0)

    @pl.when(i == 0)
    def _():
        gather(dest_ref, slot)

    @pl.when(i + 1 < n)
    def _():
        gather(dest_next_ref, 1 - slot)

    pltpu.make_async_copy(y_ref.at[pl.ds(0, TOP_K * tt)], buf.at[slot], sem.at[slot]).wait()

    route = route_ref[...]
    x = x1_ref[...]
    for k in range(TOP_K):
        gate = route[:, 2 * TOP_K + k:2 * TOP_K + k + 1]
        x = x + gate * buf[slot, k * tt:(k + 1) * tt, :]
    if final_norm:
        x = x * lax.rsqrt(jnp.mean(x * x, axis=-1, keepdims=True) + EPS) * nw_ref[...]
    o_ref[...] = x


def _combine(dest_flat, y_sorted, x1, route, norm_w, final_norm):
    t, d = x1.shape
    tt = _largest_divisor(t, (512, 256))
    n = t // tt
    return pl.pallas_call(
        functools.partial(_combine_kernel, final_norm=final_norm),
        grid=(n,),
        in_specs=[
            pl.BlockSpec((tt * TOP_K,), lambda i: (i,), memory_space=pltpu.SMEM),
            pl.BlockSpec((tt * TOP_K,), lambda i: (jnp.minimum(i + 1, n - 1),), memory_space=pltpu.SMEM),
            pl.BlockSpec(memory_space=pl.ANY),
            pl.BlockSpec((tt, d), lambda i: (i, 0)),
            pl.BlockSpec((tt, LANES), lambda i: (i, 0)),
            pl.BlockSpec((1, d), lambda i: (0, 0)),
        ],
        out_specs=pl.BlockSpec((tt, d), lambda i: (i, 0)),
        out_shape=jax.ShapeDtypeStruct((t, d), F32),
        scratch_shapes=[pltpu.VMEM((2, TOP_K * tt, d), F32), pltpu.SemaphoreType.DMA((2,))],
        compiler_params=_cparams("arbitrary"),
        name="combine",
    )(dest_flat, dest_flat, y_sorted, x1, route, norm_w)


MOE_ROW_TILE = 512


def _layer(x2, bsz, seq, norm_mix_w, w_in, conv_a_w, ssd_conv_w, ssd_conv_b, dt_bias_fw, dt_bias_bw,
           a_log_fw, a_log_bw, d_skip, ssd_norm_w, w_out, norm_ffn_w, w_router, b_router,
           w_gate_up, b_gate_up, w_down, b_down):
    t, d = x2.shape
    dc = conv_a_w.shape[1]
    ds = ssd_norm_w.shape[0]
    dxbc = ssd_conv_w.shape[1]
    heads = dt_bias_fw.shape[0]
    g = SSD_GROUPS
    hpg = heads // g
    n_main = 3 * dc + ds + dxbc
    n_experts = w_router.shape[1]

    w_all = w_in.astype(BF16)
    w_dt = jnp.transpose(w_in[:, n_main:].reshape(d, 2, g, hpg), (0, 2, 1, 3)).reshape(d, g, 2 * hpg)
    w_dt = jnp.pad(w_dt, ((0, 0), (0, 0), (0, SUBLANES - 2 * hpg))).reshape(d, g * SUBLANES)
    w_dt = jnp.pad(w_dt, ((0, 0), (0, LANES - g * SUBLANES))).astype(BF16)
    proj, dt_rows = _inproj(x2, norm_mix_w.reshape(1, d), w_all, n_main, w_dt)
    proj3 = proj.reshape(bsz, seq, n_main)
    y_a = _conv_a(proj3, conv_a_w, dc)
    xbc_act = _conv_ssd(proj3, ssd_conv_w, ssd_conv_b.reshape(1, dxbc), 3 * dc + ds, dxbc)

    def head_rows(fw, bw):
        both = jnp.concatenate([fw.reshape(g, hpg), bw.reshape(g, hpg)], axis=1)
        return jnp.pad(both, ((0, 0), (0, SUBLANES - 2 * hpg))).reshape(g, SUBLANES, 1)

    dskip = jnp.repeat(d_skip, SSD_HEAD_DIM).reshape(1, ds)
    y_b = _ssd(xbc_act, proj3, dt_rows, head_rows(dt_bias_fw, dt_bias_bw), head_rows(a_log_fw, a_log_bw),
               dskip, ssd_norm_w.reshape(1, ds), ds, 3 * dc)

    w_out_b = w_out.astype(BF16)
    w_r = jnp.pad(w_router, ((0, 0), (0, LANES - n_experts)))
    b_r = jnp.pad(b_router, (0, LANES - n_experts)).reshape(1, LANES)
    x1, h2, route, counts = _outproj_route(
        y_a.reshape(t, dc), y_b.reshape(t, ds), x2, w_out_b[:dc], w_out_b[dc:],
        norm_ffn_w.reshape(1, d), w_r, b_r, n_experts)

    tm = MOE_ROW_TILE
    n_slots = t * TOP_K
    n_tiles = n_slots // tm + n_experts
    cnt = counts[0, :n_experts].astype(jnp.int32)
    padded = (cnt + tm - 1) // tm * tm
    pends = jnp.cumsum(padded)
    pstarts = pends - padded
    e_idx = route[:, :TOP_K].astype(jnp.int32)
    rank = route[:, TOP_K:2 * TOP_K].astype(jnp.int32)
    experts = jnp.arange(n_experts, dtype=jnp.int32)
    dest = (jnp.sum((e_idx[..., None] == experts) * pstarts, axis=-1) + rank).reshape(n_slots)
    n_used = (pends[-1] // tm).astype(jnp.int32)
    tile_ids = jnp.arange(n_tiles, dtype=jnp.int32)
    tile_row0 = jnp.minimum(tile_ids, n_used - 1) * tm
    tile_e = jnp.minimum(jnp.sum((pends[None, :] <= tile_row0[:, None]).astype(jnp.int32), axis=1),
                         n_experts - 1)
    seg_end = jnp.sum((tile_e[:, None] == experts) * (pstarts + cnt), axis=-1)
    tile_valid = jnp.clip(seg_end - tile_row0, 1, tm)
    tile_blocks = ((tile_valid + MOE_ROW_BLOCK - 1) // MOE_ROW_BLOCK).astype(jnp.int32)
    partial_last = (pends - tm) * ((cnt % tm) != 0) - (cnt % tm == 0)
    zflag = (jnp.any(partial_last[None, :] == (tile_ids * tm)[:, None], axis=1)
             | (tile_ids >= n_used)).astype(jnp.int32)

    x_sorted = _dispatch(zflag, dest, h2, n_tiles * tm, tm)
    f = w_down.shape[1]
    y_sorted = _experts(tile_e, n_used.reshape(1), tile_blocks, x_sorted, w_gate_up,
                        b_gate_up.reshape(n_experts, 1, 2 * f), w_down,
                        b_down.reshape(n_experts, 1, d), tm)
    return dest, y_sorted, x1, route


def kernel(x, norm_mix_w, w_in, conv_a_w, ssd_conv_w, ssd_conv_b, dt_bias_fw, dt_bias_bw, a_log_fw,
           a_log_bw, d_skip, ssd_norm_w, w_out, norm_ffn_w, w_router, b_router, w_gate_up, b_gate_up,
           w_down, b_down, norm_final_w):
    bsz, seq, d = x.shape
    depth = w_in.shape[0]
    x2 = x.reshape(bsz * seq, d)
    for layer in range(depth):
        dest, y_sorted, x1, route = _layer(
            x2, bsz, seq, norm_mix_w[layer], w_in[layer], conv_a_w[layer], ssd_conv_w[layer],
            ssd_conv_b[layer], dt_bias_fw[layer], dt_bias_bw[layer], a_log_fw[layer], a_log_bw[layer],
            d_skip[layer], ssd_norm_w[layer], w_out[layer], norm_ffn_w[layer], w_router[layer],
            b_router[layer], w_gate_up[layer], b_gate_up[layer], w_down[layer], b_down[layer])
        x2 = _combine(dest, y_sorted, x1, route, norm_final_w.reshape(1, d), layer == depth - 1)
    return x2.reshape(bsz, seq, d)
```

```python
import functools

import jax
import jax.numpy as jnp
from jax import lax
from jax.experimental import pallas as pl
from jax.experimental.pallas import tpu as pltpu

F32 = jnp.float32
BF16 = jnp.bfloat16

EPS = 1e-5
SSD_HEAD_DIM = 64
SSD_GROUPS = 4
SSD_STATE = 128
SSD_CHUNK = 128
TOP_K = 4
SWIGLU_LIMIT = 7.0
SWIGLU_ALPHA = 1.702

LANES = 128
SUBLANES = 8
VMEM_LIMIT_BYTES = 56 * 1024 * 1024


def _largest_divisor(n, candidates):
    for c in candidates:
        if n % c == 0:
            return c
    raise ValueError(f"no tile in {candidates} divides {n}")


def _cparams(*sem):
    return pltpu.CompilerParams(dimension_semantics=tuple(sem), vmem_limit_bytes=VMEM_LIMIT_BYTES)


def _inproj_kernel(x_ref, nw_ref, w_ref, wdt_ref, o_ref, dt_ref, h_scr):
    @pl.when(pl.program_id(1) == 0)
    def _():
        x = x_ref[...]
        ms = jnp.mean(x * x, axis=-1, keepdims=True)
        h = (x * lax.rsqrt(ms + EPS) * nw_ref[...]).astype(BF16)
        h_scr[...] = h
        dt_ref[...] = jnp.dot(h, wdt_ref[...], preferred_element_type=F32).T

    o_ref[...] = jnp.dot(h_scr[...], w_ref[...], preferred_element_type=F32).astype(o_ref.dtype)


def _inproj(x2, norm_w, w_all, n, w_dt):
    t, d = x2.shape
    tm = _largest_divisor(t, (1024, 512, 256, 128))
    tn = _largest_divisor(n, (2048, 1536, 1024, 512, 256, 128))
    return pl.pallas_call(
        _inproj_kernel,
        grid=(t // tm, n // tn),
        in_specs=[
            pl.BlockSpec((tm, d), lambda i, j: (i, 0)),
            pl.BlockSpec((1, d), lambda i, j: (0, 0)),
            pl.BlockSpec((d, tn), lambda i, j: (0, j)),
            pl.BlockSpec((d, LANES), lambda i, j: (0, 0)),
        ],
        out_specs=[
            pl.BlockSpec((tm, tn), lambda i, j: (i, j)),
            pl.BlockSpec((LANES, tm), lambda i, j: (0, i)),
        ],
        out_shape=[
            jax.ShapeDtypeStruct((t, n), BF16),
            jax.ShapeDtypeStruct((LANES, t), F32),
        ],
        scratch_shapes=[pltpu.VMEM((tm, d), BF16)],
        compiler_params=_cparams("parallel", "arbitrary"),
        name="inproj",
    )(x2, norm_w, w_all, w_dt)


def _centred_conv(v, w_ref):
    s = v.shape[0]
    width = w_ref.shape[0]
    half = width // 2
    edge = 2 * SUBLANES
    assert half <= SUBLANES and s >= 2 * edge

    def taps(u, mask_rows):
        n = u.shape[0]
        row = lax.broadcasted_iota(jnp.int32, u.shape, 0)
        acc = u * w_ref[half:half + 1, :]
        for k in range(width):
            off = k - half
            if off == 0:
                continue
            shifted = pltpu.roll(u, (-off) % n, 0)
            if mask_rows:
                shifted = jnp.where((row + off >= 0) & (row + off < n), shifted, 0.0)
            acc = acc + shifted * w_ref[k:k + 1, :]
        return acc

    body = taps(v, False)
    head = taps(v[0:edge, :], True)[0:SUBLANES, :]
    tail = taps(v[s - edge:s, :], True)[SUBLANES:edge, :]
    return jnp.concatenate([head, body[SUBLANES:s - SUBLANES, :], tail], axis=0)


def _conv_a_kernel(gb_ref, gc_ref, u_ref, w_ref, o_ref):
    v = gc_ref[0].astype(F32) * u_ref[0].astype(F32)
    o_ref[0] = (gb_ref[0].astype(F32) * _centred_conv(v, w_ref)).astype(o_ref.dtype)


def _conv_a(proj3, conv_w, dc):
    b, s, _ = proj3.shape
    tc = _largest_divisor(dc, (512, 256, 128))
    nb = dc // tc
    blk = lambda off: pl.BlockSpec((1, s, tc), lambda i, j: (i, 0, off + j))
    return pl.pallas_call(
        _conv_a_kernel,
        grid=(b, nb),
        in_specs=[blk(0), blk(nb), blk(2 * nb),
                  pl.BlockSpec((conv_w.shape[0], tc), lambda i, j: (0, j))],
        out_specs=pl.BlockSpec((1, s, tc), lambda i, j: (i, 0, j)),
        out_shape=jax.ShapeDtypeStruct((b, s, dc), BF16),
        compiler_params=_cparams("parallel", "parallel"),
        name="conv_a",
    )(proj3, proj3, proj3, conv_w)


CONV_CHUNK = 128
CONV_HALO = 16


def _conv_ssd_kernel(x_ref, w_ref, b_ref, o_ref):
    s = x_ref.shape[1]
    width = w_ref.shape[0]
    half = width // 2
    win_rows = CONV_CHUNK + 2 * CONV_HALO
    assert half <= CONV_HALO and s % CONV_CHUNK == 0 and s >= win_rows

    r = lax.broadcasted_iota(jnp.int32, (CONV_CHUNK, win_rows), 0)
    j = lax.broadcasted_iota(jnp.int32, (CONV_CHUNK, win_rows), 1)
    shift = {}

    def shift_matrix(lead, off):
        if (lead, off) not in shift:
            shift[(lead, off)] = jnp.where(j == r + (lead + off), 1.0, 0.0).astype(x_ref.dtype)
        return shift[(lead, off)]

    for c in range(s // CONV_CHUNK):
        r0 = c * CONV_CHUNK
        w0 = min(max(r0 - CONV_HALO, 0), s - win_rows)
        win = x_ref[0, w0:w0 + win_rows, :]
        acc = x_ref[0, r0:r0 + CONV_CHUNK, :].astype(F32) * w_ref[half:half + 1, :]
        for k in range(width):
            off = k - half
            if off != 0:
                acc = acc + jnp.dot(shift_matrix(r0 - w0, off), win,
                                    preferred_element_type=F32) * w_ref[k:k + 1, :]
        y = acc + b_ref[...]
        o_ref[0, r0:r0 + CONV_CHUNK, :] = (y * jax.nn.sigmoid(y)).astype(o_ref.dtype)


def _conv_ssd(proj3, conv_w, conv_b, col0, dxbc):
    b, s, _ = proj3.shape
    tc = _largest_divisor(dxbc, (512, 256, 128))
    assert col0 % tc == 0
    off = col0 // tc
    return pl.pallas_call(
        _conv_ssd_kernel,
        grid=(b, dxbc // tc),
        in_specs=[pl.BlockSpec((1, s, tc), lambda i, j: (i, 0, off + j)),
                  pl.BlockSpec((conv_w.shape[0], tc), lambda i, j: (0, j)),
                  pl.BlockSpec((1, tc), lambda i, j: (0, j))],
        out_specs=pl.BlockSpec((1, s, tc), lambda i, j: (i, 0, j)),
        out_shape=jax.ShapeDtypeStruct((b, s, dxbc), BF16),
        compiler_params=_cparams("parallel", "parallel"),
        name="conv_ssd",
    )(proj3, conv_w, conv_b)


def _split_rows(v, passes):
    parts = []
    rem = v
    for _ in range(passes):
        term = rem.astype(BF16).astype(F32)
        parts.append(term)
        rem = rem - term
    while len(parts) % 2:
        parts.append(jnp.zeros_like(v))
    return jnp.concatenate(parts, axis=0).astype(BF16)


def _expander(n_rows, lanes_per_row, n_cols):
    r = lax.broadcasted_iota(jnp.int32, (n_rows, n_cols), 0) & (SUBLANES - 1)
    c = lax.broadcasted_iota(jnp.int32, (n_rows, n_cols), 1)
    lo = r * lanes_per_row
    return jnp.where((c >= lo) & (c < lo + lanes_per_row), 1.0, 0.0).astype(BF16)


def _expand(rows_bf16, expander):
    return lax.dot_general(rows_bf16, expander, (((0,), (0,)), ((), ())), preferred_element_type=F32)


def _ssd_kernel(xs_ref, b_ref, c_ref, z_ref, dt_ref, bias_ref, alog_ref, dskip_ref, nw_ref, o_ref,
                a_scr, dt_scr, qx_scr, ex_scr, acol_scr, st_scr, *, hpg):
    s = xs_ref.shape[1]
    gw = xs_ref.shape[2]
    nc = s // SSD_CHUNK
    L = SSD_CHUNK
    hd = SSD_HEAD_DIM

    raw = dt_ref[...] + bias_ref[0]
    dt = jnp.maximum(raw, 0.0) + jnp.log1p(jnp.exp(-jnp.abs(raw)))
    dta = dt * (-jnp.exp(alog_ref[0]))
    pos = lax.broadcasted_iota(jnp.int32, (SUBLANES, s), 1) & (L - 1)
    head_row = lax.broadcasted_iota(jnp.int32, (SUBLANES, s), 0)
    pre = dta
    suf = dta
    sh = 1
    while sh < L:
        pre = pre + jnp.where(pos >= sh, pltpu.roll(pre, sh, 1), 0.0)
        suf = suf + jnp.where(pos < L - sh, pltpu.roll(suf, s - sh, 1), 0.0)
        sh *= 2
    acum = jnp.where(head_row < hpg, pre, suf)
    tot = pre + suf - dta
    for c in range(nc):
        lanes = slice(c * L, (c + 1) * L)
        a_scr[c] = acum[:, lanes]
        dt_scr[c] = dt[:, lanes]

    exp_head = _expander(2 * SUBLANES, hd, 2 * gw)
    exp_col = _expander(4 * SUBLANES, L, 2 * hpg * L)
    qx_scr[...] = _expand(_split_rows(dt * jnp.exp(tot - acum), 2), exp_head)
    ex_scr[...] = _expand(_split_rows(jnp.exp(acum), 2), exp_head)
    acol_scr[...] = _expand(_split_rows(acum, 3), exp_col)

    def phase1(i, run_b):
        c = nc - 1 - i
        row0 = pl.multiple_of(c * L, L)
        rows = pl.ds(row0, L)
        x = xs_ref[0, rows, :].astype(F32)
        qx = qx_scr[rows, :]
        xw = jnp.concatenate([x * qx[:, 0:gw], x * qx[:, gw:2 * gw]], axis=1).astype(BF16)
        contrib = lax.dot_general(b_ref[0, rows, :], xw, (((0,), (0,)), ((), ())),
                                  preferred_element_type=F32)
        st_scr[c, :, 0:gw] = contrib[:, 0:gw]
        st_scr[c, :, gw:2 * gw] = run_b
        dec_b = ex_scr[pl.ds(row0, 1), gw:2 * gw]
        return run_b * dec_b + contrib[:, gw:2 * gw]

    lax.fori_loop(0, nc, phase1, jnp.zeros((SSD_STATE, gw), F32), unroll=8)

    li = lax.broadcasted_iota(jnp.int32, (L, L), 0)
    si = lax.broadcasted_iota(jnp.int32, (L, L), 1)
    causal = li >= si
    below = li > si
    above = si > li
    lane_l = lax.broadcasted_iota(jnp.int32, (L, LANES), 1)

    def phase2(c, run_f):
        rows = pl.ds(pl.multiple_of(c * L, L), L)
        xb = xs_ref[0, rows, :]
        cm = c_ref[0, rows, :]
        cb = lax.dot_general(cm, b_ref[0, rows, :], (((1,), (1,)), ((), ())), preferred_element_type=F32)
        at = a_scr[c]
        dtt = dt_scr[c]
        acol = acol_scr[rows, :]
        ys = []
        for p in range(hpg // 2):
            xpair = xb[:, p * LANES:(p + 1) * LANES]
            y_pair = None
            for q in range(2):
                kf = 2 * p + q
                kb = hpg + kf
                seg = jnp.where(causal, acol[:, kf * L:(kf + 1) * L] - at[kf:kf + 1, :],
                                acol[:, kb * L:(kb + 1) * L] - at[kb:kb + 1, :])
                dt_f = dtt[kf:kf + 1, :]
                dt_b = dtt[kb:kb + 1, :]
                w = jnp.where(below, dt_f, jnp.where(above, dt_b, dt_f + dt_b))
                m = (cb * jnp.exp(seg) * w).astype(BF16)
                in_head = (lane_l < hd) if q == 0 else (lane_l >= hd)
                part = jnp.dot(m, jnp.where(in_head, xpair, jnp.zeros_like(xpair)),
                               preferred_element_type=F32)
                y_pair = part if y_pair is None else y_pair + part
            ys.append(y_pair)
        y = ys[0] if len(ys) == 1 else jnp.concatenate(ys, axis=1)
        ex = ex_scr[rows, :]
        cs_f = jnp.dot(cm, run_f.astype(BF16), preferred_element_type=F32)
        cs_b = jnp.dot(cm, st_scr[c, :, gw:2 * gw].astype(BF16), preferred_element_type=F32)
        y = y + cs_f * ex[:, 0:gw] + cs_b * ex[:, gw:2 * gw]
        y = y + xb.astype(F32) * dskip_ref[...]
        z = z_ref[0, rows, :].astype(F32)
        g = y * (z * jax.nn.sigmoid(z))
        g = g * lax.rsqrt(jnp.mean(g * g, axis=-1, keepdims=True) + EPS)
        o_ref[0, rows, :] = (g * nw_ref[...]).astype(o_ref.dtype)
        dec_f = ex[L - 1:L, 0:gw]
        return run_f * dec_f + st_scr[c, :, 0:gw]

    lax.fori_loop(0, nc, phase2, jnp.zeros((SSD_STATE, gw), F32), unroll=8)


def _ssd(xbc_act, proj3, dt_rows, bias_col, alog_col, dskip, norm_w, ds, z_col0):
    b, s, _ = xbc_act.shape
    g = SSD_GROUPS
    gw = ds // g
    hpg = gw // SSD_HEAD_DIM
    nc = s // SSD_CHUNK
    n = SSD_STATE
    L = SSD_CHUNK
    assert z_col0 % gw == 0 and ds % n == 0 and hpg % 2 == 0 and 2 * hpg <= SUBLANES
    kernel = functools.partial(_ssd_kernel, hpg=hpg)
    return pl.pallas_call(
        kernel,
        grid=(b, g),
        in_specs=[
            pl.BlockSpec((1, s, gw), lambda i, j: (i, 0, j)),
            pl.BlockSpec((1, s, n), lambda i, j: (i, 0, ds // n + j)),
            pl.BlockSpec((1, s, n), lambda i, j: (i, 0, ds // n + g + j)),
            pl.BlockSpec((1, s, gw), lambda i, j: (i, 0, z_col0 // gw + j)),
            pl.BlockSpec((SUBLANES, s), lambda i, j: (j, i)),
            pl.BlockSpec((1, SUBLANES, 1), lambda i, j: (j, 0, 0)),
            pl.BlockSpec((1, SUBLANES, 1), lambda i, j: (j, 0, 0)),
            pl.BlockSpec((1, gw), lambda i, j: (0, j)),
            pl.BlockSpec((1, gw), lambda i, j: (0, j)),
        ],
        out_specs=pl.BlockSpec((1, s, gw), lambda i, j: (i, 0, j)),
        out_shape=jax.ShapeDtypeStruct((b, s, ds), BF16),
        scratch_shapes=[
            pltpu.VMEM((nc, SUBLANES, L), F32),
            pltpu.VMEM((nc, SUBLANES, L), F32),
            pltpu.VMEM((s, 2 * gw), F32),
            pltpu.VMEM((s, 2 * gw), F32),
            pltpu.VMEM((s, 2 * hpg * L), F32),
            pltpu.VMEM((nc, n, 2 * gw), F32),
        ],
        compiler_params=_cparams("parallel", "parallel"),
        name="ssd",
    )(xbc_act, xbc_act, xbc_act, proj3, dt_rows, bias_col, alog_col, dskip, norm_w)


def _split_bf16(a):
    hi = a.astype(BF16)
    lo = (a - hi.astype(F32)).astype(BF16)
    return hi, lo


def _outproj_kernel(ya_ref, yb_ref, x_ref, wa_ref, wb_ref, nw_ref, wr_ref, br_ref,
                    x1_ref, h_ref, route_ref, cnt_ref, carry_scr, *, n_experts):
    i = pl.program_id(0)
    tm = x_ref.shape[0]

    @pl.when(i == 0)
    def _():
        carry_scr[...] = jnp.zeros_like(carry_scr)

    x1 = x_ref[...] + jnp.dot(ya_ref[...], wa_ref[...], preferred_element_type=F32) \
        + jnp.dot(yb_ref[...], wb_ref[...], preferred_element_type=F32)
    x1_ref[...] = x1
    h = x1 * lax.rsqrt(jnp.mean(x1 * x1, axis=-1, keepdims=True) + EPS) * nw_ref[...]
    h_ref[...] = h

    h_hi, h_lo = _split_bf16(h)
    w_hi, w_lo = _split_bf16(wr_ref[...])
    logits = (jnp.dot(h_hi, w_hi, preferred_element_type=F32)
              + jnp.dot(h_hi, w_lo, preferred_element_type=F32)
              + jnp.dot(h_lo, w_hi, preferred_element_type=F32)) + br_ref[...]

    lane = lax.broadcasted_iota(jnp.int32, (tm, LANES), 1)
    neg = jnp.finfo(F32).min
    work = jnp.where(lane < n_experts, logits, neg)
    tops, idxs, sels = [], [], []
    for _k in range(TOP_K):
        m = jnp.max(work, axis=-1, keepdims=True)
        idx = jnp.min(jnp.where(work == m, lane, LANES), axis=-1, keepdims=True)
        sel = lane == idx
        work = jnp.where(sel, neg, work)
        tops.append(m)
        idxs.append(idx)
        sels.append(sel)
    exps = [jnp.exp(t - tops[0]) for t in tops]
    denom = exps[0]
    for e in exps[1:]:
        denom = denom + e
    inv = 1.0 / denom

    onehot = jnp.zeros((tm, LANES), F32)
    for sel in sels:
        onehot = onehot + jnp.where(sel, 1.0, 0.0)
    ri = lax.broadcasted_iota(jnp.int32, (tm, tm), 0)
    ci = lax.broadcasted_iota(jnp.int32, (tm, tm), 1)
    tri = jnp.where(ri > ci, 1.0, 0.0).astype(BF16)
    carry = carry_scr[0:1, :]
    prefix = jnp.dot(tri, onehot.astype(BF16), preferred_element_type=F32) + carry
    new_carry = carry + jnp.sum(onehot, axis=0, keepdims=True)
    carry_scr[...] = jnp.broadcast_to(new_carry, carry_scr.shape)
    cnt_ref[...] = jnp.broadcast_to(new_carry, cnt_ref.shape)

    route = jnp.zeros((tm, LANES), F32)
    for k in range(TOP_K):
        rank = jnp.sum(jnp.where(sels[k], prefix, 0.0), axis=-1, keepdims=True)
        route = jnp.where(lane == k, idxs[k].astype(F32), route)
        route = jnp.where(lane == TOP_K + k, rank, route)
        route = jnp.where(lane == 2 * TOP_K + k, exps[k] * inv, route)
    route_ref[...] = route


def _outproj_route(ya, yb, x2, wa, wb, norm_w, w_router, b_router, n_experts):
    t, d = x2.shape
    dc = ya.shape[1]
    ds = yb.shape[1]
    tm = _largest_divisor(t, (512, 256, 128))
    kernel = functools.partial(_outproj_kernel, n_experts=n_experts)
    const = lambda shape: pl.BlockSpec(shape, lambda i: (0, 0))
    return pl.pallas_call(
        kernel,
        grid=(t // tm,),
        in_specs=[
            pl.BlockSpec((tm, dc), lambda i: (i, 0)),
            pl.BlockSpec((tm, ds), lambda i: (i, 0)),
            pl.BlockSpec((tm, d), lambda i: (i, 0)),
            const((dc, d)), const((ds, d)), const((1, d)), const((d, LANES)), const((1, LANES)),
        ],
        out_specs=[
            pl.BlockSpec((tm, d), lambda i: (i, 0)),
            pl.BlockSpec((tm, d), lambda i: (i, 0)),
            pl.BlockSpec((tm, LANES), lambda i: (i, 0)),
            pl.BlockSpec((8, LANES), lambda i: (0, 0)),
        ],
        out_shape=[
            jax.ShapeDtypeStruct((t, d), F32),
            jax.ShapeDtypeStruct((t, d), F32),
            jax.ShapeDtypeStruct((t, LANES), F32),
            jax.ShapeDtypeStruct((8, LANES), F32),
        ],
        scratch_shapes=[pltpu.VMEM((8, LANES), F32)],
        compiler_params=_cparams("arbitrary"),
        name="outproj_route",
    )(ya, yb, x2, wa, wb, norm_w, w_router, b_router)


ZERO_ROWS = 256


def _dispatch_kernel(zflag_ref, dest_ref, h_ref, o_ref, zbuf, sem, zsem, *, tm):
    tt = h_ref.shape[0]
    n_tiles = o_ref.shape[0] // tm
    per_tile = tm // ZERO_ROWS

    def zero_copy(i, p):
        return pltpu.make_async_copy(zbuf, o_ref.at[pl.ds(i * tm + p * ZERO_ROWS, ZERO_ROWS)], zsem)

    @pl.when(pl.program_id(0) == 0)
    def _():
        zbuf[...] = jnp.zeros_like(zbuf)

        def start(i, carry):
            @pl.when(zflag_ref[i] != 0)
            def _():
                for p in range(per_tile):
                    zero_copy(i, p).start()
            return carry

        def wait(i, carry):
            @pl.when(zflag_ref[i] != 0)
            def _():
                for p in range(per_tile):
                    zero_copy(i, p).wait()
            return carry

        lax.fori_loop(0, n_tiles, start, 0)
        lax.fori_loop(0, n_tiles, wait, 0)

    def issue(t, carry):
        for k in range(TOP_K):
            pltpu.make_async_copy(h_ref.at[pl.ds(t, 1)], o_ref.at[pl.ds(dest_ref[t * TOP_K + k], 1)],
                                  sem).start(priority=k % 2)
        return carry

    lax.fori_loop(0, tt, issue, 0, unroll=2)
    for _k in range(TOP_K):
        pltpu.make_async_copy(h_ref, o_ref.at[pl.ds(0, tt)], sem).wait()


def _dispatch(zflag, dest_flat, h, n_rows, tm):
    t, d = h.shape
    tt = _largest_divisor(t, (1024, 512, 256))
    assert tm % ZERO_ROWS == 0
    grid_spec = pltpu.PrefetchScalarGridSpec(
        num_scalar_prefetch=1,
        grid=(t // tt,),
        in_specs=[
            pl.BlockSpec((tt * TOP_K,), lambda i, zf: (i,), memory_space=pltpu.SMEM),
            pl.BlockSpec((tt, d), lambda i, zf: (i, 0)),
        ],
        out_specs=pl.BlockSpec(memory_space=pl.ANY),
        scratch_shapes=[pltpu.VMEM((ZERO_ROWS, d), h.dtype), pltpu.SemaphoreType.DMA(()),
                        pltpu.SemaphoreType.DMA(())],
    )
    return pl.pallas_call(
        functools.partial(_dispatch_kernel, tm=tm),
        grid_spec=grid_spec,
        out_shape=jax.ShapeDtypeStruct((n_rows, d), h.dtype),
        compiler_params=_cparams("arbitrary"),
        name="dispatch",
    )(zflag, dest_flat, h)


def _ring_copies(src, stage, sem, k, slot):
    _, rows, width = stage.shape
    w = src.shape[1]
    per = width // w
    return [pltpu.make_async_copy(src.at[pl.ds((k * per + p) * rows, rows)],
                                  stage.at[slot, :, p * w:(p + 1) * w], sem.at[slot])
            for p in range(per)]


def _ring_fill(src, stage, sem):
    for slot in range(stage.shape[0]):
        for c in _ring_copies(src, stage, sem, slot, slot):
            c.start()


def _load_cast(src, dst, stage, sem, ring_filled):
    n_slots, rows, width = stage.shape
    per = width // src.shape[1]
    w = src.shape[1]
    n = dst.shape[0] // (rows * per)
    copies = functools.partial(_ring_copies, src, stage, sem)

    @pl.when(jnp.logical_not(ring_filled))
    def _():
        _ring_fill(src, stage, sem)

    def body(kk, carry):
        for slot in range(n_slots):
            k = n_slots * kk + slot
            for c in copies(k, slot):
                c.wait()
            for p in range(per):
                row0 = pl.multiple_of((k * per + p) * rows, rows)
                dst[pl.ds(row0, rows), :] = stage[slot, :, p * w:(p + 1) * w].astype(BF16)

            @pl.when(k + n_slots < n)
            def _():
                for c in copies(k + n_slots, slot):
                    c.start()
        return carry

    lax.fori_loop(0, n // n_slots, body, 0)


def _expert_kernel(te_ref, nu_ref, nb_ref, x_ref, wgu_hbm, wd_hbm, bgu_ref, bd_ref, o_ref,
                   wgu_buf, wd_buf, xb_scr, act_scr, stage, sem, *, tf):
    i = pl.program_id(0)
    tm = x_ref.shape[0]
    f = wd_buf.shape[0]
    e = te_ref[i]
    active = i < nu_ref[0]

    @pl.when(active & ((i == 0) | (e != te_ref[jnp.maximum(i - 1, 0)])))
    def _():
        _load_cast(wgu_hbm.at[e], wgu_buf, stage, sem, ring_filled=i > 0)
        _load_cast(wd_hbm.at[e], wd_buf, stage, sem, ring_filled=False)

    e_next = te_ref[jnp.minimum(i + 1, pl.num_programs(0) - 1)]

    @pl.when((i + 1 < nu_ref[0]) & (e_next != e))
    def _():
        _ring_fill(wgu_hbm.at[e_next], stage, sem)

    def mlp(rows):
        xb_scr[0:rows, :] = x_ref[0:rows, :].astype(BF16)
        xb = xb_scr[0:rows, :]
        for j in range(f // tf):
            gcols = slice(j * tf, (j + 1) * tf)
            lcols = slice(f + j * tf, f + (j + 1) * tf)
            glu = jnp.dot(xb, wgu_buf[:, gcols], preferred_element_type=F32) + bgu_ref[:, gcols]
            lin = jnp.dot(xb, wgu_buf[:, lcols], preferred_element_type=F32) + bgu_ref[:, lcols]
            glu = jnp.minimum(glu, SWIGLU_LIMIT)
            lin = jnp.clip(lin, -SWIGLU_LIMIT, SWIGLU_LIMIT)
            act_scr[0:rows, gcols] = (glu * jax.nn.sigmoid(SWIGLU_ALPHA * glu) * (lin + 1.0)).astype(BF16)
        o_ref[0:rows, :] = (jnp.dot(act_scr[0:rows, :], wd_buf[...], preferred_element_type=F32)
                            + bd_ref[...]).astype(o_ref.dtype)
        if rows < tm:
            o_ref[rows:tm, :] = jnp.zeros((tm - rows, o_ref.shape[1]), o_ref.dtype)

    for blocks in range(1, tm // MOE_ROW_BLOCK + 1):
        pl.when(active & (nb_ref[i] == blocks))(functools.partial(mlp, blocks * MOE_ROW_BLOCK))

    @pl.when(jnp.logical_not(active))
    def _():
        o_ref[...] = jnp.zeros_like(o_ref)


WEIGHT_STAGE_ROWS = 64
WEIGHT_STAGE_SLOTS = 8


MOE_ROW_BLOCK = 128


def _experts(tile_e, n_used, tile_blocks, x_sorted, w_gu, b_gu, w_d, b_d, tm):
    n_rows, d = x_sorted.shape
    f = w_d.shape[1]
    chunk = WEIGHT_STAGE_ROWS * WEIGHT_STAGE_SLOTS
    assert (2 * f) % d == 0 and d % chunk == 0 and f % (chunk * (2 * f // d)) == 0
    assert tm % MOE_ROW_BLOCK == 0
    tf = _largest_divisor(f, (1024, 512, 256, 128))
    n_tiles = n_rows // tm
    grid_spec = pltpu.PrefetchScalarGridSpec(
        num_scalar_prefetch=3,
        grid=(n_tiles,),
        in_specs=[
            pl.BlockSpec((tm, d), lambda i, te, nu, nb: (jnp.minimum(i, nu[0] - 1), 0)),
            pl.BlockSpec(memory_space=pl.ANY),
            pl.BlockSpec(memory_space=pl.ANY),
            pl.BlockSpec((None, 1, 2 * f), lambda i, te, nu, nb: (te[i], 0, 0)),
            pl.BlockSpec((None, 1, d), lambda i, te, nu, nb: (te[i], 0, 0)),
        ],
        out_specs=pl.BlockSpec((tm, d), lambda i, te, nu, nb: (i, 0)),
        scratch_shapes=[
            pltpu.VMEM((d, 2 * f), BF16),
            pltpu.VMEM((f, d), BF16),
            pltpu.VMEM((tm, d), BF16),
            pltpu.VMEM((tm, f), BF16),
            pltpu.VMEM((WEIGHT_STAGE_SLOTS, WEIGHT_STAGE_ROWS, 2 * f), F32),
            pltpu.SemaphoreType.DMA((WEIGHT_STAGE_SLOTS,)),
        ],
    )
    return pl.pallas_call(
        functools.partial(_expert_kernel, tf=tf),
        grid_spec=grid_spec,
        out_shape=jax.ShapeDtypeStruct((n_rows, d), F32),
        compiler_params=_cparams("arbitrary"),
        name="experts",
    )(tile_e, n_used, tile_blocks, x_sorted, w_gu, w_d, b_gu, b_d)


def _combine_kernel(dest_ref, dest_next_ref, y_ref, x1_ref, route_ref, nw_ref, o_ref, buf, sem,
                    *, final_norm):
    i = pl.program_id(0)
    n = pl.num_programs(0)
    tt = x1_ref.shape[0]
    slot = i % 2

    def gather(idx_ref, dst_slot):
        def issue(t, carry):
            for k in range(TOP_K):
                pltpu.make_async_copy(y_ref.at[pl.ds(idx_ref[t * TOP_K + k], 1)],
                                      buf.at[dst_slot, pl.ds(k * tt + t, 1)],
                                      sem.at[dst_slot]).start(priority=k % 2)
            return carry

        lax.fori_loop(0, tt, issue, 0)

    @pl.when(i == 0)
    def _():
        gather(dest_ref, slot)

    @pl.when(i + 1 < n)
    def _():
        gather(dest_next_ref, 1 - slot)

    pltpu.make_async_copy(y_ref.at[pl.ds(0, TOP_K * tt)], buf.at[slot], sem.at[slot]).wait()

    route = route_ref[...]
    x = x1_ref[...]
    for k in range(TOP_K):
        gate = route[:, 2 * TOP_K + k:2 * TOP_K + k + 1]
        x = x + gate * buf[slot, k * tt:(k + 1) * tt, :]
    if final_norm:
        x = x * lax.rsqrt(jnp.mean(x * x, axis=-1, keepdims=True) + EPS) * nw_ref[...]
    o_ref[...] = x


def _combine(dest_flat, y_sorted, x1, route, norm_w, final_norm):
    t, d = x1.shape
    tt = _largest_divisor(t, (512, 256))
    n = t // tt
    return pl.pallas_call(
        functools.partial(_combine_kernel, final_norm=final_norm),
        grid=(n,),
        in_specs=[
            pl.BlockSpec((tt * TOP_K,), lambda i: (i,), memory_space=pltpu.SMEM),
            pl.BlockSpec((tt * TOP_K,), lambda i: (jnp.minimum(i + 1, n - 1),), memory_space=pltpu.SMEM),
            pl.BlockSpec(memory_space=pl.ANY),
            pl.BlockSpec((tt, d), lambda i: (i, 0)),
            pl.BlockSpec((tt, LANES), lambda i: (i, 0)),
            pl.BlockSpec((1, d), lambda i: (0, 0)),
        ],
        out_specs=pl.BlockSpec((tt, d), lambda i: (i, 0)),
        out_shape=jax.ShapeDtypeStruct((t, d), F32),
        scratch_shapes=[pltpu.VMEM((2, TOP_K * tt, d), F32), pltpu.SemaphoreType.DMA((2,))],
        compiler_params=_cparams("arbitrary"),
        name="combine",
    )(dest_flat, dest_flat, y_sorted, x1, route, norm_w)


MOE_ROW_TILE = 512


def _layer(x2, bsz, seq, norm_mix_w, w_in, conv_a_w, ssd_conv_w, ssd_conv_b, dt_bias_fw, dt_bias_bw,
           a_log_fw, a_log_bw, d_skip, ssd_norm_w, w_out, norm_ffn_w, w_router, b_router,
           w_gate_up, b_gate_up, w_down, b_down):
    t, d = x2.shape
    dc = conv_a_w.shape[1]
    ds = ssd_norm_w.shape[0]
    dxbc = ssd_conv_w.shape[1]
    heads = dt_bias_fw.shape[0]
    g = SSD_GROUPS
    hpg = heads // g
    n_main = 3 * dc + ds + dxbc
    n_experts = w_router.shape[1]

    w_all = w_in.astype(BF16)
    w_dt = jnp.transpose(w_in[:, n_main:].reshape(d, 2, g, hpg), (0, 2, 1, 3)).reshape(d, g, 2 * hpg)
    w_dt = jnp.pad(w_dt, ((0, 0), (0, 0), (0, SUBLANES - 2 * hpg))).reshape(d, g * SUBLANES)
    w_dt = jnp.pad(w_dt, ((0, 0), (0, LANES - g * SUBLANES))).astype(BF16)
    proj, dt_rows = _inproj(x2, norm_mix_w.reshape(1, d), w_all, n_main, w_dt)
    proj3 = proj.reshape(bsz, seq, n_main)
    y_a = _conv_a(proj3, conv_a_w, dc)
    xbc_act = _conv_ssd(proj3, ssd_conv_w, ssd_conv_b.reshape(1, dxbc), 3 * dc + ds, dxbc)

    def head_rows(fw, bw):
        both = jnp.concatenate([fw.reshape(g, hpg), bw.reshape(g, hpg)], axis=1)
        return jnp.pad(both, ((0, 0), (0, SUBLANES - 2 * hpg))).reshape(g, SUBLANES, 1)

    dskip = jnp.repeat(d_skip, SSD_HEAD_DIM).reshape(1, ds)
    y_b = _ssd(xbc_act, proj3, dt_rows, head_rows(dt_bias_fw, dt_bias_bw), head_rows(a_log_fw, a_log_bw),
               dskip, ssd_norm_w.reshape(1, ds), ds, 3 * dc)

    w_out_b = w_out.astype(BF16)
    w_r = jnp.pad(w_router, ((0, 0), (0, LANES - n_experts)))
    b_r = jnp.pad(b_router, (0, LANES - n_experts)).reshape(1, LANES)
    x1, h2, route, counts = _outproj_route(
        y_a.reshape(t, dc), y_b.reshape(t, ds), x2, w_out_b[:dc], w_out_b[dc:],
        norm_ffn_w.reshape(1, d), w_r, b_r, n_experts)

    tm = MOE_ROW_TILE
    n_slots = t * TOP_K
    n_tiles = n_slots // tm + n_experts
    cnt = counts[0, :n_experts].astype(jnp.int32)
    padded = (cnt + tm - 1) // tm * tm
    pends = jnp.cumsum(padded)
    pstarts = pends - padded
    e_idx = route[:, :TOP_K].astype(jnp.int32)
    rank = route[:, TOP_K:2 * TOP_K].astype(jnp.int32)
    experts = jnp.arange(n_experts, dtype=jnp.int32)
    dest = (jnp.sum((e_idx[..., None] == experts) * pstarts, axis=-1) + rank).reshape(n_slots)
    n_used = (pends[-1] // tm).astype(jnp.int32)
    tile_ids = jnp.arange(n_tiles, dtype=jnp.int32)
    tile_row0 = jnp.minimum(tile_ids, n_used - 1) * tm
    tile_e = jnp.minimum(jnp.sum((pends[None, :] <= tile_row0[:, None]).astype(jnp.int32), axis=1),
                         n_experts - 1)
    seg_end = jnp.sum((tile_e[:, None] == experts) * (pstarts + cnt), axis=-1)
    tile_valid = jnp.clip(seg_end - tile_row0, 1, tm)
    tile_blocks = ((tile_valid + MOE_ROW_BLOCK - 1) // MOE_ROW_BLOCK).astype(jnp.int32)
    partial_last = (pends - tm) * ((cnt % tm) != 0) - (cnt % tm == 0)
    zflag = (jnp.any(partial_last[None, :] == (tile_ids * tm)[:, None], axis=1)
             | (tile_ids >= n_used)).astype(jnp.int32)

    x_sorted = _dispatch(zflag, dest, h2, n_tiles * tm, tm)
    f = w_down.shape[1]
    y_sorted = _experts(tile_e, n_used.reshape(1), tile_blocks, x_sorted, w_gate_up,
                        b_gate_up.reshape(n_experts, 1, 2 * f), w_down,
                        b_down.reshape(n_experts, 1, d), tm)
    return dest, y_sorted, x1, route


def kernel(x, norm_mix_w, w_in, conv_a_w, ssd_conv_w, ssd_conv_b, dt_bias_fw, dt_bias_bw, a_log_fw,
           a_log_bw, d_skip, ssd_norm_w, w_out, norm_ffn_w, w_router, b_router, w_gate_up, b_gate_up,
           w_down, b_down, norm_final_w):
    bsz, seq, d = x.shape
    depth = w_in.shape[0]
    x2 = x.reshape(bsz * seq, d)
    for layer in range(depth):
        dest, y_sorted, x1, route = _layer(
            x2, bsz, seq, norm_mix_w[layer], w_in[layer], conv_a_w[layer], ssd_conv_w[layer],
            ssd_conv_b[layer], dt_bias_fw[layer], dt_bias_bw[layer], a_log_fw[layer], a_log_bw[layer],
            d_skip[layer], ssd_norm_w[layer], w_out[layer], norm_ffn_w[layer], w_router[layer],
            b_router[layer], w_gate_up[layer], b_gate_up[layer], w_down[layer], b_down[layer])
        x2 = _combine(dest, y_sorted, x1, route, norm_final_w.reshape(1, d), layer == depth - 1)
    return x2.reshape(bsz, seq, d)
```

```python
import functools

import jax
import jax.numpy as jnp
from jax import lax
from jax.experimental import pallas as pl
from jax.experimental.pallas import tpu as pltpu

F32 = jnp.float32
BF16 = jnp.bfloat16

EPS = 1e-5
SSD_HEAD_DIM = 64
SSD_GROUPS = 4
SSD_STATE = 128
SSD_CHUNK = 128
TOP_K = 4
SWIGLU_LIMIT = 7.0
SWIGLU_ALPHA = 1.702

LANES = 128
SUBLANES = 8
VMEM_LIMIT_BYTES = 56 * 1024 * 1024


def _largest_divisor(n, candidates):
    for c in candidates:
        if n % c == 0:
            return c
    raise ValueError(f"no tile in {candidates} divides {n}")


def _cparams(*sem):
    return pltpu.CompilerParams(dimension_semantics=tuple(sem), vmem_limit_bytes=VMEM_LIMIT_BYTES)


def _inproj_kernel(x_ref, nw_ref, w_ref, wdt_ref, o_ref, dt_ref, h_scr):
    @pl.when(pl.program_id(1) == 0)
    def _():
        x = x_ref[...]
        ms = jnp.mean(x * x, axis=-1, keepdims=True)
        h = (x * lax.rsqrt(ms + EPS) * nw_ref[...]).astype(BF16)
        h_scr[...] = h
        dt_ref[...] = jnp.dot(h, wdt_ref[...], preferred_element_type=F32).T

    o_ref[...] = jnp.dot(h_scr[...], w_ref[...], preferred_element_type=F32).astype(o_ref.dtype)


def _inproj(x2, norm_w, w_all, n, w_dt):
    t, d = x2.shape
    tm = _largest_divisor(t, (1024, 512, 256, 128))
    tn = _largest_divisor(n, (2048, 1536, 1024, 512, 256, 128))
    return pl.pallas_call(
        _inproj_kernel,
        grid=(t // tm, n // tn),
        in_specs=[
            pl.BlockSpec((tm, d), lambda i, j: (i, 0)),
            pl.BlockSpec((1, d), lambda i, j: (0, 0)),
            pl.BlockSpec((d, tn), lambda i, j: (0, j)),
            pl.BlockSpec((d, LANES), lambda i, j: (0, 0)),
        ],
        out_specs=[
            pl.BlockSpec((tm, tn), lambda i, j: (i, j)),
            pl.BlockSpec((LANES, tm), lambda i, j: (0, i)),
        ],
        out_shape=[
            jax.ShapeDtypeStruct((t, n), BF16),
            jax.ShapeDtypeStruct((LANES, t), F32),
        ],
        scratch_shapes=[pltpu.VMEM((tm, d), BF16)],
        compiler_params=_cparams("parallel", "arbitrary"),
        name="inproj",
    )(x2, norm_w, w_all, w_dt)


CONV_CHUNK = 128
CONV_HALO = 16


def _shift_conv(read, s, dtype, w_ref, emit):
    width = w_ref.shape[0]
    half = width // 2
    win_rows = CONV_CHUNK + 2 * CONV_HALO
    assert half <= CONV_HALO and s % CONV_CHUNK == 0 and s >= win_rows

    r = lax.broadcasted_iota(jnp.int32, (CONV_CHUNK, win_rows), 0)
    j = lax.broadcasted_iota(jnp.int32, (CONV_CHUNK, win_rows), 1)
    shift = {}

    def shift_matrix(lead, off):
        if (lead, off) not in shift:
            shift[(lead, off)] = jnp.where(j == r + (lead + off), 1.0, 0.0).astype(dtype)
        return shift[(lead, off)]

    for c in range(s // CONV_CHUNK):
        r0 = c * CONV_CHUNK
        w0 = min(max(r0 - CONV_HALO, 0), s - win_rows)
        win = read(w0, w0 + win_rows)
        acc = read(r0, r0 + CONV_CHUNK).astype(F32) * w_ref[half:half + 1, :]
        for k in range(width):
            off = k - half
            if off != 0:
                acc = acc + jnp.dot(shift_matrix(r0 - w0, off), win,
                                    preferred_element_type=F32) * w_ref[k:k + 1, :]
        emit(r0, acc)


def _convs_kernel(gb_ref, gc_ref, u_ref, wa_ref, x_ref, ws_ref, bs_ref, ya_ref, xo_ref, v_scr, *, nb_a):
    s = x_ref.shape[1]
    j = pl.program_id(1)

    @pl.when(j < nb_a)
    def _():
        v_scr[...] = (gc_ref[0].astype(F32) * u_ref[0].astype(F32)).astype(v_scr.dtype)

        def emit(r0, acc):
            rows = slice(r0, r0 + CONV_CHUNK)
            ya_ref[0, rows, :] = (gb_ref[0, rows, :].astype(F32) * acc).astype(ya_ref.dtype)

        _shift_conv(lambda lo, hi: v_scr[lo:hi, :], s, v_scr.dtype, wa_ref, emit)

    @pl.when(j >= nb_a)
    def _():
        def emit(r0, acc):
            y = acc + bs_ref[...]
            xo_ref[0, r0:r0 + CONV_CHUNK, :] = (y * jax.nn.sigmoid(y)).astype(xo_ref.dtype)

        _shift_conv(lambda lo, hi: x_ref[0, lo:hi, :], s, x_ref.dtype, ws_ref, emit)


def _convs(proj3, conv_a_w, dc, ssd_w, ssd_b, ssd_col0, dxbc):
    b, s, _ = proj3.shape
    tc = _largest_divisor(dc, (512, 256, 128))
    assert dxbc % tc == 0 and ssd_col0 % tc == 0
    nb_a = dc // tc
    nb_s = dxbc // tc
    off_s = ssd_col0 // tc
    ja = lambda j: jnp.minimum(j, nb_a - 1)
    js = lambda j: jnp.maximum(j - nb_a, 0)
    a_blk = lambda off: pl.BlockSpec((1, s, tc), lambda i, j: (i, 0, off + ja(j)))
    return pl.pallas_call(
        functools.partial(_convs_kernel, nb_a=nb_a),
        grid=(b, nb_a + nb_s),
        in_specs=[a_blk(0), a_blk(nb_a), a_blk(2 * nb_a),
                  pl.BlockSpec((conv_a_w.shape[0], tc), lambda i, j: (0, ja(j))),
                  pl.BlockSpec((1, s, tc), lambda i, j: (i, 0, off_s + js(j))),
                  pl.BlockSpec((ssd_w.shape[0], tc), lambda i, j: (0, js(j))),
                  pl.BlockSpec((1, tc), lambda i, j: (0, js(j)))],
        out_specs=[pl.BlockSpec((1, s, tc), lambda i, j: (i, 0, ja(j))),
                   pl.BlockSpec((1, s, tc), lambda i, j: (i, 0, js(j)))],
        out_shape=[jax.ShapeDtypeStruct((b, s, dc), BF16),
                   jax.ShapeDtypeStruct((b, s, dxbc), BF16)],
        scratch_shapes=[pltpu.VMEM((s, tc), BF16)],
        compiler_params=_cparams("parallel", "arbitrary"),
        name="convs",
    )(proj3, proj3, proj3, conv_a_w, proj3, ssd_w, ssd_b)


def _split_rows(v, passes):
    parts = []
    rem = v
    for _ in range(passes):
        term = rem.astype(BF16).astype(F32)
        parts.append(term)
        rem = rem - term
    while len(parts) % 2:
        parts.append(jnp.zeros_like(v))
    return jnp.concatenate(parts, axis=0).astype(BF16)


def _expander(n_rows, lanes_per_row, n_cols):
    r = lax.broadcasted_iota(jnp.int32, (n_rows, n_cols), 0) & (SUBLANES - 1)
    c = lax.broadcasted_iota(jnp.int32, (n_rows, n_cols), 1)
    lo = r * lanes_per_row
    return jnp.where((c >= lo) & (c < lo + lanes_per_row), 1.0, 0.0).astype(BF16)


def _expand(rows_bf16, expander):
    return lax.dot_general(rows_bf16, expander, (((0,), (0,)), ((), ())), preferred_element_type=F32)


def _ssd_kernel(xs_ref, b_ref, c_ref, z_ref, dt_ref, bias_ref, alog_ref, dskip_ref, nw_ref, o_ref,
                a_scr, dt_scr, qx_scr, ex_scr, acol_scr, st_scr, *, hpg):
    s = xs_ref.shape[1]
    gw = xs_ref.shape[2]
    nc = s // SSD_CHUNK
    L = SSD_CHUNK
    hd = SSD_HEAD_DIM

    raw = dt_ref[...] + bias_ref[0]
    dt = jnp.maximum(raw, 0.0) + jnp.log1p(jnp.exp(-jnp.abs(raw)))
    dta = dt * (-jnp.exp(alog_ref[0]))
    pos = lax.broadcasted_iota(jnp.int32, (SUBLANES, s), 1) & (L - 1)
    head_row = lax.broadcasted_iota(jnp.int32, (SUBLANES, s), 0)
    pre = dta
    suf = dta
    sh = 1
    while sh < L:
        pre = pre + jnp.where(pos >= sh, pltpu.roll(pre, sh, 1), 0.0)
        suf = suf + jnp.where(pos < L - sh, pltpu.roll(suf, s - sh, 1), 0.0)
        sh *= 2
    acum = jnp.where(head_row < hpg, pre, suf)
    tot = pre + suf - dta
    for c in range(nc):
        lanes = slice(c * L, (c + 1) * L)
        a_scr[c] = acum[:, lanes]
        dt_scr[c] = dt[:, lanes]

    exp_head = _expander(2 * SUBLANES, hd, 2 * gw)
    exp_col = _expander(4 * SUBLANES, L, 2 * hpg * L)
    qx_scr[...] = _expand(_split_rows(dt * jnp.exp(tot - acum), 2), exp_head)
    ex_scr[...] = _expand(_split_rows(jnp.exp(acum), 2), exp_head)
    acol_scr[...] = _expand(_split_rows(acum, 3), exp_col)

    def phase1(i, run_b):
        c = nc - 1 - i
        row0 = pl.multiple_of(c * L, L)
        rows = pl.ds(row0, L)
        x = xs_ref[0, rows, :].astype(F32)
        qx = qx_scr[rows, :]
        xw = jnp.concatenate([x * qx[:, 0:gw], x * qx[:, gw:2 * gw]], axis=1).astype(BF16)
        contrib = lax.dot_general(b_ref[0, rows, :], xw, (((0,), (0,)), ((), ())),
                                  preferred_element_type=F32)
        st_scr[c, :, 0:gw] = contrib[:, 0:gw]
        st_scr[c, :, gw:2 * gw] = run_b
        dec_b = ex_scr[pl.ds(row0, 1), gw:2 * gw]
        return run_b * dec_b + contrib[:, gw:2 * gw]

    lax.fori_loop(0, nc, phase1, jnp.zeros((SSD_STATE, gw), F32), unroll=8)

    li = lax.broadcasted_iota(jnp.int32, (L, L), 0)
    si = lax.broadcasted_iota(jnp.int32, (L, L), 1)
    causal = li >= si
    below = li > si
    above = si > li
    lane_l = lax.broadcasted_iota(jnp.int32, (L, LANES), 1)

    def phase2(c, run_f):
        rows = pl.ds(pl.multiple_of(c * L, L), L)
        xb = xs_ref[0, rows, :]
        cm = c_ref[0, rows, :]
        cb = lax.dot_general(cm, b_ref[0, rows, :], (((1,), (1,)), ((), ())), preferred_element_type=F32)
        at = a_scr[c]
        dtt = dt_scr[c]
        acol = acol_scr[rows, :]
        ys = []
        for p in range(hpg // 2):
            xpair = xb[:, p * LANES:(p + 1) * LANES]
            y_pair = None
            for q in range(2):
                kf = 2 * p + q
                kb = hpg + kf
                seg = jnp.where(causal, acol[:, kf * L:(kf + 1) * L] - at[kf:kf + 1, :],
                                acol[:, kb * L:(kb + 1) * L] - at[kb:kb + 1, :])
                dt_f = dtt[kf:kf + 1, :]
                dt_b = dtt[kb:kb + 1, :]
                w = jnp.where(below, dt_f, jnp.where(above, dt_b, dt_f + dt_b))
                m = (cb * jnp.exp(seg) * w).astype(BF16)
                in_head = (lane_l < hd) if q == 0 else (lane_l >= hd)
                part = jnp.dot(m, jnp.where(in_head, xpair, jnp.zeros_like(xpair)),
                               preferred_element_type=F32)
                y_pair = part if y_pair is None else y_pair + part
            ys.append(y_pair)
        y = ys[0] if len(ys) == 1 else jnp.concatenate(ys, axis=1)
        ex = ex_scr[rows, :]
        cs_f = jnp.dot(cm, run_f.astype(BF16), preferred_element_type=F32)
        cs_b = jnp.dot(cm, st_scr[c, :, gw:2 * gw].astype(BF16), preferred_element_type=F32)
        y = y + cs_f * ex[:, 0:gw] + cs_b * ex[:, gw:2 * gw]
        y = y + xb.astype(F32) * dskip_ref[...]
        z = z_ref[0, rows, :].astype(F32)
        g = y * (z * jax.nn.sigmoid(z))
        g = g * lax.rsqrt(jnp.mean(g * g, axis=-1, keepdims=True) + EPS)
        o_ref[0, rows, :] = (g * nw_ref[...]).astype(o_ref.dtype)
        dec_f = ex[L - 1:L, 0:gw]
        return run_f * dec_f + st_scr[c, :, 0:gw]

    lax.fori_loop(0, nc, phase2, jnp.zeros((SSD_STATE, gw), F32), unroll=8)


def _ssd(xbc_act, proj3, dt_rows, bias_col, alog_col, dskip, norm_w, ds, z_col0):
    b, s, _ = xbc_act.shape
    g = SSD_GROUPS
    gw = ds // g
    hpg = gw // SSD_HEAD_DIM
    nc = s // SSD_CHUNK
    n = SSD_STATE
    L = SSD_CHUNK
    assert z_col0 % gw == 0 and ds % n == 0 and hpg % 2 == 0 and 2 * hpg <= SUBLANES
    kernel = functools.partial(_ssd_kernel, hpg=hpg)
    return pl.pallas_call(
        kernel,
        grid=(b, g),
        in_specs=[
            pl.BlockSpec((1, s, gw), lambda i, j: (i, 0, j)),
            pl.BlockSpec((1, s, n), lambda i, j: (i, 0, ds // n + j)),
            pl.BlockSpec((1, s, n), lambda i, j: (i, 0, ds // n + g + j)),
            pl.BlockSpec((1, s, gw), lambda i, j: (i, 0, z_col0 // gw + j)),
            pl.BlockSpec((SUBLANES, s), lambda i, j: (j, i)),
            pl.BlockSpec((1, SUBLANES, 1), lambda i, j: (j, 0, 0)),
            pl.BlockSpec((1, SUBLANES, 1), lambda i, j: (j, 0, 0)),
            pl.BlockSpec((1, gw), lambda i, j: (0, j)),
            pl.BlockSpec((1, gw), lambda i, j: (0, j)),
        ],
        out_specs=pl.BlockSpec((1, s, gw), lambda i, j: (i, 0, j)),
        out_shape=jax.ShapeDtypeStruct((b, s, ds), BF16),
        scratch_shapes=[
            pltpu.VMEM((nc, SUBLANES, L), F32),
            pltpu.VMEM((nc, SUBLANES, L), F32),
            pltpu.VMEM((s, 2 * gw), F32),
            pltpu.VMEM((s, 2 * gw), F32),
            pltpu.VMEM((s, 2 * hpg * L), F32),
            pltpu.VMEM((nc, n, 2 * gw), F32),
        ],
        compiler_params=_cparams("parallel", "parallel"),
        name="ssd",
    )(xbc_act, xbc_act, xbc_act, proj3, dt_rows, bias_col, alog_col, dskip, norm_w)


def _split_bf16(a):
    hi = a.astype(BF16)
    lo = (a - hi.astype(F32)).astype(BF16)
    return hi, lo


def _outproj_kernel(ya_ref, yb_ref, x_ref, wa_ref, wb_ref, nw_ref, wr_ref, br_ref,
                    x1_ref, h_ref, route_ref, cnt_ref, carry_scr, *, n_experts):
    i = pl.program_id(0)
    tm = x_ref.shape[0]

    @pl.when(i == 0)
    def _():
        carry_scr[...] = jnp.zeros_like(carry_scr)

    x1 = x_ref[...] + jnp.dot(ya_ref[...], wa_ref[...], preferred_element_type=F32) \
        + jnp.dot(yb_ref[...], wb_ref[...], preferred_element_type=F32)
    x1_ref[...] = x1
    h = x1 * lax.rsqrt(jnp.mean(x1 * x1, axis=-1, keepdims=True) + EPS) * nw_ref[...]
    h_ref[...] = h

    h_hi, h_lo = _split_bf16(h)
    w_hi, w_lo = _split_bf16(wr_ref[...])
    logits = (jnp.dot(h_hi, w_hi, preferred_element_type=F32)
              + jnp.dot(h_hi, w_lo, preferred_element_type=F32)
              + jnp.dot(h_lo, w_hi, preferred_element_type=F32)) + br_ref[...]

    lane = lax.broadcasted_iota(jnp.int32, (tm, LANES), 1)
    neg = jnp.finfo(F32).min
    work = jnp.where(lane < n_experts, logits, neg)
    tops, idxs, sels = [], [], []
    for _k in range(TOP_K):
        m = jnp.max(work, axis=-1, keepdims=True)
        idx = jnp.min(jnp.where(work == m, lane, LANES), axis=-1, keepdims=True)
        sel = lane == idx
        work = jnp.where(sel, neg, work)
        tops.append(m)
        idxs.append(idx)
        sels.append(sel)
    exps = [jnp.exp(t - tops[0]) for t in tops]
    denom = exps[0]
    for e in exps[1:]:
        denom = denom + e
    inv = 1.0 / denom

    onehot = jnp.zeros((tm, LANES), F32)
    for sel in sels:
        onehot = onehot + jnp.where(sel, 1.0, 0.0)
    ri = lax.broadcasted_iota(jnp.int32, (tm, tm), 0)
    ci = lax.broadcasted_iota(jnp.int32, (tm, tm), 1)
    tri = jnp.where(ri > ci, 1.0, 0.0).astype(BF16)
    carry = carry_scr[0:1, :]
    prefix = jnp.dot(tri, onehot.astype(BF16), preferred_element_type=F32) + carry
    new_carry = carry + jnp.sum(onehot, axis=0, keepdims=True)
    carry_scr[...] = jnp.broadcast_to(new_carry, carry_scr.shape)
    cnt_ref[...] = jnp.broadcast_to(new_carry, cnt_ref.shape)

    route = jnp.zeros((tm, LANES), F32)
    for k in range(TOP_K):
        rank = jnp.sum(jnp.where(sels[k], prefix, 0.0), axis=-1, keepdims=True)
        route = jnp.where(lane == k, idxs[k].astype(F32), route)
        route = jnp.where(lane == TOP_K + k, rank, route)
        route = jnp.where(lane == 2 * TOP_K + k, exps[k] * inv, route)
    route_ref[...] = route


def _outproj_route(ya, yb, x2, wa, wb, norm_w, w_router, b_router, n_experts):
    t, d = x2.shape
    dc = ya.shape[1]
    ds = yb.shape[1]
    tm = _largest_divisor(t, (512, 256, 128))
    kernel = functools.partial(_outproj_kernel, n_experts=n_experts)
    const = lambda shape: pl.BlockSpec(shape, lambda i: (0, 0))
    return pl.pallas_call(
        kernel,
        grid=(t // tm,),
        in_specs=[
            pl.BlockSpec((tm, dc), lambda i: (i, 0)),
            pl.BlockSpec((tm, ds), lambda i: (i, 0)),
            pl.BlockSpec((tm, d), lambda i: (i, 0)),
            const((dc, d)), const((ds, d)), const((1, d)), const((d, LANES)), const((1, LANES)),
        ],
        out_specs=[
            pl.BlockSpec((tm, d), lambda i: (i, 0)),
            pl.BlockSpec((tm, d), lambda i: (i, 0)),
            pl.BlockSpec((tm, LANES), lambda i: (i, 0)),
            pl.BlockSpec((8, LANES), lambda i: (0, 0)),
        ],
        out_shape=[
            jax.ShapeDtypeStruct((t, d), F32),
            jax.ShapeDtypeStruct((t, d), F32),
            jax.ShapeDtypeStruct((t, LANES), F32),
            jax.ShapeDtypeStruct((8, LANES), F32),
        ],
        scratch_shapes=[pltpu.VMEM((8, LANES), F32)],
        compiler_params=_cparams("arbitrary"),
        name="outproj_route",
    )(ya, yb, x2, wa, wb, norm_w, w_router, b_router)


ZERO_ROWS = 256


def _dispatch_kernel(zflag_ref, dest_ref, h_ref, o_ref, zbuf, sem, zsem, *, tm):
    tt = h_ref.shape[0]
    n_tiles = o_ref.shape[0] // tm
    per_tile = tm // ZERO_ROWS

    def zero_copy(i, p):
        return pltpu.make_async_copy(zbuf, o_ref.at[pl.ds(i * tm + p * ZERO_ROWS, ZERO_ROWS)], zsem)

    @pl.when(pl.program_id(0) == 0)
    def _():
        zbuf[...] = jnp.zeros_like(zbuf)

        def start(i, carry):
            @pl.when(zflag_ref[i] != 0)
            def _():
                for p in range(per_tile):
                    zero_copy(i, p).start()
            return carry

        def wait(i, carry):
            @pl.when(zflag_ref[i] != 0)
            def _():
                for p in range(per_tile):
                    zero_copy(i, p).wait()
            return carry

        lax.fori_loop(0, n_tiles, start, 0)
        lax.fori_loop(0, n_tiles, wait, 0)

    def issue(t, carry):
        for k in range(TOP_K):
            pltpu.make_async_copy(h_ref.at[pl.ds(t, 1)], o_ref.at[pl.ds(dest_ref[t * TOP_K + k], 1)],
                                  sem).start(priority=k % 2)
        return carry

    lax.fori_loop(0, tt, issue, 0, unroll=2)
    for _k in range(TOP_K):
        pltpu.make_async_copy(h_ref, o_ref.at[pl.ds(0, tt)], sem).wait()


def _dispatch(zflag, dest_flat, h, n_rows, tm):
    t, d = h.shape
    tt = _largest_divisor(t, (1024, 512, 256))
    assert tm % ZERO_ROWS == 0
    grid_spec = pltpu.PrefetchScalarGridSpec(
        num_scalar_prefetch=1,
        grid=(t // tt,),
        in_specs=[
            pl.BlockSpec((tt * TOP_K,), lambda i, zf: (i,), memory_space=pltpu.SMEM),
            pl.BlockSpec((tt, d), lambda i, zf: (i, 0)),
        ],
        out_specs=pl.BlockSpec(memory_space=pl.ANY),
        scratch_shapes=[pltpu.VMEM((ZERO_ROWS, d), h.dtype), pltpu.SemaphoreType.DMA(()),
                        pltpu.SemaphoreType.DMA(())],
    )
    return pl.pallas_call(
        functools.partial(_dispatch_kernel, tm=tm),
        grid_spec=grid_spec,
        out_shape=jax.ShapeDtypeStruct((n_rows, d), h.dtype),
        compiler_params=_cparams("arbitrary"),
        name="dispatch",
    )(zflag, dest_flat, h)


def _ring_copies(src, stage, sem, k, slot):
    _, rows, width = stage.shape
    w = src.shape[1]
    per = width // w
    return [pltpu.make_async_copy(src.at[pl.ds((k * per + p) * rows, rows)],
                                  stage.at[slot, :, p * w:(p + 1) * w], sem.at[slot])
            for p in range(per)]


def _ring_fill(src, stage, sem):
    for slot in range(stage.shape[0]):
        for c in _ring_copies(src, stage, sem, slot, slot):
            c.start()


def _load_cast(src, dst, stage, sem, ring_filled):
    n_slots, rows, width = stage.shape
    per = width // src.shape[1]
    w = src.shape[1]
    n = dst.shape[0] // (rows * per)
    copies = functools.partial(_ring_copies, src, stage, sem)

    @pl.when(jnp.logical_not(ring_filled))
    def _():
        _ring_fill(src, stage, sem)

    def body(kk, carry):
        for slot in range(n_slots):
            k = n_slots * kk + slot
            for c in copies(k, slot):
                c.wait()
            for p in range(per):
                row0 = pl.multiple_of((k * per + p) * rows, rows)
                dst[pl.ds(row0, rows), :] = stage[slot, :, p * w:(p + 1) * w].astype(BF16)

            @pl.when(k + n_slots < n)
            def _():
                for c in copies(k + n_slots, slot):
                    c.start()
        return carry

    lax.fori_loop(0, n // n_slots, body, 0)


def _expert_kernel(te_ref, nu_ref, nb_ref, x_ref, wgu_hbm, wd_hbm, bgu_ref, bd_ref, o_ref,
                   wgu_buf, wd_buf, xb_scr, act_scr, stage, sem, *, tf):
    i = pl.program_id(0)
    tm = x_ref.shape[0]
    f = wd_buf.shape[0]
    e = te_ref[i]
    active = i < nu_ref[0]

    @pl.when(active & ((i == 0) | (e != te_ref[jnp.maximum(i - 1, 0)])))
    def _():
        _load_cast(wgu_hbm.at[e], wgu_buf, stage, sem, ring_filled=i > 0)
        _load_cast(wd_hbm.at[e], wd_buf, stage, sem, ring_filled=False)

    e_next = te_ref[jnp.minimum(i + 1, pl.num_programs(0) - 1)]

    @pl.when((i + 1 < nu_ref[0]) & (e_next != e))
    def _():
        _ring_fill(wgu_hbm.at[e_next], stage, sem)

    def mlp(rows):
        xb_scr[0:rows, :] = x_ref[0:rows, :].astype(BF16)
        xb = xb_scr[0:rows, :]
        for j in range(f // tf):
            gcols = slice(j * tf, (j + 1) * tf)
            lcols = slice(f + j * tf, f + (j + 1) * tf)
            glu = jnp.dot(xb, wgu_buf[:, gcols], preferred_element_type=F32) + bgu_ref[:, gcols]
            lin = jnp.dot(xb, wgu_buf[:, lcols], preferred_element_type=F32) + bgu_ref[:, lcols]
            glu = jnp.minimum(glu, SWIGLU_LIMIT)
            lin = jnp.clip(lin, -SWIGLU_LIMIT, SWIGLU_LIMIT)
            act_scr[0:rows, gcols] = (glu * jax.nn.sigmoid(SWIGLU_ALPHA * glu) * (lin + 1.0)).astype(BF16)
        o_ref[0:rows, :] = (jnp.dot(act_scr[0:rows, :], wd_buf[...], preferred_element_type=F32)
                            + bd_ref[...]).astype(o_ref.dtype)
        if rows < tm:
            o_ref[rows:tm, :] = jnp.zeros((tm - rows, o_ref.shape[1]), o_ref.dtype)

    for blocks in range(1, tm // MOE_ROW_BLOCK + 1):
        pl.when(active & (nb_ref[i] == blocks))(functools.partial(mlp, blocks * MOE_ROW_BLOCK))

    @pl.when(jnp.logical_not(active))
    def _():
        o_ref[...] = jnp.zeros_like(o_ref)


WEIGHT_STAGE_ROWS = 64
WEIGHT_STAGE_SLOTS = 8


MOE_ROW_BLOCK = 128


def _experts(tile_e, n_used, tile_blocks, x_sorted, w_gu, b_gu, w_d, b_d, tm):
    n_rows, d = x_sorted.shape
    f = w_d.shape[1]
    chunk = WEIGHT_STAGE_ROWS * WEIGHT_STAGE_SLOTS
    assert (2 * f) % d == 0 and d % chunk == 0 and f % (chunk * (2 * f // d)) == 0
    assert tm % MOE_ROW_BLOCK == 0
    tf = _largest_divisor(f, (1024, 512, 256, 128))
    n_tiles = n_rows // tm
    grid_spec = pltpu.PrefetchScalarGridSpec(
        num_scalar_prefetch=3,
        grid=(n_tiles,),
        in_specs=[
            pl.BlockSpec((tm, d), lambda i, te, nu, nb: (jnp.minimum(i, nu[0] - 1), 0)),
            pl.BlockSpec(memory_space=pl.ANY),
            pl.BlockSpec(memory_space=pl.ANY),
            pl.BlockSpec((None, 1, 2 * f), lambda i, te, nu, nb: (te[i], 0, 0)),
            pl.BlockSpec((None, 1, d), lambda i, te, nu, nb: (te[i], 0, 0)),
        ],
        out_specs=pl.BlockSpec((tm, d), lambda i, te, nu, nb: (i, 0)),
        scratch_shapes=[
            pltpu.VMEM((d, 2 * f), BF16),
            pltpu.VMEM((f, d), BF16),
            pltpu.VMEM((tm, d), BF16),
            pltpu.VMEM((tm, f), BF16),
            pltpu.VMEM((WEIGHT_STAGE_SLOTS, WEIGHT_STAGE_ROWS, 2 * f), F32),
            pltpu.SemaphoreType.DMA((WEIGHT_STAGE_SLOTS,)),
        ],
    )
    return pl.pallas_call(
        functools.partial(_expert_kernel, tf=tf),
        grid_spec=grid_spec,
        out_shape=jax.ShapeDtypeStruct((n_rows, d), F32),
        compiler_params=_cparams("arbitrary"),
        name="experts",
    )(tile_e, n_used, tile_blocks, x_sorted, w_gu, w_d, b_gu, b_d)


def _combine_kernel(dest_ref, dest_next_ref, y_ref, x1_ref, route_ref, nw_ref, o_ref, buf, sem,
                    *, final_norm):
    i = pl.program_id(0)
    n = pl.num_programs(0)
    tt = x1_ref.shape[0]
    slot = i % 2

    def gather(idx_ref, dst_slot):
        def issue(t, carry):
            for k in range(TOP_K):
                pltpu.make_async_copy(y_ref.at[pl.ds(idx_ref[t * TOP_K + k], 1)],
                                      buf.at[dst_slot, pl.ds(k * tt + t, 1)],
                                      sem.at[dst_slot]).start(priority=k % 2)
            return carry

        lax.fori_loop(0, tt, issue, 0)

    @pl.when(i == 0)
    def _():
        gather(dest_ref, slot)

    @pl.when(i + 1 < n)
    def _():
        gather(dest_next_ref, 1 - slot)

    pltpu.make_async_copy(y_ref.at[pl.ds(0, TOP_K * tt)], buf.at[slot], sem.at[slot]).wait()

    route = route_ref[...]
    x = x1_ref[...]
    for k in range(TOP_K):
        gate = route[:, 2 * TOP_K + k:2 * TOP_K + k + 1]
        x = x + gate * buf[slot, k * tt:(k + 1) * tt, :]
    if final_norm:
        x = x * lax.rsqrt(jnp.mean(x * x, axis=-1, keepdims=True) + EPS) * nw_ref[...]
    o_ref[...] = x


def _combine(dest_flat, y_sorted, x1, route, norm_w, final_norm):
    t, d = x1.shape
    tt = _largest_divisor(t, (512, 256))
    n = t // tt
    return pl.pallas_call(
        functools.partial(_combine_kernel, final_norm=final_norm),
        grid=(n,),
        in_specs=[
            pl.BlockSpec((tt * TOP_K,), lambda i: (i,), memory_space=pltpu.SMEM),
            pl.BlockSpec((tt * TOP_K,), lambda i: (jnp.minimum(i + 1, n - 1),), memory_space=pltpu.SMEM),
            pl.BlockSpec(memory_space=pl.ANY),
            pl.BlockSpec((tt, d), lambda i: (i, 0)),
            pl.BlockSpec((tt, LANES), lambda i: (i, 0)),
            pl.BlockSpec((1, d), lambda i: (0, 0)),
        ],
        out_specs=pl.BlockSpec((tt, d), lambda i: (i, 0)),
        out_shape=jax.ShapeDtypeStruct((t, d), F32),
        scratch_shapes=[pltpu.VMEM((2, TOP_K * tt, d), F32), pltpu.SemaphoreType.DMA((2,))],
        compiler_params=_cparams("arbitrary"),
        name="combine",
    )(dest_flat, dest_flat, y_sorted, x1, route, norm_w)


MOE_ROW_TILE = 512


def _layer(x2, bsz, seq, norm_mix_w, w_in, conv_a_w, ssd_conv_w, ssd_conv_b, dt_bias_fw, dt_bias_bw,
           a_log_fw, a_log_bw, d_skip, ssd_norm_w, w_out, norm_ffn_w, w_router, b_router,
           w_gate_up, b_gate_up, w_down, b_down):
    t, d = x2.shape
    dc = conv_a_w.shape[1]
    ds = ssd_norm_w.shape[0]
    dxbc = ssd_conv_w.shape[1]
    heads = dt_bias_fw.shape[0]
    g = SSD_GROUPS
    hpg = heads // g
    n_main = 3 * dc + ds + dxbc
    n_experts = w_router.shape[1]

    w_all = w_in.astype(BF16)
    w_dt = jnp.transpose(w_in[:, n_main:].reshape(d, 2, g, hpg), (0, 2, 1, 3)).reshape(d, g, 2 * hpg)
    w_dt = jnp.pad(w_dt, ((0, 0), (0, 0), (0, SUBLANES - 2 * hpg))).reshape(d, g * SUBLANES)
    w_dt = jnp.pad(w_dt, ((0, 0), (0, LANES - g * SUBLANES))).astype(BF16)
    proj, dt_rows = _inproj(x2, norm_mix_w.reshape(1, d), w_all, n_main, w_dt)
    proj3 = proj.reshape(bsz, seq, n_main)
    y_a, xbc_act = _convs(proj3, conv_a_w, dc, ssd_conv_w, ssd_conv_b.reshape(1, dxbc), 3 * dc + ds, dxbc)

    def head_rows(fw, bw):
        both = jnp.concatenate([fw.reshape(g, hpg), bw.reshape(g, hpg)], axis=1)
        return jnp.pad(both, ((0, 0), (0, SUBLANES - 2 * hpg))).reshape(g, SUBLANES, 1)

    dskip = jnp.repeat(d_skip, SSD_HEAD_DIM).reshape(1, ds)
    y_b = _ssd(xbc_act, proj3, dt_rows, head_rows(dt_bias_fw, dt_bias_bw), head_rows(a_log_fw, a_log_bw),
               dskip, ssd_norm_w.reshape(1, ds), ds, 3 * dc)

    w_out_b = w_out.astype(BF16)
    w_r = jnp.pad(w_router, ((0, 0), (0, LANES - n_experts)))
    b_r = jnp.pad(b_router, (0, LANES - n_experts)).reshape(1, LANES)
    x1, h2, route, counts = _outproj_route(
        y_a.reshape(t, dc), y_b.reshape(t, ds), x2, w_out_b[:dc], w_out_b[dc:],
        norm_ffn_w.reshape(1, d), w_r, b_r, n_experts)

    tm = MOE_ROW_TILE
    n_slots = t * TOP_K
    n_tiles = n_slots // tm + n_experts
    cnt = counts[0, :n_experts].astype(jnp.int32)
    padded = (cnt + tm - 1) // tm * tm
    pends = jnp.cumsum(padded)
    pstarts = pends - padded
    e_idx = route[:, :TOP_K].astype(jnp.int32)
    rank = route[:, TOP_K:2 * TOP_K].astype(jnp.int32)
    experts = jnp.arange(n_experts, dtype=jnp.int32)
    dest = (jnp.sum((e_idx[..., None] == experts) * pstarts, axis=-1) + rank).reshape(n_slots)
    n_used = (pends[-1] // tm).astype(jnp.int32)
    tile_ids = jnp.arange(n_tiles, dtype=jnp.int32)
    tile_row0 = jnp.minimum(tile_ids, n_used - 1) * tm
    tile_e = jnp.minimum(jnp.sum((pends[None, :] <= tile_row0[:, None]).astype(jnp.int32), axis=1),
                         n_experts - 1)
    seg_end = jnp.sum((tile_e[:, None] == experts) * (pstarts + cnt), axis=-1)
    tile_valid = jnp.clip(seg_end - tile_row0, 1, tm)
    tile_blocks = ((tile_valid + MOE_ROW_BLOCK - 1) // MOE_ROW_BLOCK).astype(jnp.int32)
    partial_last = (pends - tm) * ((cnt % tm) != 0) - (cnt % tm == 0)
    zflag = (jnp.any(partial_last[None, :] == (tile_ids * tm)[:, None], axis=1)
             | (tile_ids >= n_used)).astype(jnp.int32)

    x_sorted = _dispatch(zflag, dest, h2, n_tiles * tm, tm)
    f = w_down.shape[1]
    y_sorted = _experts(tile_e, n_used.reshape(1), tile_blocks, x_sorted, w_gate_up,
                        b_gate_up.reshape(n_experts, 1, 2 * f), w_down,
                        b_down.reshape(n_experts, 1, d), tm)
    return dest, y_sorted, x1, route


def kernel(x, norm_mix_w, w_in, conv_a_w, ssd_conv_w, ssd_conv_b, dt_bias_fw, dt_bias_bw, a_log_fw,
           a_log_bw, d_skip, ssd_norm_w, w_out, norm_ffn_w, w_router, b_router, w_gate_up, b_gate_up,
           w_down, b_down, norm_final_w):
    bsz, seq, d = x.shape
    depth = w_in.shape[0]
    x2 = x.reshape(bsz * seq, d)
    for layer in range(depth):
        dest, y_sorted, x1, route = _layer(
            x2, bsz, seq, norm_mix_w[layer], w_in[layer], conv_a_w[layer], ssd_conv_w[layer],
            ssd_conv_b[layer], dt_bias_fw[layer], dt_bias_bw[layer], a_log_fw[layer], a_log_bw[layer],
            d_skip[layer], ssd_norm_w[layer], w_out[layer], norm_ffn_w[layer], w_router[layer],
            b_router[layer], w_gate_up[layer], b_gate_up[layer], w_down[layer], b_down[layer])
        x2 = _combine(dest, y_sorted, x1, route, norm_final_w.reshape(1, d), layer == depth - 1)
    return x2.reshape(bsz, seq, d)
```

```python
import functools

import jax
import jax.numpy as jnp
from jax import lax
from jax.experimental import pallas as pl
from jax.experimental.pallas import tpu as pltpu

F32 = jnp.float32
BF16 = jnp.bfloat16

EPS = 1e-5
SSD_HEAD_DIM = 64
SSD_GROUPS = 4
SSD_STATE = 128
SSD_CHUNK = 128
TOP_K = 4
SWIGLU_LIMIT = 7.0
SWIGLU_ALPHA = 1.702

LANES = 128
SUBLANES = 8
VMEM_LIMIT_BYTES = 56 * 1024 * 1024


def _largest_divisor(n, candidates):
    for c in candidates:
        if n % c == 0:
            return c
    raise ValueError(f"no tile in {candidates} divides {n}")


def _cparams(*sem):
    return pltpu.CompilerParams(dimension_semantics=tuple(sem), vmem_limit_bytes=VMEM_LIMIT_BYTES)


def _inproj_kernel(x_ref, nw_ref, w_ref, wdt_ref, o_ref, dt_ref, h_scr):
    @pl.when(pl.program_id(1) == 0)
    def _():
        x = x_ref[...]
        ms = jnp.mean(x * x, axis=-1, keepdims=True)
        h = (x * lax.rsqrt(ms + EPS) * nw_ref[...]).astype(BF16)
        h_scr[...] = h
        dt_ref[...] = jnp.dot(h, wdt_ref[...], preferred_element_type=F32).T

    o_ref[...] = jnp.dot(h_scr[...], w_ref[...], preferred_element_type=F32).astype(o_ref.dtype)


def _inproj(x2, norm_w, w_all, n, w_dt):
    t, d = x2.shape
    tm = _largest_divisor(t, (1024, 512, 256, 128))
    tn = _largest_divisor(n, (2048, 1536, 1024, 512, 256, 128))
    return pl.pallas_call(
        _inproj_kernel,
        grid=(t // tm, n // tn),
        in_specs=[
            pl.BlockSpec((tm, d), lambda i, j: (i, 0)),
            pl.BlockSpec((1, d), lambda i, j: (0, 0)),
            pl.BlockSpec((d, tn), lambda i, j: (0, j)),
            pl.BlockSpec((d, LANES), lambda i, j: (0, 0)),
        ],
        out_specs=[
            pl.BlockSpec((tm, tn), lambda i, j: (i, j)),
            pl.BlockSpec((LANES, tm), lambda i, j: (0, i)),
        ],
        out_shape=[
            jax.ShapeDtypeStruct((t, n), BF16),
            jax.ShapeDtypeStruct((LANES, t), F32),
        ],
        scratch_shapes=[pltpu.VMEM((tm, d), BF16)],
        compiler_params=_cparams("parallel", "arbitrary"),
        name="inproj",
    )(x2, norm_w, w_all, w_dt)


def _centred_conv(v, w_ref):
    s = v.shape[0]
    width = w_ref.shape[0]
    half = width // 2
    edge = 2 * SUBLANES
    assert half <= SUBLANES and s >= 2 * edge

    def taps(u, mask_rows):
        n = u.shape[0]
        row = lax.broadcasted_iota(jnp.int32, u.shape, 0)
        acc = u * w_ref[half:half + 1, :]
        for k in range(width):
            off = k - half
            if off == 0:
                continue
            shifted = pltpu.roll(u, (-off) % n, 0)
            if mask_rows:
                shifted = jnp.where((row + off >= 0) & (row + off < n), shifted, 0.0)
            acc = acc + shifted * w_ref[k:k + 1, :]
        return acc

    body = taps(v, False)
    head = taps(v[0:edge, :], True)[0:SUBLANES, :]
    tail = taps(v[s - edge:s, :], True)[SUBLANES:edge, :]
    return jnp.concatenate([head, body[SUBLANES:s - SUBLANES, :], tail], axis=0)


def _conv_a_kernel(gb_ref, gc_ref, u_ref, w_ref, o_ref):
    v = gc_ref[0].astype(F32) * u_ref[0].astype(F32)
    o_ref[0] = (gb_ref[0].astype(F32) * _centred_conv(v, w_ref)).astype(o_ref.dtype)


def _conv_a(proj3, conv_w, dc):
    b, s, _ = proj3.shape
    tc = _largest_divisor(dc, (512, 256, 128))
    nb = dc // tc
    blk = lambda off: pl.BlockSpec((1, s, tc), lambda i, j: (i, 0, off + j))
    return pl.pallas_call(
        _conv_a_kernel,
        grid=(b, nb),
        in_specs=[blk(0), blk(nb), blk(2 * nb),
                  pl.BlockSpec((conv_w.shape[0], tc), lambda i, j: (0, j))],
        out_specs=pl.BlockSpec((1, s, tc), lambda i, j: (i, 0, j)),
        out_shape=jax.ShapeDtypeStruct((b, s, dc), BF16),
        compiler_params=_cparams("parallel", "parallel"),
        name="conv_a",
    )(proj3, proj3, proj3, conv_w)


CONV_CHUNK = 128
CONV_HALO = 16


def _conv_ssd_kernel(x_ref, w_ref, b_ref, o_ref):
    s = x_ref.shape[1]
    width = w_ref.shape[0]
    half = width // 2
    win_rows = CONV_CHUNK + 2 * CONV_HALO
    assert half <= CONV_HALO and s % CONV_CHUNK == 0 and s >= win_rows

    r = lax.broadcasted_iota(jnp.int32, (CONV_CHUNK, win_rows), 0)
    j = lax.broadcasted_iota(jnp.int32, (CONV_CHUNK, win_rows), 1)
    shift = {}

    def shift_matrix(lead, off):
        if (lead, off) not in shift:
            shift[(lead, off)] = jnp.where(j == r + (lead + off), 1.0, 0.0).astype(x_ref.dtype)
        return shift[(lead, off)]

    for c in range(s // CONV_CHUNK):
        r0 = c * CONV_CHUNK
        w0 = min(max(r0 - CONV_HALO, 0), s - win_rows)
        win = x_ref[0, w0:w0 + win_rows, :]
        acc = x_ref[0, r0:r0 + CONV_CHUNK, :].astype(F32) * w_ref[half:half + 1, :]
        for k in range(width):
            off = k - half
            if off != 0:
                acc = acc + jnp.dot(shift_matrix(r0 - w0, off), win,
                                    preferred_element_type=F32) * w_ref[k:k + 1, :]
        y = acc + b_ref[...]
        o_ref[0, r0:r0 + CONV_CHUNK, :] = (y * jax.nn.sigmoid(y)).astype(o_ref.dtype)


def _conv_ssd(proj3, conv_w, conv_b, col0, dxbc):
    b, s, _ = proj3.shape
    tc = _largest_divisor(dxbc, (512, 256, 128))
    assert col0 % tc == 0
    off = col0 // tc
    return pl.pallas_call(
        _conv_ssd_kernel,
        grid=(b, dxbc // tc),
        in_specs=[pl.BlockSpec((1, s, tc), lambda i, j: (i, 0, off + j)),
                  pl.BlockSpec((conv_w.shape[0], tc), lambda i, j: (0, j)),
                  pl.BlockSpec((1, tc), lambda i, j: (0, j))],
        out_specs=pl.BlockSpec((1, s, tc), lambda i, j: (i, 0, j)),
        out_shape=jax.ShapeDtypeStruct((b, s, dxbc), BF16),
        compiler_params=_cparams("parallel", "parallel"),
        name="conv_ssd",
    )(proj3, conv_w, conv_b)


def _split_rows(v, passes):
    parts = []
    rem = v
    for _ in range(passes):
        term = rem.astype(BF16).astype(F32)
        parts.append(term)
        rem = rem - term
    while len(parts) % 2:
        parts.append(jnp.zeros_like(v))
    return jnp.concatenate(parts, axis=0).astype(BF16)


def _expander(n_rows, lanes_per_row, n_cols):
    r = lax.broadcasted_iota(jnp.int32, (n_rows, n_cols), 0) & (SUBLANES - 1)
    c = lax.broadcasted_iota(jnp.int32, (n_rows, n_cols), 1)
    lo = r * lanes_per_row
    return jnp.where((c >= lo) & (c < lo + lanes_per_row), 1.0, 0.0).astype(BF16)


def _expand(rows_bf16, expander):
    return lax.dot_general(rows_bf16, expander, (((0,), (0,)), ((), ())), preferred_element_type=F32)


def _ssd_kernel(xs_ref, b_ref, c_ref, z_ref, dt_ref, bias_ref, alog_ref, dskip_ref, nw_ref, o_ref,
                a_scr, dt_scr, qx_scr, ex_scr, acol_scr, st_scr, *, hpg):
    s = xs_ref.shape[1]
    gw = xs_ref.shape[2]
    nc = s // SSD_CHUNK
    L = SSD_CHUNK
    hd = SSD_HEAD_DIM

    raw = dt_ref[...] + bias_ref[0]
    dt = jnp.maximum(raw, 0.0) + jnp.log1p(jnp.exp(-jnp.abs(raw)))
    dta = dt * (-jnp.exp(alog_ref[0]))
    pos = lax.broadcasted_iota(jnp.int32, (SUBLANES, s), 1) & (L - 1)
    head_row = lax.broadcasted_iota(jnp.int32, (SUBLANES, s), 0)
    pre = dta
    suf = dta
    sh = 1
    while sh < L:
        pre = pre + jnp.where(pos >= sh, pltpu.roll(pre, sh, 1), 0.0)
        suf = suf + jnp.where(pos < L - sh, pltpu.roll(suf, s - sh, 1), 0.0)
        sh *= 2
    acum = jnp.where(head_row < hpg, pre, suf)
    tot = pre + suf - dta
    for c in range(nc):
        lanes = slice(c * L, (c + 1) * L)
        a_scr[c] = acum[:, lanes]
        dt_scr[c] = dt[:, lanes]

    exp_head = _expander(2 * SUBLANES, hd, 2 * gw)
    exp_col = _expander(4 * SUBLANES, L, 2 * hpg * L)
    qx_scr[...] = _expand(_split_rows(dt * jnp.exp(tot - acum), 2), exp_head)
    ex_scr[...] = _expand(_split_rows(jnp.exp(acum), 2), exp_head)
    acol_scr[...] = _expand(_split_rows(acum, 3), exp_col)

    def phase1(i, run_b):
        c = nc - 1 - i
        row0 = pl.multiple_of(c * L, L)
        rows = pl.ds(row0, L)
        x = xs_ref[0, rows, :].astype(F32)
        qx = qx_scr[rows, :]
        xw = jnp.concatenate([x * qx[:, 0:gw], x * qx[:, gw:2 * gw]], axis=1).astype(BF16)
        contrib = lax.dot_general(b_ref[0, rows, :], xw, (((0,), (0,)), ((), ())),
                                  preferred_element_type=F32)
        st_scr[c, :, 0:gw] = contrib[:, 0:gw]
        st_scr[c, :, gw:2 * gw] = run_b
        dec_b = ex_scr[pl.ds(row0, 1), gw:2 * gw]
        return run_b * dec_b + contrib[:, gw:2 * gw]

    lax.fori_loop(0, nc, phase1, jnp.zeros((SSD_STATE, gw), F32), unroll=8)

    li = lax.broadcasted_iota(jnp.int32, (L, L), 0)
    si = lax.broadcasted_iota(jnp.int32, (L, L), 1)
    causal = li >= si
    below = li > si
    above = si > li
    lane_l = lax.broadcasted_iota(jnp.int32, (L, LANES), 1)

    def phase2(c, run_f):
        rows = pl.ds(pl.multiple_of(c * L, L), L)
        xb = xs_ref[0, rows, :]
        cm = c_ref[0, rows, :]
        cb = lax.dot_general(cm, b_ref[0, rows, :], (((1,), (1,)), ((), ())), preferred_element_type=F32)
        at = a_scr[c]
        dtt = dt_scr[c]
        acol = acol_scr[rows, :]
        ys = []
        for p in range(hpg // 2):
            xpair = xb[:, p * LANES:(p + 1) * LANES]
            y_pair = None
            for q in range(2):
                kf = 2 * p + q
                kb = hpg + kf
                seg = jnp.where(causal, acol[:, kf * L:(kf + 1) * L] - at[kf:kf + 1, :],
                                acol[:, kb * L:(kb + 1) * L] - at[kb:kb + 1, :])
                dt_f = dtt[kf:kf + 1, :]
                dt_b = dtt[kb:kb + 1, :]
                w = jnp.where(below, dt_f, jnp.where(above, dt_b, dt_f + dt_b))
                m = (cb * jnp.exp(seg) * w).astype(BF16)
                in_head = (lane_l < hd) if q == 0 else (lane_l >= hd)
                part = jnp.dot(m, jnp.where(in_head, xpair, jnp.zeros_like(xpair)),
                               preferred_element_type=F32)
                y_pair = part if y_pair is None else y_pair + part
            ys.append(y_pair)
        y = ys[0] if len(ys) == 1 else jnp.concatenate(ys, axis=1)
        ex = ex_scr[rows, :]
        cs_f = jnp.dot(cm, run_f.astype(BF16), preferred_element_type=F32)
        cs_b = jnp.dot(cm, st_scr[c, :, gw:2 * gw].astype(BF16), preferred_element_type=F32)
        y = y + cs_f * ex[:, 0:gw] + cs_b * ex[:, gw:2 * gw]
        y = y + xb.astype(F32) * dskip_ref[...]
        z = z_ref[0, rows, :].astype(F32)
        g = y * (z * jax.nn.sigmoid(z))
        g = g * lax.rsqrt(jnp.mean(g * g, axis=-1, keepdims=True) + EPS)
        o_ref[0, rows, :] = (g * nw_ref[...]).astype(o_ref.dtype)
        dec_f = ex[L - 1:L, 0:gw]
        return run_f * dec_f + st_scr[c, :, 0:gw]

    lax.fori_loop(0, nc, phase2, jnp.zeros((SSD_STATE, gw), F32), unroll=8)


def _ssd(xbc_act, proj3, dt_rows, bias_col, alog_col, dskip, norm_w, ds, z_col0):
    b, s, _ = xbc_act.shape
    g = SSD_GROUPS
    gw = ds // g
    hpg = gw // SSD_HEAD_DIM
    nc = s // SSD_CHUNK
    n = SSD_STATE
    L = SSD_CHUNK
    assert z_col0 % gw == 0 and ds % n == 0 and hpg % 2 == 0 and 2 * hpg <= SUBLANES
    kernel = functools.partial(_ssd_kernel, hpg=hpg)
    return pl.pallas_call(
        kernel,
        grid=(b, g),
        in_specs=[
            pl.BlockSpec((1, s, gw), lambda i, j: (i, 0, j)),
            pl.BlockSpec((1, s, n), lambda i, j: (i, 0, ds // n + j)),
            pl.BlockSpec((1, s, n), lambda i, j: (i, 0, ds // n + g + j)),
            pl.BlockSpec((1, s, gw), lambda i, j: (i, 0, z_col0 // gw + j)),
            pl.BlockSpec((SUBLANES, s), lambda i, j: (j, i)),
            pl.BlockSpec((1, SUBLANES, 1), lambda i, j: (j, 0, 0)),
            pl.BlockSpec((1, SUBLANES, 1), lambda i, j: (j, 0, 0)),
            pl.BlockSpec((1, gw), lambda i, j: (0, j)),
            pl.BlockSpec((1, gw), lambda i, j: (0, j)),
        ],
        out_specs=pl.BlockSpec((1, s, gw), lambda i, j: (i, 0, j)),
        out_shape=jax.ShapeDtypeStruct((b, s, ds), BF16),
        scratch_shapes=[
            pltpu.VMEM((nc, SUBLANES, L), F32),
            pltpu.VMEM((nc, SUBLANES, L), F32),
            pltpu.VMEM((s, 2 * gw), F32),
            pltpu.VMEM((s, 2 * gw), F32),
            pltpu.VMEM((s, 2 * hpg * L), F32),
            pltpu.VMEM((nc, n, 2 * gw), F32),
        ],
        compiler_params=_cparams("parallel", "parallel"),
        name="ssd",
    )(xbc_act, xbc_act, xbc_act, proj3, dt_rows, bias_col, alog_col, dskip, norm_w)


def _outproj_kernel(ya_ref, yb_ref, x_ref, wa_ref, wb_ref, nw_ref, wr_ref, br_ref,
                    x1_ref, h_ref, route_ref, cnt_ref, carry_scr, *, n_experts):
    i = pl.program_id(0)
    tm = x_ref.shape[0]

    @pl.when(i == 0)
    def _():
        carry_scr[...] = jnp.zeros_like(carry_scr)

    x1 = x_ref[...] + jnp.dot(ya_ref[...], wa_ref[...], preferred_element_type=F32) \
        + jnp.dot(yb_ref[...], wb_ref[...], preferred_element_type=F32)
    x1_ref[...] = x1
    h = x1 * lax.rsqrt(jnp.mean(x1 * x1, axis=-1, keepdims=True) + EPS) * nw_ref[...]
    h_ref[...] = h

    logits = jnp.dot(h.astype(BF16), wr_ref[...].astype(BF16), preferred_element_type=F32) + br_ref[...]

    lane = lax.broadcasted_iota(jnp.int32, (tm, LANES), 1)
    neg = jnp.finfo(F32).min
    work = jnp.where(lane < n_experts, logits, neg)
    tops, idxs, sels = [], [], []
    for _k in range(TOP_K):
        m = jnp.max(work, axis=-1, keepdims=True)
        idx = jnp.min(jnp.where(work == m, lane, LANES), axis=-1, keepdims=True)
        sel = lane == idx
        work = jnp.where(sel, neg, work)
        tops.append(m)
        idxs.append(idx)
        sels.append(sel)
    exps = [jnp.exp(t - tops[0]) for t in tops]
    denom = exps[0]
    for e in exps[1:]:
        denom = denom + e
    inv = 1.0 / denom

    onehot = jnp.zeros((tm, LANES), F32)
    for sel in sels:
        onehot = onehot + jnp.where(sel, 1.0, 0.0)
    ri = lax.broadcasted_iota(jnp.int32, (tm, tm), 0)
    ci = lax.broadcasted_iota(jnp.int32, (tm, tm), 1)
    tri = jnp.where(ri > ci, 1.0, 0.0).astype(BF16)
    carry = carry_scr[0:1, :]
    prefix = jnp.dot(tri, onehot.astype(BF16), preferred_element_type=F32) + carry
    new_carry = carry + jnp.sum(onehot, axis=0, keepdims=True)
    carry_scr[...] = jnp.broadcast_to(new_carry, carry_scr.shape)
    cnt_ref[...] = jnp.broadcast_to(new_carry, cnt_ref.shape)

    route = jnp.zeros((tm, LANES), F32)
    for k in range(TOP_K):
        rank = jnp.sum(jnp.where(sels[k], prefix, 0.0), axis=-1, keepdims=True)
        route = jnp.where(lane == k, idxs[k].astype(F32), route)
        route = jnp.where(lane == TOP_K + k, rank, route)
        route = jnp.where(lane == 2 * TOP_K + k, exps[k] * inv, route)
    route_ref[...] = route


def _outproj_route(ya, yb, x2, wa, wb, norm_w, w_router, b_router, n_experts):
    t, d = x2.shape
    dc = ya.shape[1]
    ds = yb.shape[1]
    tm = _largest_divisor(t, (512, 256, 128))
    kernel = functools.partial(_outproj_kernel, n_experts=n_experts)
    const = lambda shape: pl.BlockSpec(shape, lambda i: (0, 0))
    return pl.pallas_call(
        kernel,
        grid=(t // tm,),
        in_specs=[
            pl.BlockSpec((tm, dc), lambda i: (i, 0)),
            pl.BlockSpec((tm, ds), lambda i: (i, 0)),
            pl.BlockSpec((tm, d), lambda i: (i, 0)),
            const((dc, d)), const((ds, d)), const((1, d)), const((d, LANES)), const((1, LANES)),
        ],
        out_specs=[
            pl.BlockSpec((tm, d), lambda i: (i, 0)),
            pl.BlockSpec((tm, d), lambda i: (i, 0)),
            pl.BlockSpec((tm, LANES), lambda i: (i, 0)),
            pl.BlockSpec((8, LANES), lambda i: (0, 0)),
        ],
        out_shape=[
            jax.ShapeDtypeStruct((t, d), F32),
            jax.ShapeDtypeStruct((t, d), F32),
            jax.ShapeDtypeStruct((t, LANES), F32),
            jax.ShapeDtypeStruct((8, LANES), F32),
        ],
        scratch_shapes=[pltpu.VMEM((8, LANES), F32)],
        compiler_params=_cparams("arbitrary"),
        name="outproj_route",
    )(ya, yb, x2, wa, wb, norm_w, w_router, b_router)


ZERO_ROWS = 256


def _dispatch_kernel(zflag_ref, dest_ref, h_ref, o_ref, zbuf, sem, zsem, *, tm):
    tt = h_ref.shape[0]
    n_tiles = o_ref.shape[0] // tm
    per_tile = tm // ZERO_ROWS

    def zero_copy(i, p):
        return pltpu.make_async_copy(zbuf, o_ref.at[pl.ds(i * tm + p * ZERO_ROWS, ZERO_ROWS)], zsem)

    @pl.when(pl.program_id(0) == 0)
    def _():
        zbuf[...] = jnp.zeros_like(zbuf)

        def start(i, carry):
            @pl.when(zflag_ref[i] != 0)
            def _():
                for p in range(per_tile):
                    zero_copy(i, p).start()
            return carry

        def wait(i, carry):
            @pl.when(zflag_ref[i] != 0)
            def _():
                for p in range(per_tile):
                    zero_copy(i, p).wait()
            return carry

        lax.fori_loop(0, n_tiles, start, 0)
        lax.fori_loop(0, n_tiles, wait, 0)

    def issue(t, carry):
        for k in range(TOP_K):
            pltpu.make_async_copy(h_ref.at[pl.ds(t, 1)], o_ref.at[pl.ds(dest_ref[t * TOP_K + k], 1)],
                                  sem).start(priority=k % 2)
        return carry

    lax.fori_loop(0, tt, issue, 0, unroll=2)
    for _k in range(TOP_K):
        pltpu.make_async_copy(h_ref, o_ref.at[pl.ds(0, tt)], sem).wait()


def _dispatch(zflag, dest_flat, h, n_rows, tm):
    t, d = h.shape
    tt = _largest_divisor(t, (1024, 512, 256))
    assert tm % ZERO_ROWS == 0
    grid_spec = pltpu.PrefetchScalarGridSpec(
        num_scalar_prefetch=1,
        grid=(t // tt,),
        in_specs=[
            pl.BlockSpec((tt * TOP_K,), lambda i, zf: (i,), memory_space=pltpu.SMEM),
            pl.BlockSpec((tt, d), lambda i, zf: (i, 0)),
        ],
        out_specs=pl.BlockSpec(memory_space=pl.ANY),
        scratch_shapes=[pltpu.VMEM((ZERO_ROWS, d), h.dtype), pltpu.SemaphoreType.DMA(()),
                        pltpu.SemaphoreType.DMA(())],
    )
    return pl.pallas_call(
        functools.partial(_dispatch_kernel, tm=tm),
        grid_spec=grid_spec,
        out_shape=jax.ShapeDtypeStruct((n_rows, d), h.dtype),
        compiler_params=_cparams("arbitrary"),
        name="dispatch",
    )(zflag, dest_flat, h)


def _ring_copies(src, stage, sem, k, slot):
    _, rows, width = stage.shape
    w = src.shape[1]
    per = width // w
    return [pltpu.make_async_copy(src.at[pl.ds((k * per + p) * rows, rows)],
                                  stage.at[slot, :, p * w:(p + 1) * w], sem.at[slot])
            for p in range(per)]


def _ring_fill(src, stage, sem):
    for slot in range(stage.shape[0]):
        for c in _ring_copies(src, stage, sem, slot, slot):
            c.start()


def _load_cast(src, dst, stage, sem, ring_filled):
    n_slots, rows, width = stage.shape
    per = width // src.shape[1]
    w = src.shape[1]
    n = dst.shape[0] // (rows * per)
    copies = functools.partial(_ring_copies, src, stage, sem)

    @pl.when(jnp.logical_not(ring_filled))
    def _():
        _ring_fill(src, stage, sem)

    def body(kk, carry):
        for slot in range(n_slots):
            k = n_slots * kk + slot
            for c in copies(k, slot):
                c.wait()
            for p in range(per):
                row0 = pl.multiple_of((k * per + p) * rows, rows)
                dst[pl.ds(row0, rows), :] = stage[slot, :, p * w:(p + 1) * w].astype(BF16)

            @pl.when(k + n_slots < n)
            def _():
                for c in copies(k + n_slots, slot):
                    c.start()
        return carry

    lax.fori_loop(0, n // n_slots, body, 0)


def _expert_kernel(te_ref, nu_ref, nb_ref, x_ref, wgu_hbm, wd_hbm, bgu_ref, bd_ref, o_ref,
                   wgu_buf, wd_buf, xb_scr, act_scr, stage, sem, *, tf):
    i = pl.program_id(0)
    tm = x_ref.shape[0]
    f = wd_buf.shape[0]
    e = te_ref[i]
    active = i < nu_ref[0]

    @pl.when(active & ((i == 0) | (e != te_ref[jnp.maximum(i - 1, 0)])))
    def _():
        _load_cast(wgu_hbm.at[e], wgu_buf, stage, sem, ring_filled=i > 0)
        _load_cast(wd_hbm.at[e], wd_buf, stage, sem, ring_filled=False)

    e_next = te_ref[jnp.minimum(i + 1, pl.num_programs(0) - 1)]

    @pl.when((i + 1 < nu_ref[0]) & (e_next != e))
    def _():
        _ring_fill(wgu_hbm.at[e_next], stage, sem)

    def mlp(rows):
        xb_scr[0:rows, :] = x_ref[0:rows, :].astype(BF16)
        xb = xb_scr[0:rows, :]
        for j in range(f // tf):
            gcols = slice(j * tf, (j + 1) * tf)
            lcols = slice(f + j * tf, f + (j + 1) * tf)
            glu = jnp.dot(xb, wgu_buf[:, gcols], preferred_element_type=F32) + bgu_ref[:, gcols]
            lin = jnp.dot(xb, wgu_buf[:, lcols], preferred_element_type=F32) + bgu_ref[:, lcols]
            glu = jnp.minimum(glu, SWIGLU_LIMIT)
            lin = jnp.clip(lin, -SWIGLU_LIMIT, SWIGLU_LIMIT)
            act_scr[0:rows, gcols] = (glu * jax.nn.sigmoid(SWIGLU_ALPHA * glu) * (lin + 1.0)).astype(BF16)
        o_ref[0:rows, :] = (jnp.dot(act_scr[0:rows, :], wd_buf[...], preferred_element_type=F32)
                            + bd_ref[...]).astype(o_ref.dtype)
        if rows < tm:
            o_ref[rows:tm, :] = jnp.zeros((tm - rows, o_ref.shape[1]), o_ref.dtype)

    for blocks in range(1, tm // MOE_ROW_BLOCK + 1):
        pl.when(active & (nb_ref[i] == blocks))(functools.partial(mlp, blocks * MOE_ROW_BLOCK))

    @pl.when(jnp.logical_not(active))
    def _():
        o_ref[...] = jnp.zeros_like(o_ref)


WEIGHT_STAGE_ROWS = 64
WEIGHT_STAGE_SLOTS = 8


MOE_ROW_BLOCK = 128


def _experts(tile_e, n_used, tile_blocks, x_sorted, w_gu, b_gu, w_d, b_d, tm):
    n_rows, d = x_sorted.shape
    f = w_d.shape[1]
    chunk = WEIGHT_STAGE_ROWS * WEIGHT_STAGE_SLOTS
    assert (2 * f) % d == 0 and d % chunk == 0 and f % (chunk * (2 * f // d)) == 0
    assert tm % MOE_ROW_BLOCK == 0
    tf = _largest_divisor(f, (1024, 512, 256, 128))
    n_tiles = n_rows // tm
    grid_spec = pltpu.PrefetchScalarGridSpec(
        num_scalar_prefetch=3,
        grid=(n_tiles,),
        in_specs=[
            pl.BlockSpec((tm, d), lambda i, te, nu, nb: (jnp.minimum(i, nu[0] - 1), 0)),
            pl.BlockSpec(memory_space=pl.ANY),
            pl.BlockSpec(memory_space=pl.ANY),
            pl.BlockSpec((None, 1, 2 * f), lambda i, te, nu, nb: (te[i], 0, 0)),
            pl.BlockSpec((None, 1, d), lambda i, te, nu, nb: (te[i], 0, 0)),
        ],
        out_specs=pl.BlockSpec((tm, d), lambda i, te, nu, nb: (i, 0)),
        scratch_shapes=[
            pltpu.VMEM((d, 2 * f), BF16),
            pltpu.VMEM((f, d), BF16),
            pltpu.VMEM((tm, d), BF16),
            pltpu.VMEM((tm, f), BF16),
            pltpu.VMEM((WEIGHT_STAGE_SLOTS, WEIGHT_STAGE_ROWS, 2 * f), F32),
            pltpu.SemaphoreType.DMA((WEIGHT_STAGE_SLOTS,)),
        ],
    )
    return pl.pallas_call(
        functools.partial(_expert_kernel, tf=tf),
        grid_spec=grid_spec,
        out_shape=jax.ShapeDtypeStruct((n_rows, d), F32),
        compiler_params=_cparams("arbitrary"),
        name="experts",
    )(tile_e, n_used, tile_blocks, x_sorted, w_gu, w_d, b_gu, b_d)


def _combine_kernel(dest_ref, dest_next_ref, y_ref, x1_ref, route_ref, nw_ref, o_ref, buf, sem,
                    *, final_norm):
    i = pl.program_id(0)
    n = pl.num_programs(0)
    tt = x1_ref.shape[0]
    slot = i % 2

    def gather(idx_ref, dst_slot):
        def issue(t, carry):
            for k in range(TOP_K):
                pltpu.make_async_copy(y_ref.at[pl.ds(idx_ref[t * TOP_K + k], 1)],
                                      buf.at[dst_slot, pl.ds(k * tt + t, 1)],
                                      sem.at[dst_slot]).start(priority=k % 2)
            return carry

        lax.fori_loop(0, tt, issue, 0)

    @pl.when(i == 0)
    def _():
        gather(dest_ref, slot)

    @pl.when(i + 1 < n)
    def _():
        gather(dest_next_ref, 1 - slot)

    pltpu.make_async_copy(y_ref.at[pl.ds(0, TOP_K * tt)], buf.at[slot], sem.at[slot]).wait()

    route = route_ref[...]
    x = x1_ref[...]
    for k in range(TOP_K):
        gate = route[:, 2 * TOP_K + k:2 * TOP_K + k + 1]
        x = x + gate * buf[slot, k * tt:(k + 1) * tt, :]
    if final_norm:
        x = x * lax.rsqrt(jnp.mean(x * x, axis=-1, keepdims=True) + EPS) * nw_ref[...]
    o_ref[...] = x


def _combine(dest_flat, y_sorted, x1, route, norm_w, final_norm):
    t, d = x1.shape
    tt = _largest_divisor(t, (512, 256))
    n = t // tt
    return pl.pallas_call(
        functools.partial(_combine_kernel, final_norm=final_norm),
        grid=(n,),
        in_specs=[
            pl.BlockSpec((tt * TOP_K,), lambda i: (i,), memory_space=pltpu.SMEM),
            pl.BlockSpec((tt * TOP_K,), lambda i: (jnp.minimum(i + 1, n - 1),), memory_space=pltpu.SMEM),
            pl.BlockSpec(memory_space=pl.ANY),
            pl.BlockSpec((tt, d), lambda i: (i, 0)),
            pl.BlockSpec((tt, LANES), lambda i: (i, 0)),
            pl.BlockSpec((1, d), lambda i: (0, 0)),
        ],
        out_specs=pl.BlockSpec((tt, d), lambda i: (i, 0)),
        out_shape=jax.ShapeDtypeStruct((t, d), F32),
        scratch_shapes=[pltpu.VMEM((2, TOP_K * tt, d), F32), pltpu.SemaphoreType.DMA((2,))],
        compiler_params=_cparams("arbitrary"),
        name="combine",
    )(dest_flat, dest_flat, y_sorted, x1, route, norm_w)


MOE_ROW_TILE = 512


def _layer(x2, bsz, seq, norm_mix_w, w_in, conv_a_w, ssd_conv_w, ssd_conv_b, dt_bias_fw, dt_bias_bw,
           a_log_fw, a_log_bw, d_skip, ssd_norm_w, w_out, norm_ffn_w, w_router, b_router,
           w_gate_up, b_gate_up, w_down, b_down):
    t, d = x2.shape
    dc = conv_a_w.shape[1]
    ds = ssd_norm_w.shape[0]
    dxbc = ssd_conv_w.shape[1]
    heads = dt_bias_fw.shape[0]
    g = SSD_GROUPS
    hpg = heads // g
    n_main = 3 * dc + ds + dxbc
    n_experts = w_router.shape[1]

    w_all = w_in.astype(BF16)
    w_dt = jnp.transpose(w_in[:, n_main:].reshape(d, 2, g, hpg), (0, 2, 1, 3)).reshape(d, g, 2 * hpg)
    w_dt = jnp.pad(w_dt, ((0, 0), (0, 0), (0, SUBLANES - 2 * hpg))).reshape(d, g * SUBLANES)
    w_dt = jnp.pad(w_dt, ((0, 0), (0, LANES - g * SUBLANES))).astype(BF16)
    proj, dt_rows = _inproj(x2, norm_mix_w.reshape(1, d), w_all, n_main, w_dt)
    proj3 = proj.reshape(bsz, seq, n_main)
    y_a = _conv_a(proj3, conv_a_w, dc)
    xbc_act = _conv_ssd(proj3, ssd_conv_w, ssd_conv_b.reshape(1, dxbc), 3 * dc + ds, dxbc)

    def head_rows(fw, bw):
        both = jnp.concatenate([fw.reshape(g, hpg), bw.reshape(g, hpg)], axis=1)
        return jnp.pad(both, ((0, 0), (0, SUBLANES - 2 * hpg))).reshape(g, SUBLANES, 1)

    dskip = jnp.repeat(d_skip, SSD_HEAD_DIM).reshape(1, ds)
    y_b = _ssd(xbc_act, proj3, dt_rows, head_rows(dt_bias_fw, dt_bias_bw), head_rows(a_log_fw, a_log_bw),
               dskip, ssd_norm_w.reshape(1, ds), ds, 3 * dc)

    w_out_b = w_out.astype(BF16)
    w_r = jnp.pad(w_router, ((0, 0), (0, LANES - n_experts)))
    b_r = jnp.pad(b_router, (0, LANES - n_experts)).reshape(1, LANES)
    x1, h2, route, counts = _outproj_route(
        y_a.reshape(t, dc), y_b.reshape(t, ds), x2, w_out_b[:dc], w_out_b[dc:],
        norm_ffn_w.reshape(1, d), w_r, b_r, n_experts)

    tm = MOE_ROW_TILE
    n_slots = t * TOP_K
    n_tiles = n_slots // tm + n_experts
    cnt = counts[0, :n_experts].astype(jnp.int32)
    padded = (cnt + tm - 1) // tm * tm
    pends = jnp.cumsum(padded)
    pstarts = pends - padded
    e_idx = route[:, :TOP_K].astype(jnp.int32)
    rank = route[:, TOP_K:2 * TOP_K].astype(jnp.int32)
    experts = jnp.arange(n_experts, dtype=jnp.int32)
    dest = (jnp.sum((e_idx[..., None] == experts) * pstarts, axis=-1) + rank).reshape(n_slots)
    n_used = (pends[-1] // tm).astype(jnp.int32)
    tile_ids = jnp.arange(n_tiles, dtype=jnp.int32)
    tile_row0 = jnp.minimum(tile_ids, n_used - 1) * tm
    tile_e = jnp.minimum(jnp.sum((pends[None, :] <= tile_row0[:, None]).astype(jnp.int32), axis=1),
                         n_experts - 1)
    seg_end = jnp.sum((tile_e[:, None] == experts) * (pstarts + cnt), axis=-1)
    tile_valid = jnp.clip(seg_end - tile_row0, 1, tm)
    tile_blocks = ((tile_valid + MOE_ROW_BLOCK - 1) // MOE_ROW_BLOCK).astype(jnp.int32)
    partial_last = (pends - tm) * ((cnt % tm) != 0) - (cnt % tm == 0)
    zflag = (jnp.any(partial_last[None, :] == (tile_ids * tm)[:, None], axis=1)
             | (tile_ids >= n_used)).astype(jnp.int32)

    x_sorted = _dispatch(zflag, dest, h2, n_tiles * tm, tm)
    f = w_down.shape[1]
    y_sorted = _experts(tile_e, n_used.reshape(1), tile_blocks, x_sorted, w_gate_up,
                        b_gate_up.reshape(n_experts, 1, 2 * f), w_down,
                        b_down.reshape(n_experts, 1, d), tm)
    return dest, y_sorted, x1, route


def kernel(x, norm_mix_w, w_in, conv_a_w, ssd_conv_w, ssd_conv_b, dt_bias_fw, dt_bias_bw, a_log_fw,
           a_log_bw, d_skip, ssd_norm_w, w_out, norm_ffn_w, w_router, b_router, w_gate_up, b_gate_up,
           w_down, b_down, norm_final_w):
    bsz, seq, d = x.shape
    depth = w_in.shape[0]
    x2 = x.reshape(bsz * seq, d)
    for layer in range(depth):
        dest, y_sorted, x1, route = _layer(
            x2, bsz, seq, norm_mix_w[layer], w_in[layer], conv_a_w[layer], ssd_conv_w[layer],
            ssd_conv_b[layer], dt_bias_fw[layer], dt_bias_bw[layer], a_log_fw[layer], a_log_bw[layer],
            d_skip[layer], ssd_norm_w[layer], w_out[layer], norm_ffn_w[layer], w_router[layer],
            b_router[layer], w_gate_up[layer], b_gate_up[layer], w_down[layer], b_down[layer])
        x2 = _combine(dest, y_sorted, x1, route, norm_final_w.reshape(1, d), layer == depth - 1)
    return x2.reshape(bsz, seq, d)
```
